```python
import math
import jax
import jax.numpy as jnp
from jax import lax
import numpy as np

D_MODEL = 1024
BATCH = 2
SEQ = 8192
DEPTH = 2

GRID_W = 64
CTX_LEN = 256
N_MIXERS = 2
N_HG_HEADS = 8
HG_HEAD_DIM = D_MODEL // N_HG_HEADS
HG_CHUNK = 64
SGU_WIDTH = D_MODEL
SGU_CHUNK = 128
SGU_GROUPS = 8
SGU_GROUP_DIM = SGU_WIDTH // SGU_GROUPS
N_EXPERTS = 16
EC_CAPACITY_FACTOR = 2
EXPERT_FF = 2 * D_MODEL
N_ADA = 6
EPS = 1e-6
N_HGRN_LAYERS = (DEPTH + 1) // 2
N_SGU_LAYERS = DEPTH // 2

kernel_name = 'hybrid_hgrn2_sgu_ecmoe_dit'


def _rmsnorm(x, w):
    xf = x.astype(jnp.float32)
    y = xf * lax.rsqrt(jnp.mean(xf * xf, axis=-1, keepdims=True) + EPS)
    return (y * w.astype(jnp.float32)).astype(x.dtype)


def _layernorm(x, w, b):
    xf = x.astype(jnp.float32)
    mu = jnp.mean(xf, axis=-1, keepdims=True)
    xc = xf - mu
    y = xc * lax.rsqrt(jnp.mean(xc * xc, axis=-1, keepdims=True) + EPS)
    return (y * w.astype(jnp.float32) + b.astype(jnp.float32)).astype(x.dtype)


def _modulate(h, shift, scale):
    return h * (1 + scale) + shift


def _sincos_2d(rows, dim):
    quarter = dim // 4
    half = dim // 2
    freqs = jnp.exp(-math.log(10000.0) * jnp.arange(quarter, dtype=jnp.float32) / quarter)

    def emb1d(n):
        ang = jnp.arange(n, dtype=jnp.float32)[:, None] * freqs[None, :]
        return jnp.concatenate([jnp.sin(ang), jnp.cos(ang)], axis=-1)

    er = emb1d(rows)
    ec = emb1d(GRID_W)
    pos = jnp.concatenate([jnp.broadcast_to(er[:, None, :], (rows, GRID_W, half)),
                           jnp.broadcast_to(ec[None, :, :], (rows, GRID_W, half))], axis=-1)
    return pos.reshape(rows * GRID_W, dim)


def _to_chunks(t):
    b, n, h, d = t.shape
    return t.reshape(b, n // HG_CHUNK, HG_CHUNK, h, d).transpose(1, 0, 3, 2, 4)


def _from_chunks(t):
    n, b, h, c, d = t.shape
    return t.transpose(1, 0, 3, 2, 4).reshape(b, n * c, h, d)


def _gla_scan(k, v, log_f, s0, q=None):
    tri = jnp.tril(jnp.ones((HG_CHUNK, HG_CHUNK), dtype=bool))[:, :, None]
    xs = (_to_chunks(k), _to_chunks(v), _to_chunks(log_f))
    if q is not None:
        xs = xs + (_to_chunks(q),)

    def body(state, inp):
        kc, vc, gc = inp[0], inp[1], inp[2]
        b = jnp.cumsum(gc, axis=2)
        b_last = b[:, :, -1:, :]
        new_state = (jnp.exp(b_last[:, :, 0, :])[..., None] * state
                     + jnp.einsum('bhsk,bhsv->bhkv', kc * jnp.exp(b_last - b), vc))
        if q is None:
            return new_state, None
        qc = inp[3]
        o_inter = jnp.einsum('bhtk,bhkv->bhtv', qc * jnp.exp(b), state)
        diff = b[:, :, :, None, :] - b[:, :, None, :, :]
        decay = jnp.exp(jnp.where(tri, diff, -jnp.inf))
        scores = jnp.einsum('bhtk,bhsk,bhtsk->bhts', qc, kc, decay)
        return new_state, o_inter + jnp.einsum('bhts,bhsv->bhtv', scores, vc)

    final_state, o = lax.scan(body, s0, xs)
    return (None if q is None else _from_chunks(o)), final_state


def _forget(f_raw, lb):
    f = lb + (1 - lb) * jax.nn.sigmoid(f_raw)
    return 1 - f, jnp.log(f)


def _hgrn2_bidir(pr, lb, s0_f, s0_b, with_output):
    bsz, t = pr.shape[0], pr.shape[1]
    d = D_MODEL

    def heads(a):
        return a.reshape(bsz, t, N_HG_HEADS, HG_HEAD_DIM).astype(jnp.float32)

    def flip(a):
        return jnp.flip(a, axis=1)

    v = heads(pr[..., 0:d])
    k_f, lf_f = _forget(heads(pr[..., d:2 * d]), lb[0])
    k_b, lf_b = _forget(heads(pr[..., 2 * d:3 * d]), lb[1])
    q = jax.nn.silu(heads(pr[..., 3 * d:4 * d])) if with_output else None
    o_f, s_f = _gla_scan(k_f, v, lf_f, s0_f, q)
    o_b, s_b = _gla_scan(flip(k_b), flip(v), flip(lf_b), s0_b, None if q is None else flip(q))
    o = (o_f + flip(o_b)) if with_output else None
    return o, s_f, s_b


def _hgrn2_readout(o, g, gnorm_w, w_out, dtype):
    bsz, t = o.shape[0], o.shape[1]
    gw = gnorm_w.astype(jnp.float32).reshape(N_HG_HEADS, HG_HEAD_DIM)
    o = o * lax.rsqrt(jnp.mean(o * o, axis=-1, keepdims=True) + EPS) * gw
    o = o.reshape(bsz, t, D_MODEL).astype(dtype) * jax.nn.silu(g)
    return o @ w_out


def _sgu_mixer(h, w_in, ln_w, ln_b, w_s, b_s, w_out):
    bsz, t, _ = h.shape
    z = jax.nn.gelu(h @ w_in)
    u, v = jnp.split(z, 2, axis=-1)
    v = _layernorm(v, ln_w, ln_b)
    vb = v.reshape(bsz, t // SGU_CHUNK, SGU_CHUNK, SGU_GROUPS, SGU_GROUP_DIM)
    mixed = jnp.einsum('gpq,bnqgc->bnpgc', w_s, vb) + b_s.T[:, :, None]
    return (u * mixed.reshape(bsz, t, SGU_WIDTH)) @ w_out


def _ec_moe(h, router, w_gate, w_up, w_down):
    bsz, t, d = h.shape
    cap = EC_CAPACITY_FACTOR * t // N_EXPERTS
    aff = jax.nn.softmax((h @ router).astype(jnp.float32), axis=-1)
    gate, idx = lax.top_k(jnp.swapaxes(aff, 1, 2), cap)
    xs = jax.vmap(lambda hb, ib: hb[ib])(h, idx)
    hid = jax.nn.silu(jnp.einsum('becd,edf->becf', xs, w_gate)) * jnp.einsum('becd,edf->becf', xs, w_up)
    y = jnp.einsum('becf,efd->becd', hid, w_down) * gate[..., None].astype(h.dtype)
    return jax.vmap(lambda yb, ib: jnp.zeros((t, d), y.dtype).at[ib.reshape(-1)].add(yb.reshape(-1, d)))(y, idx)


def setup_inputs(seed: int = 0) -> dict:
    key = jax.random.key(seed)
    ks = jax.random.split(key, 21)
    d = D_MODEL

    def nrm(k, shape, s):
        return jax.random.normal(k, shape, jnp.float32) * s

    return {
        'x': nrm(ks[0], (BATCH, SEQ, d), 1.0),
        'c': nrm(ks[1], (BATCH, d), 1.0),
        'ctx': nrm(ks[2], (BATCH, CTX_LEN, d), 1.0),
        'c_ctx': nrm(ks[3], (d,), 1.0),
        'w_ada': nrm(ks[4], (DEPTH, d, N_ADA * d), 0.5 * d ** -0.5),
        'b_ada': nrm(ks[5], (DEPTH, N_ADA * d), 0.01),
        'norm_w': 1.0 + nrm(ks[6], (DEPTH, 4, d), 0.02),
        'hg_w_in': nrm(ks[7], (N_HGRN_LAYERS, d, 5 * d), d ** -0.5),
        'hg_lb': nrm(ks[8], (DEPTH + 1, 2, d), 0.5),
        'hg_gnorm': 1.0 + nrm(ks[9], (N_HGRN_LAYERS, d), 0.02),
        'hg_w_out': nrm(ks[10], (N_HGRN_LAYERS, d, d), d ** -0.5),
        'sg_w_in': nrm(ks[11], (N_SGU_LAYERS, d, 2 * SGU_WIDTH), d ** -0.5),
        'sg_ln_w': 1.0 + nrm(ks[12], (N_SGU_LAYERS, SGU_WIDTH), 0.02),
        'sg_ln_b': nrm(ks[13], (N_SGU_LAYERS, SGU_WIDTH), 0.02),
        'sg_w_s': nrm(ks[14], (N_SGU_LAYERS, SGU_GROUPS, SGU_CHUNK, SGU_CHUNK), SGU_CHUNK ** -0.5),
        'sg_b_s': 1.0 + nrm(ks[15], (N_SGU_LAYERS, SGU_GROUPS, SGU_CHUNK), 0.1),
        'sg_w_out': nrm(ks[16], (N_SGU_LAYERS, SGU_WIDTH, d), SGU_WIDTH ** -0.5),
        'moe_router': nrm(ks[17], (DEPTH, d, N_EXPERTS), d ** -0.5),
        'moe_w_gate': nrm(ks[18], (DEPTH, N_EXPERTS, d, EXPERT_FF), d ** -0.5),
        'moe_w_up': nrm(ks[19], (DEPTH, N_EXPERTS, d, EXPERT_FF), d ** -0.5),
        'moe_w_down': nrm(ks[20], (DEPTH, N_EXPERTS, EXPERT_FF, d), EXPERT_FF ** -0.5),
    }


def reference(x, c, ctx, c_ctx, w_ada, b_ada, norm_w, hg_w_in, hg_lb, hg_gnorm, hg_w_out,
              sg_w_in, sg_ln_w, sg_ln_b, sg_w_s, sg_b_s, sg_w_out,
              moe_router, moe_w_gate, moe_w_up, moe_w_down):
    bsz, seq_len, d = x.shape
    ROWS = seq_len // GRID_W
    x = x + _sincos_2d(ROWS, d).astype(x.dtype)[None]
    lb_all = jnp.cumsum(jax.nn.softmax(hg_lb.astype(jnp.float32), axis=0), axis=0)
    is_hgrn = [i % N_MIXERS == 0 for i in range(DEPTH)]
    ctx_needed = [any(is_hgrn[j:]) for j in range(DEPTH + 1)]
    silu_c = jax.nn.silu(c)
    silu_cc = jax.nn.silu(c_ctx)
    for i in range(DEPTH):
        j = i // N_MIXERS
        upd_ctx = ctx_needed[i + 1]
        mod = jnp.split(silu_c @ w_ada[i] + b_ada[i], N_ADA, axis=-1)
        sh1, sc1, g1, sh2, sc2, g2 = [m[:, None, :] for m in mod]
        sh1c, sc1c, g1c, sh2c, sc2c, g2c = jnp.split(silu_cc @ w_ada[i] + b_ada[i], N_ADA, axis=-1)

        h = _modulate(_rmsnorm(x, norm_w[i, 0]), sh1, sc1)
        if is_hgrn[i]:
            lb = lb_all[i].reshape(2, N_HG_HEADS, HG_HEAD_DIM)
            hc = _modulate(_rmsnorm(ctx, norm_w[i, 0]), sh1c, sc1c)
            w_c = hg_w_in[j] if upd_ctx else hg_w_in[j][:, :3 * d]
            pc = hc @ w_c
            s0 = jnp.zeros((bsz, N_HG_HEADS, HG_HEAD_DIM, HG_HEAD_DIM), jnp.float32)
            o_c, s_f, s_b = _hgrn2_bidir(pc, lb, s0, s0, upd_ctx)
            pr = h @ hg_w_in[j]
            o, _, _ = _hgrn2_bidir(pr, lb, s_f, s_b, True)
            y = _hgrn2_readout(o, pr[..., 4 * d:], hg_gnorm[j], hg_w_out[j], x.dtype)
            if upd_ctx:
                yc = _hgrn2_readout(o_c, pc[..., 4 * d:], hg_gnorm[j], hg_w_out[j], ctx.dtype)
        else:
            y = _sgu_mixer(h, sg_w_in[j], sg_ln_w[j], sg_ln_b[j], sg_w_s[j], sg_b_s[j], sg_w_out[j])
            if upd_ctx:
                hc = _modulate(_rmsnorm(ctx, norm_w[i, 0]), sh1c, sc1c)
                yc = _sgu_mixer(hc, sg_w_in[j], sg_ln_w[j], sg_ln_b[j], sg_w_s[j], sg_b_s[j], sg_w_out[j])
        x = x + g1 * _rmsnorm(y, norm_w[i, 1])
        if upd_ctx:
            ctx = ctx + g1c * _rmsnorm(yc, norm_w[i, 1])

        h = _modulate(_rmsnorm(x, norm_w[i, 2]), sh2, sc2)
        y = _ec_moe(h, moe_router[i], moe_w_gate[i], moe_w_up[i], moe_w_down[i])
        x = x + g2 * _rmsnorm(y, norm_w[i, 3])
        if upd_ctx:
            hc = _modulate(_rmsnorm(ctx, norm_w[i, 2]), sh2c, sc2c)
            yc = _ec_moe(hc, moe_router[i], moe_w_gate[i], moe_w_up[i], moe_w_down[i])
            ctx = ctx + g2c * _rmsnorm(yc, norm_w[i, 3])
    return x
```

```python
import functools
import math

import jax
import jax.numpy as jnp
from jax import lax
from jax.experimental import pallas as pl
from jax.experimental.pallas import tpu as pltpu

F32 = jnp.float32
BF16 = jnp.bfloat16
I32 = jnp.int32
HIGHEST = lax.Precision.HIGHEST

EPS = 1e-6
GRID_W = 64
N_ADA = 6
N_HEADS = 8
HEAD_DIM = 128
N_EXPERTS = 16
EC_CAPACITY_FACTOR = 2
SGU_CHUNK = 128
SGU_GROUPS = 8

LANES = 128
SUBLANES = 8
GLA_CHUNK = 128
GLA_LEVELS = (64, 32, 16, 8, 4, 2, 1)
BISECT_ITERS = 160
VMEM_LIMIT = 52 * 1024 * 1024


def _cparams(sem):
    return pltpu.CompilerParams(dimension_semantics=sem, vmem_limit_bytes=VMEM_LIMIT)


def _rms(x, w):
    ms = jnp.mean(x * x, axis=-1, keepdims=True)
    return x * lax.rsqrt(ms + EPS) * w


def _dot(a, b):
    return jnp.dot(a, b, preferred_element_type=F32)


def _dot_nt(a, b):
    return lax.dot_general(a, b, (((1,), (1,)), ((), ())), preferred_element_type=F32)


def _dot_tn(a, b):
    return lax.dot_general(a, b, (((0,), (0,)), ((), ())), preferred_element_type=F32)


def _ada_kernel(c_ref, w_ref, b_ref, o_ref):
    c = c_ref[...]
    s = c * jax.nn.sigmoid(c)
    o_ref[0] = jnp.dot(s, w_ref[0], precision=HIGHEST, preferred_element_type=F32) + b_ref[0]


def _ada(cvec, w_ada, b_ada):
    depth, d, nd = w_ada.shape
    rows = cvec.shape[0]
    return pl.pallas_call(
        _ada_kernel,
        grid=(depth, nd // d),
        in_specs=[pl.BlockSpec((rows, d), lambda l, n: (0, 0)),
                  pl.BlockSpec((1, d, d), lambda l, n: (l, 0, n)),
                  pl.BlockSpec((1, 1, d), lambda l, n: (l, 0, n))],
        out_specs=pl.BlockSpec((1, rows, d), lambda l, n: (l, 0, n)),
        out_shape=jax.ShapeDtypeStruct((depth, rows, nd), F32),
        compiler_params=_cparams(("arbitrary", "arbitrary")),
        name="ada",
    )(cvec, w_ada, b_ada.reshape(depth, 1, nd))


def _hgin_body(x, nw_ref, sh_ref, sc_ref, w_ref, lb_ref, outs, n_lb):
    v_ref, kf_ref, gf_ref, kb_ref, gb_ref, q_ref, sg_ref = outs
    d = x.shape[-1]
    h = _rms(x, nw_ref[...]) * (1.0 + sc_ref[0]) + sh_ref[0]
    hb = h.astype(BF16)
    lbs = lb_ref[...]
    e = jnp.exp(lbs - jnp.max(lbs, axis=0, keepdims=True))
    lb = jnp.sum(e[:n_lb], axis=0) / jnp.sum(e, axis=0)

    v_ref[0] = _dot(hb, w_ref[:, 0:d]).astype(BF16)
    for j, (k_ref, g_ref) in enumerate(((kf_ref, gf_ref), (kb_ref, gb_ref))):
        raw = _dot(hb, w_ref[:, (1 + j) * d:(2 + j) * d])
        lbj = lb[j:j + 1]
        sig = jax.nn.sigmoid(raw)
        f = lbj + (1.0 - lbj) * sig
        k_ref[0] = ((1.0 - lbj) * (1.0 - sig)).astype(BF16)
        g_ref[0] = jnp.log(f)
    qr = _dot(hb, w_ref[:, 3 * d:4 * d])
    q_ref[0] = (qr * jax.nn.sigmoid(qr)).astype(BF16)
    gr = _dot(hb, w_ref[:, 4 * d:5 * d])
    sg_ref[0] = (gr * jax.nn.sigmoid(gr)).astype(BF16)


def _hgin_pos_kernel(x_ref, pos_ref, nw_ref, sh_ref, sc_ref, w_ref, lb_ref, xp_ref, *outs, n_lb):
    x = x_ref[0] + pos_ref[...]
    xp_ref[0] = x
    _hgin_body(x, nw_ref, sh_ref, sc_ref, w_ref, lb_ref, outs, n_lb)


def _hgin_kernel(x_ref, nw_ref, sh_ref, sc_ref, w_ref, lb_ref, *outs, n_lb):
    _hgin_body(x_ref[0], nw_ref, sh_ref, sc_ref, w_ref, lb_ref, outs, n_lb)


def _hg_in(x, pos, nw, sh, sc, w_bf, lb_raw, n_lb, tm):
    bsz, t, d = x.shape
    tm = min(tm, t)
    tok = pl.BlockSpec((1, tm, d), lambda b, i: (b, i, 0))
    vec = pl.BlockSpec((1, d), lambda b, i: (0, 0))
    bvec = pl.BlockSpec((1, 1, d), lambda b, i: (b, 0, 0))
    wspec = pl.BlockSpec(w_bf.shape, lambda b, i: (0, 0))
    lbspec = pl.BlockSpec(lb_raw.shape, lambda b, i: (0, 0, 0))
    gate_shapes = [jax.ShapeDtypeStruct((bsz, t, d), dt) for dt in (BF16, BF16, F32, BF16, F32, BF16, BF16)]
    if pos is not None:
        kern = functools.partial(_hgin_pos_kernel, n_lb=n_lb)
        in_specs = [tok, pl.BlockSpec((tm, d), lambda b, i: (i, 0)), vec, bvec, bvec, wspec, lbspec]
        args = (x, pos, nw, sh, sc, w_bf, lb_raw)
        out_shape = [jax.ShapeDtypeStruct((bsz, t, d), F32)] + gate_shapes
    else:
        kern = functools.partial(_hgin_kernel, n_lb=n_lb)
        in_specs = [tok, vec, bvec, bvec, wspec, lbspec]
        args = (x, nw, sh, sc, w_bf, lb_raw)
        out_shape = gate_shapes
    return pl.pallas_call(
        kern,
        grid=(bsz, t // tm),
        in_specs=in_specs,
        out_specs=[tok] * len(out_shape),
        out_shape=out_shape,
        compiler_params=_cparams(("arbitrary", "arbitrary")),
        name="hg_in",
    )(*args)


def _gla_consts():
    c = GLA_CHUNK
    t = jnp.arange(c)[:, None]
    s = jnp.arange(c)[None, :]
    tri = jnp.stack([(s <= t), (s >= t)]).astype(BF16)
    masks = []
    for rev in (False, True):
        lv = []
        for m in GLA_LEVELS:
            same = (t // (2 * m)) == (s // (2 * m))
            tq = ((t // m) % 2) == (0 if rev else 1)
            sk = ((s // m) % 2) == (1 if rev else 0)
            lv.append(same & tq & sk)
        lv.append(t == s)
        masks.append(jnp.stack(lv))
    return tri, jnp.stack(masks).astype(F32)


def _split3(x):
    p0 = x.astype(BF16)
    r1 = x - p0.astype(F32)
    p1 = r1.astype(BF16)
    p2 = (r1 - p1.astype(F32)).astype(BF16)
    return p0, p1, p2


def _level_ref(b_scr, hs, m, rev):
    c = GLA_CHUNK
    off = m if rev else m - 1

    def row8(i):
        return jnp.broadcast_to(b_scr[pl.ds(i, 1), hs], (SUBLANES, HEAD_DIM))

    pieces = []
    if m >= SUBLANES:
        for blk in range(c // (2 * m)):
            r8 = row8(blk * 2 * m + off)
            pieces.extend([r8] * (2 * m // SUBLANES))
    else:
        sub = lax.broadcasted_iota(I32, (SUBLANES, HEAD_DIM), 0) // (2 * m)
        for grp in range(c // SUBLANES):
            piece = row8(grp * SUBLANES + off)
            for cls in range(1, SUBLANES // (2 * m)):
                piece = jnp.where(sub == cls, row8(grp * SUBLANES + cls * 2 * m + off), piece)
            pieces.append(piece)
    return jnp.concatenate(pieces, axis=0)


def _gla_direction(q_ref, k_ref, v_ref, g_ref, o_ref, tri_ref, mask_ref, st_scr, b_scr, d_idx, rev):
    c = GLA_CHUNK
    g = g_ref[0]
    tri = tri_ref[d_idx]
    p0, p1, p2 = _split3(g)
    b_scr[...] = _dot(tri, p0) + _dot(tri, p1) + _dot(tri, p2)
    row = lax.broadcasted_iota(I32, (c, HEAD_DIM), 0)

    def head(h, carry):
        hs = pl.ds(pl.multiple_of(h * HEAD_DIM, HEAD_DIM), HEAD_DIM)
        b = b_scr[:, hs]
        q = q_ref[0, :, hs].astype(F32)
        k = k_ref[0, :, hs].astype(F32)
        vb = v_ref[0, :, hs]
        bl = b_scr[pl.ds(0 if rev else c - 1, 1), hs]
        qt = (q * jnp.exp(b)).astype(BF16)
        kt = (k * jnp.exp(bl - b)).astype(BF16)
        st = st_scr[d_idx, h]
        o = _dot_nt(qt, st.astype(BF16))
        a = mask_ref[d_idx, len(GLA_LEVELS)] * _dot_nt(q.astype(BF16), k.astype(BF16))
        for li, m in enumerate(GLA_LEVELS):
            r = _level_ref(b_scr, hs, m, rev)
            qside = ((row // m) % 2) == (0 if rev else 1)
            diff = b - r
            e = jnp.exp(jnp.where(qside, diff, -diff))
            a = a + mask_ref[d_idx, li] * _dot_nt((q * e).astype(BF16), (k * e).astype(BF16))
        o = o + _dot(a.astype(BF16), vb)
        o_ref[0, :, hs] = o
        st_scr[d_idx, h] = st * jnp.exp(bl) + _dot_tn(vb, kt)
        return carry

    lax.fori_loop(0, N_HEADS, head, 0)


def _gla_kernel(kf_ref, gf_ref, vf_ref, qf_ref, kb_ref, gb_ref, vb_ref, qb_ref, s0_ref, tri_ref, mask_ref,
                of_ref, ob_ref, sout_ref, st_scr, bf_scr, bb_scr):
    n = pl.program_id(1)

    @pl.when(n == 0)
    def _():
        st_scr[...] = s0_ref[0]

    _gla_direction(qf_ref, kf_ref, vf_ref, gf_ref, of_ref, tri_ref, mask_ref, st_scr, bf_scr, 0, False)
    _gla_direction(qb_ref, kb_ref, vb_ref, gb_ref, ob_ref, tri_ref, mask_ref, st_scr, bb_scr, 1, True)

    @pl.when(n == pl.num_programs(1) - 1)
    def _():
        sout_ref[0] = st_scr[...]


def _gla(kf, gf, kb, gb, v, q, s0, tri, masks):
    bsz, t, d = v.shape
    c = GLA_CHUNK
    n = t // c
    fwd = pl.BlockSpec((1, c, d), lambda b, i: (b, i, 0))
    bwd = pl.BlockSpec((1, c, d), lambda b, i: (b, n - 1 - i, 0))
    sspec = pl.BlockSpec((1,) + s0.shape[1:], lambda b, i: (b, 0, 0, 0, 0))
    return pl.pallas_call(
        _gla_kernel,
        grid=(bsz, n),
        in_specs=[fwd, fwd, fwd, fwd, bwd, bwd, bwd, bwd, sspec,
                  pl.BlockSpec(tri.shape, lambda b, i: (0, 0, 0)),
                  pl.BlockSpec(masks.shape, lambda b, i: (0, 0, 0, 0))],
        out_specs=[fwd, bwd, sspec],
        out_shape=[jax.ShapeDtypeStruct((bsz, t, d), F32), jax.ShapeDtypeStruct((bsz, t, d), F32),
                   jax.ShapeDtypeStruct(s0.shape, F32)],
        scratch_shapes=[pltpu.VMEM(s0.shape[1:], F32), pltpu.VMEM((c, d), F32), pltpu.VMEM((c, d), F32)],
        compiler_params=_cparams(("arbitrary", "arbitrary")),
        name="gla",
    )(kf, gf, v, q, kb, gb, v, q, s0, tri, masks)


def _post_mixer(y, xres, nw1_ref, g1_ref, nw2_ref, sh2_ref, sc2_ref, rt_ref, x1_ref, h2_ref, lg_ref):
    x1 = xres + g1_ref[0] * _rms(y, nw1_ref[...])
    x1_ref[0] = x1
    h2 = _rms(x1, nw2_ref[...]) * (1.0 + sc2_ref[0]) + sh2_ref[0]
    h2_ref[0] = h2
    lg_ref[0] = lax.dot_general(rt_ref[...], h2, (((1,), (1,)), ((), ())),
                                precision=HIGHEST, preferred_element_type=F32)


def _hgout_kernel(of_ref, ob_ref, sg_ref, gn_ref, w_ref, xp_ref, nw1_ref, g1_ref, nw2_ref, sh2_ref, sc2_ref,
                  rt_ref, x1_ref, h2_ref, lg_ref, z_scr):
    o = of_ref[0] + ob_ref[0]
    for h in range(N_HEADS):
        hs = slice(h * HEAD_DIM, (h + 1) * HEAD_DIM)
        oh = o[:, hs]
        ms = jnp.mean(oh * oh, axis=-1, keepdims=True)
        z = oh * lax.rsqrt(ms + EPS) * gn_ref[:, hs] * sg_ref[0, :, hs].astype(F32)
        z_scr[:, hs] = z.astype(BF16)
    y = _dot(z_scr[...], w_ref[...])
    _post_mixer(y, xp_ref[0], nw1_ref, g1_ref, nw2_ref, sh2_ref, sc2_ref, rt_ref, x1_ref, h2_ref, lg_ref)


def _hg_out(o_f, o_b, sg, gnorm, w_out_bf, xp, nw1, g1, nw2, sh2, sc2, router_t, tm):
    bsz, t, d = xp.shape
    ne = router_t.shape[0]
    tok = pl.BlockSpec((1, tm, d), lambda b, i: (b, i, 0))
    vec = pl.BlockSpec((1, d), lambda b, i: (0, 0))
    bvec = pl.BlockSpec((1, 1, d), lambda b, i: (b, 0, 0))
    return pl.pallas_call(
        _hgout_kernel,
        grid=(bsz, t // tm),
        in_specs=[tok, tok, tok, vec, pl.BlockSpec((d, d), lambda b, i: (0, 0)), tok, vec, bvec, vec, bvec, bvec,
                  pl.BlockSpec((ne, d), lambda b, i: (0, 0))],
        out_specs=[tok, tok, pl.BlockSpec((1, ne, tm), lambda b, i: (b, 0, i))],
        out_shape=[jax.ShapeDtypeStruct((bsz, t, d), F32), jax.ShapeDtypeStruct((bsz, t, d), F32),
                   jax.ShapeDtypeStruct((bsz, ne, t), F32)],
        scratch_shapes=[pltpu.VMEM((tm, d), BF16)],
        compiler_params=_cparams(("arbitrary", "arbitrary")),
        name="hg_out",
    )(o_f, o_b, sg, gnorm, w_out_bf, xp, nw1, g1, nw2, sh2, sc2, router_t)


def _sgu_kernel(x1p_ref, moe_ref, nw3p_ref, g2p_ref, nw0_ref, sh1_ref, sc1_ref, win_ref, lnw_ref, lnb_ref,
                ws_ref, bs_ref, wout_ref, nw1_ref, g1_ref, nw2_ref, sh2_ref, sc2_ref, rt_ref,
                x1_ref, h2_ref, lg_ref, m_scr):
    tm, d = x1p_ref.shape[1], x1p_ref.shape[2]
    w = lnw_ref.shape[-1]
    gd = w // SGU_GROUPS
    x = x1p_ref[0] + g2p_ref[0] * _rms(moe_ref[0], nw3p_ref[...])
    h = _rms(x, nw0_ref[...]) * (1.0 + sc1_ref[0]) + sh1_ref[0]
    hb = h.astype(BF16)

    def gelu(z):
        return 0.5 * z * (1.0 + jnp.tanh(math.sqrt(2.0 / math.pi) * (z + 0.044715 * (z * z * z))))

    u = gelu(_dot(hb, win_ref[:, 0:w]))
    v = gelu(_dot(hb, win_ref[:, w:2 * w]))
    mu = jnp.mean(v, axis=-1, keepdims=True)
    vc = v - mu
    vn = vc * lax.rsqrt(jnp.mean(vc * vc, axis=-1, keepdims=True) + EPS) * lnw_ref[...] + lnb_ref[...]
    vnb = vn.astype(BF16)
    for ck in range(tm // SGU_CHUNK):
        rs = slice(ck * SGU_CHUNK, (ck + 1) * SGU_CHUNK)
        for g in range(SGU_GROUPS):
            cs = slice(g * gd, (g + 1) * gd)
            mixed = _dot(ws_ref[g], vnb[rs, cs]) + bs_ref[:, g:g + 1]
            m_scr[rs, cs] = (u[rs, cs] * mixed).astype(BF16)
    y = _dot(m_scr[...], wout_ref[...])
    _post_mixer(y, x, nw1_ref, g1_ref, nw2_ref, sh2_ref, sc2_ref, rt_ref, x1_ref, h2_ref, lg_ref)


def _sgu(x1p, moe, nw3p, g2p, nw0, sh1, sc1, w_in_bf, ln_w, ln_b, w_s_bf, b_s_t, w_out_bf,
         nw1, g1, nw2, sh2, sc2, router_t, tm):
    bsz, t, d = x1p.shape
    ne = router_t.shape[0]
    w = ln_w.shape[-1]
    tok = pl.BlockSpec((1, tm, d), lambda b, i: (b, i, 0))
    vec = pl.BlockSpec((1, d), lambda b, i: (0, 0))
    wvec = pl.BlockSpec((1, w), lambda b, i: (0, 0))
    bvec = pl.BlockSpec((1, 1, d), lambda b, i: (b, 0, 0))

    def full(a):
        return pl.BlockSpec(a.shape, lambda b, i: (0,) * a.ndim)

    return pl.pallas_call(
        _sgu_kernel,
        grid=(bsz, t // tm),
        in_specs=[tok, tok, vec, bvec, vec, bvec, bvec, full(w_in_bf), wvec, wvec, full(w_s_bf), full(b_s_t),
                  full(w_out_bf), vec, bvec, vec, bvec, bvec, full(router_t)],
        out_specs=[tok, tok, pl.BlockSpec((1, ne, tm), lambda b, i: (b, 0, i))],
        out_shape=[jax.ShapeDtypeStruct((bsz, t, d), F32), jax.ShapeDtypeStruct((bsz, t, d), F32),
                   jax.ShapeDtypeStruct((bsz, ne, t), F32)],
        scratch_shapes=[pltpu.VMEM((tm, w), BF16)],
        compiler_params=_cparams(("arbitrary", "arbitrary")),
        name="sgu",
    )(x1p, moe, nw3p, g2p, nw0, sh1, sc1, w_in_bf, ln_w, ln_b, w_s_bf, b_s_t, w_out_bf,
      nw1, g1, nw2, sh2, sc2, router_t)


def _lane_cumsum(src_scr, dst_scr, triu_ref, nseg):
    ne = src_scr.shape[0]

    def seg(i, carry):
        ls = pl.ds(pl.multiple_of(i * LANES, LANES), LANES)
        loc = _dot(src_scr[:, ls].astype(BF16), triu_ref[...]) + carry
        dst_scr[:, ls] = loc
        return jnp.broadcast_to(loc[:, LANES - 1:LANES], (ne, LANES))

    lax.fori_loop(0, nseg, seg, jnp.zeros((ne, LANES), F32))


def _route_kernel(lg_ref, triu_ref, idx_ref, gate_ref, a_scr, m_scr, c_scr, sel_scr, *, cap):
    ne, t = a_scr.shape
    nseg = t // LANES
    lg = lg_ref[0]
    ex = jnp.exp(lg - jnp.max(lg, axis=0, keepdims=True))
    a = ex / jnp.sum(ex, axis=0, keepdims=True)
    a_scr[...] = a
    capf = jnp.float32(cap)

    def bis(_, lohi):
        lo, hi = lohi
        mid = 0.5 * (lo + hi)
        cnt = jnp.sum((a >= mid).astype(F32), axis=1, keepdims=True)
        ge = cnt >= capf
        return jnp.where(ge, mid, lo), jnp.where(ge, hi, mid)

    lo, _ = lax.fori_loop(0, BISECT_ITERS, bis, (jnp.zeros((ne, 1), F32), jnp.full((ne, 1), 2.0, F32)))
    gt = a > lo
    eq = a == lo
    need = capf - jnp.sum(gt.astype(F32), axis=1, keepdims=True)
    m_scr[...] = eq.astype(F32)
    _lane_cumsum(m_scr, c_scr, triu_ref, nseg)
    sel = gt | (eq & ((c_scr[...] - m_scr[...]) < need))
    sel_scr[...] = sel.astype(F32)
    _lane_cumsum(sel_scr, c_scr, triu_ref, nseg)

    idx_ref[0] = jnp.zeros(idx_ref.shape[1:], I32)
    gate_ref[0] = jnp.zeros(gate_ref.shape[1:], F32)
    tpos = lax.broadcasted_iota(I32, (SUBLANES, t), 1).astype(F32)
    lane = lax.broadcasted_iota(I32, (SUBLANES, LANES), 1)

    def per_expert(e, carry):
        slot_row = jnp.where(sel_scr[pl.ds(e, 1), :] > 0.0, c_scr[pl.ds(e, 1), :], 0.0)
        a_row = a_scr[pl.ds(e, 1), :]

        def per_block(jb, carry2):
            j1 = (jb * SUBLANES + 1 + lax.broadcasted_iota(I32, (SUBLANES, 1), 0)).astype(F32)
            hit = slot_row == j1
            tok = jnp.sum(jnp.where(hit, tpos, 0.0), axis=1, keepdims=True)
            gat = jnp.sum(jnp.where(hit, a_row, 0.0), axis=1, keepdims=True)
            rs = pl.ds(pl.multiple_of(jb * SUBLANES, SUBLANES), SUBLANES)
            idx_ref[0, rs, :] = jnp.where(lane == e, tok.astype(I32), idx_ref[0, rs, :])
            gate_ref[0, rs, :] = jnp.where(lane == e, gat, gate_ref[0, rs, :])
            return carry2

        return lax.fori_loop(0, cap // SUBLANES, per_block, carry)

    lax.fori_loop(0, ne, per_expert, 0)


def _route(logits_t, cap):
    bsz, ne, t = logits_t.shape
    triu = (jnp.arange(LANES)[:, None] <= jnp.arange(LANES)[None, :]).astype(BF16)
    idx, gate = pl.pallas_call(
        functools.partial(_route_kernel, cap=cap),
        grid=(bsz,),
        in_specs=[pl.BlockSpec((1, ne, t), lambda b: (b, 0, 0)), pl.BlockSpec((LANES, LANES), lambda b: (0, 0))],
        out_specs=[pl.BlockSpec((1, cap, LANES), lambda b: (b, 0, 0))] * 2,
        out_shape=[jax.ShapeDtypeStruct((bsz, cap, LANES), I32), jax.ShapeDtypeStruct((bsz, cap, LANES), F32)],
        scratch_shapes=[pltpu.VMEM((ne, t), F32)] * 4,
        compiler_params=_cparams(("arbitrary",)),
        name="route",
    )(logits_t, triu)
    idx = jnp.swapaxes(idx[:, :, :ne], 1, 2)
    gate = jnp.swapaxes(gate[:, :, :ne], 1, 2)
    return idx, gate


GATHER_COLS = 256
ROW_UNROLL = 8


def _gather_kernel(idx_ref, h_ref, o_ref, *, cap, ne):
    b, e = pl.program_id(0), pl.program_id(2)
    base = (b * ne + e) * cap

    def body(j, carry):
        tkn = idx_ref[base + j]
        o_ref[0, 0, pl.ds(j, 1), :] = h_ref[0, pl.ds(tkn, 1), :]
        return carry

    lax.fori_loop(0, cap, body, 0, unroll=ROW_UNROLL)


def _gather(idx_flat, h2, ne, cap):
    bsz, t, d = h2.shape
    gc = min(GATHER_COLS, d)
    return pl.pallas_call(
        functools.partial(_gather_kernel, cap=cap, ne=ne),
        grid_spec=pltpu.PrefetchScalarGridSpec(
            num_scalar_prefetch=1,
            grid=(bsz, d // gc, ne),
            in_specs=[pl.BlockSpec((1, t, gc), lambda b, c, e, idx: (b, 0, c))],
            out_specs=pl.BlockSpec((1, 1, cap, gc), lambda b, c, e, idx: (b, e, 0, c)),
        ),
        out_shape=jax.ShapeDtypeStruct((bsz, ne, cap, d), F32),
        compiler_params=_cparams(("arbitrary", "arbitrary", "arbitrary")),
        name="gather",
    )(idx_flat, h2)


def _ffn_kernel(xs_ref, wg_ref, wu_ref, wd_ref, y_ref, xb_scr, acc_scr):
    f = pl.program_id(2)

    @pl.when(f == 0)
    def _():
        xb_scr[...] = xs_ref[0, 0].astype(BF16)

    xb = xb_scr[...]
    g = _dot(xb, wg_ref[0, 0].astype(BF16))
    u = _dot(xb, wu_ref[0, 0].astype(BF16))
    hid = (g * jax.nn.sigmoid(g) * u).astype(BF16)
    part = _dot(hid, wd_ref[0, 0].astype(BF16))

    @pl.when(f == 0)
    def _():
        acc_scr[...] = part

    @pl.when(f > 0)
    def _():
        acc_scr[...] += part

    @pl.when(f == pl.num_programs(2) - 1)
    def _():
        y_ref[0, 0] = acc_scr[...]


def _ffn(xs, w_gate, w_up, w_down, layer, tf):
    bsz, ne, cap, d = xs.shape
    ff = w_gate.shape[-1]
    tf = min(tf, ff)
    return pl.pallas_call(
        _ffn_kernel,
        grid=(bsz, ne, ff // tf),
        in_specs=[pl.BlockSpec((1, 1, cap, d), lambda b, e, f: (b, e, 0, 0)),
                  pl.BlockSpec((1, 1, d, tf), lambda b, e, f: (layer, e, 0, f)),
                  pl.BlockSpec((1, 1, d, tf), lambda b, e, f: (layer, e, 0, f)),
                  pl.BlockSpec((1, 1, tf, d), lambda b, e, f: (layer, e, f, 0))],
        out_specs=pl.BlockSpec((1, 1, cap, d), lambda b, e, f: (b, e, 0, 0)),
        out_shape=jax.ShapeDtypeStruct((bsz, ne, cap, d), F32),
        scratch_shapes=[pltpu.VMEM((cap, d), BF16), pltpu.VMEM((cap, d), F32)],
        compiler_params=_cparams(("arbitrary", "arbitrary", "arbitrary")),
        name="ffn",
    )(xs, w_gate, w_up, w_down)


SCATTER_SPLIT = 2


def _scatter_kernel(idx_ref, gate_ref, cut_ref, y_ref, o_ref, *, cap, ne, tpart):
    b, p, e = pl.program_id(0), pl.program_id(1), pl.program_id(2)

    @pl.when(e == 0)
    def _():
        o_ref[...] = jnp.zeros(o_ref.shape, F32)

    base = (b * ne + e) * cap
    cbase = (b * ne + e) * (SCATTER_SPLIT + 1)
    lo = cut_ref[cbase + p]
    hi = cut_ref[cbase + p + 1]
    t0 = p * tpart

    def body(j, carry):
        tkn = idx_ref[base + j] - t0
        gt = gate_ref[base + j]
        o_ref[0, pl.ds(tkn, 1), :] += gt * y_ref[0, 0, pl.ds(j, 1), :]
        return carry

    lax.fori_loop(lo, hi, body, 0)


def _scatter(idx_flat, gate_flat, cuts_flat, y, t):
    bsz, ne, cap, d = y.shape
    tpart = t // SCATTER_SPLIT
    return pl.pallas_call(
        functools.partial(_scatter_kernel, cap=cap, ne=ne, tpart=tpart),
        grid_spec=pltpu.PrefetchScalarGridSpec(
            num_scalar_prefetch=3,
            grid=(bsz, SCATTER_SPLIT, ne),
            in_specs=[pl.BlockSpec((1, 1, cap, d), lambda b, p, e, i, g, c: (b, e, 0, 0))],
            out_specs=pl.BlockSpec((1, tpart, d), lambda b, p, e, i, g, c: (b, p, 0)),
        ),
        out_shape=jax.ShapeDtypeStruct((bsz, t, d), F32),
        compiler_params=_cparams(("arbitrary", "arbitrary", "arbitrary")),
        name="scatter",
    )(idx_flat, gate_flat, cuts_flat, y)


def _moe(h2, logits_t, w_gate, w_up, w_down, layer, tf):
    bsz, t, d = h2.shape
    ne = logits_t.shape[1]
    cap = EC_CAPACITY_FACTOR * t // ne
    idx, gate = _route(logits_t, cap)
    idx_flat = idx.reshape(-1)
    xs = _gather(idx_flat, h2, ne, cap)
    y = _ffn(xs, w_gate, w_up, w_down, layer, tf)
    edges = jnp.arange(SCATTER_SPLIT + 1, dtype=I32) * (t // SCATTER_SPLIT)
    cuts = jnp.sum(idx[..., None] < edges, axis=2).astype(I32)
    return _scatter(idx_flat, gate.reshape(-1), cuts.reshape(-1), y, t)


def _final_kernel(x_ref, moe_ref, nw_ref, g_ref, o_ref):
    o_ref[0] = x_ref[0] + g_ref[0] * _rms(moe_ref[0], nw_ref[...])


def _final(x1, moe, nw3, g2, tm):
    bsz, t, d = x1.shape
    tok = pl.BlockSpec((1, tm, d), lambda b, i: (b, i, 0))
    return pl.pallas_call(
        _final_kernel,
        grid=(bsz, t // tm),
        in_specs=[tok, tok, pl.BlockSpec((1, d), lambda b, i: (0, 0)), pl.BlockSpec((1, 1, d), lambda b, i: (b, 0, 0))],
        out_specs=tok,
        out_shape=jax.ShapeDtypeStruct((bsz, t, d), F32),
        compiler_params=_cparams(("arbitrary", "arbitrary")),
        name="final",
    )(x1, moe, nw3, g2)


def _sincos_2d(rows, dim):
    quarter = dim // 4
    half = dim // 2
    freqs = jnp.exp(-math.log(10000.0) * jnp.arange(quarter, dtype=F32) / quarter)

    def emb1d(n):
        ang = jnp.arange(n, dtype=F32)[:, None] * freqs[None, :]
        return jnp.concatenate([jnp.sin(ang), jnp.cos(ang)], axis=-1)

    er = emb1d(rows)
    ec = emb1d(GRID_W)
    pos = jnp.concatenate([jnp.broadcast_to(er[:, None, :], (rows, GRID_W, half)),
                           jnp.broadcast_to(ec[None, :, :], (rows, GRID_W, half))], axis=-1)
    return pos.reshape(rows * GRID_W, dim)


TOKEN_TILE = 256
FFN_TILE = 512


def kernel(x, c, ctx, c_ctx, w_ada, b_ada, norm_w, hg_w_in, hg_lb, hg_gnorm, hg_w_out, sg_w_in, sg_ln_w, sg_ln_b,
           sg_w_s, sg_b_s, sg_w_out, moe_router, moe_w_gate, moe_w_up, moe_w_down):
    bsz, t, d = x.shape
    depth = w_ada.shape[0]
    assert depth == 2 and d == N_HEADS * HEAD_DIM and t % GLA_CHUNK == 0 and ctx.shape[1] % GLA_CHUNK == 0
    tm = min(TOKEN_TILE, t)

    cvec = jnp.zeros((SUBLANES, d), F32).at[:bsz].set(c).at[bsz].set(c_ctx)
    mod = _ada(cvec, w_ada, b_ada)

    def mods(layer, rows):
        m = mod[layer, rows].reshape(-1, N_ADA, 1, d)
        return [m[:, k] for k in range(N_ADA)]

    nw = norm_w.reshape(depth, 4, 1, d)
    router_t = jnp.swapaxes(moe_router, 1, 2)

    sh1, sc1, g1, sh2, sc2, g2 = mods(0, slice(0, bsz))
    sh1c, sc1c = [jnp.broadcast_to(m, (bsz, 1, d)) for m in mods(0, slice(bsz, bsz + 1))[:2]]
    w_in_bf = hg_w_in[0].astype(BF16)
    pos = _sincos_2d(t // GRID_W, d)
    tri, masks = _gla_consts()

    cv, ckf, cgf, ckb, cgb, cq, _ = _hg_in(ctx, None, nw[0, 0], sh1c, sc1c, w_in_bf, hg_lb, 1, tm)
    s_zero = jnp.zeros((bsz, 2, N_HEADS, HEAD_DIM, HEAD_DIM), F32)
    _, _, s_ctx = _gla(ckf, cgf, ckb, cgb, cv, cq, s_zero, tri, masks)

    xp, v, kf, gf, kb, gb, q, sg = _hg_in(x, pos, nw[0, 0], sh1, sc1, w_in_bf, hg_lb, 1, tm)
    o_f, o_b, _ = _gla(kf, gf, kb, gb, v, q, s_ctx, tri, masks)
    x1, h2, lg = _hg_out(o_f, o_b, sg, hg_gnorm[0:1], hg_w_out[0].astype(BF16), xp, nw[0, 1], g1, nw[0, 2],
                         sh2, sc2, router_t[0], tm)
    moe = _moe(h2, lg, moe_w_gate, moe_w_up, moe_w_down, 0, FFN_TILE)

    sh1b, sc1b, g1b, sh2b, sc2b, g2b = mods(1, slice(0, bsz))
    x1b, h2b, lgb = _sgu(x1, moe, nw[0, 3], g2, nw[1, 0], sh1b, sc1b, sg_w_in[0].astype(BF16),
                         sg_ln_w[0:1], sg_ln_b[0:1], sg_w_s[0].astype(BF16), sg_b_s[0].T, sg_w_out[0].astype(BF16),
                         nw[1, 1], g1b, nw[1, 2], sh2b, sc2b, router_t[1], tm)
    moe_b = _moe(h2b, lgb, moe_w_gate, moe_w_up, moe_w_down, 1, FFN_TILE)
    return _final(x1b, moe_b, nw[1, 3], g2b, tm)
```

```python
import functools
import math

import jax
import jax.numpy as jnp
from jax import lax
from jax.experimental import pallas as pl
from jax.experimental.pallas import tpu as pltpu

F32 = jnp.float32
BF16 = jnp.bfloat16
I32 = jnp.int32
HIGHEST = lax.Precision.HIGHEST

EPS = 1e-6
GRID_W = 64
N_ADA = 6
N_HEADS = 8
HEAD_DIM = 128
N_EXPERTS = 16
EC_CAPACITY_FACTOR = 2
SGU_CHUNK = 128
SGU_GROUPS = 8

LANES = 128
SUBLANES = 8
GLA_CHUNK = 128
GLA_LEVELS = (64, 32, 16, 8, 4, 2, 1)
BISECT_ITERS = 160
VMEM_LIMIT = 52 * 1024 * 1024


def _cparams(sem):
    return pltpu.CompilerParams(dimension_semantics=sem, vmem_limit_bytes=VMEM_LIMIT)


def _rms(x, w):
    ms = jnp.mean(x * x, axis=-1, keepdims=True)
    return x * lax.rsqrt(ms + EPS) * w


def _dot(a, b):
    return jnp.dot(a, b, preferred_element_type=F32)


def _dot_nt(a, b):
    return lax.dot_general(a, b, (((1,), (1,)), ((), ())), preferred_element_type=F32)


def _dot_tn(a, b):
    return lax.dot_general(a, b, (((0,), (0,)), ((), ())), preferred_element_type=F32)


def _ada_kernel(c_ref, w_ref, b_ref, o_ref):
    c = c_ref[...]
    s = c * jax.nn.sigmoid(c)
    o_ref[0] = jnp.dot(s, w_ref[0], precision=HIGHEST, preferred_element_type=F32) + b_ref[0]


def _ada(cvec, w_ada, b_ada):
    depth, d, nd = w_ada.shape
    rows = cvec.shape[0]
    return pl.pallas_call(
        _ada_kernel,
        grid=(depth, nd // d),
        in_specs=[pl.BlockSpec((rows, d), lambda l, n: (0, 0)),
                  pl.BlockSpec((1, d, d), lambda l, n: (l, 0, n)),
                  pl.BlockSpec((1, 1, d), lambda l, n: (l, 0, n))],
        out_specs=pl.BlockSpec((1, rows, d), lambda l, n: (l, 0, n)),
        out_shape=jax.ShapeDtypeStruct((depth, rows, nd), F32),
        compiler_params=_cparams(("arbitrary", "arbitrary")),
        name="ada",
    )(cvec, w_ada, b_ada.reshape(depth, 1, nd))


def _hgin_body(x, nw_ref, sh_ref, sc_ref, w_ref, lb_ref, outs, n_lb):
    v_ref, kf_ref, gf_ref, kb_ref, gb_ref, q_ref, sg_ref = outs
    d = x.shape[-1]
    h = _rms(x, nw_ref[...]) * (1.0 + sc_ref[0]) + sh_ref[0]
    hb = h.astype(BF16)
    lbs = lb_ref[...]
    e = jnp.exp(lbs - jnp.max(lbs, axis=0, keepdims=True))
    lb = jnp.sum(e[:n_lb], axis=0) / jnp.sum(e, axis=0)

    v_ref[0] = _dot(hb, w_ref[:, 0:d]).astype(BF16)
    for j, (k_ref, g_ref) in enumerate(((kf_ref, gf_ref), (kb_ref, gb_ref))):
        raw = _dot(hb, w_ref[:, (1 + j) * d:(2 + j) * d])
        lbj = lb[j:j + 1]
        sig = jax.nn.sigmoid(raw)
        f = lbj + (1.0 - lbj) * sig
        k_ref[0] = ((1.0 - lbj) * (1.0 - sig)).astype(BF16)
        g_ref[0] = jnp.log(f)
    qr = _dot(hb, w_ref[:, 3 * d:4 * d])
    q_ref[0] = (qr * jax.nn.sigmoid(qr)).astype(BF16)
    gr = _dot(hb, w_ref[:, 4 * d:5 * d])
    sg_ref[0] = (gr * jax.nn.sigmoid(gr)).astype(BF16)


def _hgin_pos_kernel(x_ref, pos_ref, nw_ref, sh_ref, sc_ref, w_ref, lb_ref, xp_ref, *outs, n_lb):
    x = x_ref[0] + pos_ref[...]
    xp_ref[0] = x
    _hgin_body(x, nw_ref, sh_ref, sc_ref, w_ref, lb_ref, outs, n_lb)


def _hgin_kernel(x_ref, nw_ref, sh_ref, sc_ref, w_ref, lb_ref, *outs, n_lb):
    _hgin_body(x_ref[0], nw_ref, sh_ref, sc_ref, w_ref, lb_ref, outs, n_lb)


def _hg_in(x, pos, nw, sh, sc, w_bf, lb_raw, n_lb, tm):
    bsz, t, d = x.shape
    tm = min(tm, t)
    tok = pl.BlockSpec((1, tm, d), lambda b, i: (b, i, 0))
    vec = pl.BlockSpec((1, d), lambda b, i: (0, 0))
    bvec = pl.BlockSpec((1, 1, d), lambda b, i: (b, 0, 0))
    wspec = pl.BlockSpec(w_bf.shape, lambda b, i: (0, 0))
    lbspec = pl.BlockSpec(lb_raw.shape, lambda b, i: (0, 0, 0))
    gate_shapes = [jax.ShapeDtypeStruct((bsz, t, d), dt) for dt in (BF16, BF16, F32, BF16, F32, BF16, BF16)]
    if pos is not None:
        kern = functools.partial(_hgin_pos_kernel, n_lb=n_lb)
        in_specs = [tok, pl.BlockSpec((tm, d), lambda b, i: (i, 0)), vec, bvec, bvec, wspec, lbspec]
        args = (x, pos, nw, sh, sc, w_bf, lb_raw)
        out_shape = [jax.ShapeDtypeStruct((bsz, t, d), F32)] + gate_shapes
    else:
        kern = functools.partial(_hgin_kernel, n_lb=n_lb)
        in_specs = [tok, vec, bvec, bvec, wspec, lbspec]
        args = (x, nw, sh, sc, w_bf, lb_raw)
        out_shape = gate_shapes
    return pl.pallas_call(
        kern,
        grid=(bsz, t // tm),
        in_specs=in_specs,
        out_specs=[tok] * len(out_shape),
        out_shape=out_shape,
        compiler_params=_cparams(("arbitrary", "arbitrary")),
        name="hg_in",
    )(*args)


def _gla_consts():
    c = GLA_CHUNK
    t = jnp.arange(c)[:, None]
    s = jnp.arange(c)[None, :]
    tri = jnp.stack([(s <= t), (s >= t)]).astype(BF16)
    masks = []
    for rev in (False, True):
        lv = []
        for m in GLA_LEVELS:
            same = (t // (2 * m)) == (s // (2 * m))
            tq = ((t // m) % 2) == (0 if rev else 1)
            sk = ((s // m) % 2) == (1 if rev else 0)
            lv.append(same & tq & sk)
        lv.append(t == s)
        masks.append(jnp.stack(lv))
    return tri, jnp.stack(masks).astype(F32)


def _split3(x):
    p0 = x.astype(BF16)
    r1 = x - p0.astype(F32)
    p1 = r1.astype(BF16)
    p2 = (r1 - p1.astype(F32)).astype(BF16)
    return p0, p1, p2


def _level_ref(b_scr, hs, m, rev):
    c = GLA_CHUNK
    off = m if rev else m - 1

    def row8(i):
        return jnp.broadcast_to(b_scr[pl.ds(i, 1), hs], (SUBLANES, HEAD_DIM))

    pieces = []
    if m >= SUBLANES:
        for blk in range(c // (2 * m)):
            r8 = row8(blk * 2 * m + off)
            pieces.extend([r8] * (2 * m // SUBLANES))
    else:
        sub = lax.broadcasted_iota(I32, (SUBLANES, HEAD_DIM), 0) // (2 * m)
        for grp in range(c // SUBLANES):
            piece = row8(grp * SUBLANES + off)
            for cls in range(1, SUBLANES // (2 * m)):
                piece = jnp.where(sub == cls, row8(grp * SUBLANES + cls * 2 * m + off), piece)
            pieces.append(piece)
    return jnp.concatenate(pieces, axis=0)


def _gla_direction(q_ref, k_ref, v_ref, g_ref, o_ref, tri_ref, mask_ref, st_scr, b_scr, d_idx, rev):
    c = GLA_CHUNK
    g = g_ref[0]
    tri = tri_ref[d_idx]
    p0, p1, p2 = _split3(g)
    b_scr[...] = _dot(tri, p0) + _dot(tri, p1) + _dot(tri, p2)
    row = lax.broadcasted_iota(I32, (c, HEAD_DIM), 0)

    def head(h, carry):
        hs = pl.ds(pl.multiple_of(h * HEAD_DIM, HEAD_DIM), HEAD_DIM)
        b = b_scr[:, hs]
        q = q_ref[0, :, hs].astype(F32)
        k = k_ref[0, :, hs].astype(F32)
        vb = v_ref[0, :, hs]
        bl = b_scr[pl.ds(0 if rev else c - 1, 1), hs]
        qt = (q * jnp.exp(b)).astype(BF16)
        kt = (k * jnp.exp(bl - b)).astype(BF16)
        st = st_scr[d_idx, h]
        o = _dot_nt(qt, st.astype(BF16))
        a = mask_ref[d_idx, len(GLA_LEVELS)] * _dot_nt(q.astype(BF16), k.astype(BF16))
        for li, m in enumerate(GLA_LEVELS):
            r = _level_ref(b_scr, hs, m, rev)
            qside = ((row // m) % 2) == (0 if rev else 1)
            diff = b - r
            e = jnp.exp(jnp.where(qside, diff, -diff))
            a = a + mask_ref[d_idx, li] * _dot_nt((q * e).astype(BF16), (k * e).astype(BF16))
        o = o + _dot(a.astype(BF16), vb)
        o_ref[0, :, hs] = o
        st_scr[d_idx, h] = st * jnp.exp(bl) + _dot_tn(vb, kt)
        return carry

    lax.fori_loop(0, N_HEADS, head, 0)


def _gla_kernel(kf_ref, gf_ref, vf_ref, qf_ref, kb_ref, gb_ref, vb_ref, qb_ref, s0_ref, tri_ref, mask_ref,
                of_ref, ob_ref, sout_ref, st_scr, bf_scr, bb_scr):
    n = pl.program_id(1)

    @pl.when(n == 0)
    def _():
        st_scr[...] = s0_ref[0]

    _gla_direction(qf_ref, kf_ref, vf_ref, gf_ref, of_ref, tri_ref, mask_ref, st_scr, bf_scr, 0, False)
    _gla_direction(qb_ref, kb_ref, vb_ref, gb_ref, ob_ref, tri_ref, mask_ref, st_scr, bb_scr, 1, True)

    @pl.when(n == pl.num_programs(1) - 1)
    def _():
        sout_ref[0] = st_scr[...]


def _gla(kf, gf, kb, gb, v, q, s0, tri, masks):
    bsz, t, d = v.shape
    c = GLA_CHUNK
    n = t // c
    fwd = pl.BlockSpec((1, c, d), lambda b, i: (b, i, 0))
    bwd = pl.BlockSpec((1, c, d), lambda b, i: (b, n - 1 - i, 0))
    sspec = pl.BlockSpec((1,) + s0.shape[1:], lambda b, i: (b, 0, 0, 0, 0))
    return pl.pallas_call(
        _gla_kernel,
        grid=(bsz, n),
        in_specs=[fwd, fwd, fwd, fwd, bwd, bwd, bwd, bwd, sspec,
                  pl.BlockSpec(tri.shape, lambda b, i: (0, 0, 0)),
                  pl.BlockSpec(masks.shape, lambda b, i: (0, 0, 0, 0))],
        out_specs=[fwd, bwd, sspec],
        out_shape=[jax.ShapeDtypeStruct((bsz, t, d), F32), jax.ShapeDtypeStruct((bsz, t, d), F32),
                   jax.ShapeDtypeStruct(s0.shape, F32)],
        scratch_shapes=[pltpu.VMEM(s0.shape[1:], F32), pltpu.VMEM((c, d), F32), pltpu.VMEM((c, d), F32)],
        compiler_params=_cparams(("arbitrary", "arbitrary")),
        name="gla",
    )(kf, gf, v, q, kb, gb, v, q, s0, tri, masks)


def _post_mixer(y, xres, nw1_ref, g1_ref, nw2_ref, sh2_ref, sc2_ref, rt_ref, x1_ref, h2_ref, lg_ref):
    x1 = xres + g1_ref[0] * _rms(y, nw1_ref[...])
    x1_ref[0] = x1
    h2 = _rms(x1, nw2_ref[...]) * (1.0 + sc2_ref[0]) + sh2_ref[0]
    h2_ref[0] = h2
    lg_ref[0] = lax.dot_general(rt_ref[...], h2, (((1,), (1,)), ((), ())),
                                precision=HIGHEST, preferred_element_type=F32)


def _hgout_kernel(of_ref, ob_ref, sg_ref, gn_ref, w_ref, xp_ref, nw1_ref, g1_ref, nw2_ref, sh2_ref, sc2_ref,
                  rt_ref, x1_ref, h2_ref, lg_ref, z_scr):
    o = of_ref[0] + ob_ref[0]
    for h in range(N_HEADS):
        hs = slice(h * HEAD_DIM, (h + 1) * HEAD_DIM)
        oh = o[:, hs]
        ms = jnp.mean(oh * oh, axis=-1, keepdims=True)
        z = oh * lax.rsqrt(ms + EPS) * gn_ref[:, hs] * sg_ref[0, :, hs].astype(F32)
        z_scr[:, hs] = z.astype(BF16)
    y = _dot(z_scr[...], w_ref[...])
    _post_mixer(y, xp_ref[0], nw1_ref, g1_ref, nw2_ref, sh2_ref, sc2_ref, rt_ref, x1_ref, h2_ref, lg_ref)


def _hg_out(o_f, o_b, sg, gnorm, w_out_bf, xp, nw1, g1, nw2, sh2, sc2, router_t, tm):
    bsz, t, d = xp.shape
    ne = router_t.shape[0]
    tok = pl.BlockSpec((1, tm, d), lambda b, i: (b, i, 0))
    vec = pl.BlockSpec((1, d), lambda b, i: (0, 0))
    bvec = pl.BlockSpec((1, 1, d), lambda b, i: (b, 0, 0))
    return pl.pallas_call(
        _hgout_kernel,
        grid=(bsz, t // tm),
        in_specs=[tok, tok, tok, vec, pl.BlockSpec((d, d), lambda b, i: (0, 0)), tok, vec, bvec, vec, bvec, bvec,
                  pl.BlockSpec((ne, d), lambda b, i: (0, 0))],
        out_specs=[tok, tok, pl.BlockSpec((1, ne, tm), lambda b, i: (b, 0, i))],
        out_shape=[jax.ShapeDtypeStruct((bsz, t, d), F32), jax.ShapeDtypeStruct((bsz, t, d), F32),
                   jax.ShapeDtypeStruct((bsz, ne, t), F32)],
        scratch_shapes=[pltpu.VMEM((tm, d), BF16)],
        compiler_params=_cparams(("arbitrary", "arbitrary")),
        name="hg_out",
    )(o_f, o_b, sg, gnorm, w_out_bf, xp, nw1, g1, nw2, sh2, sc2, router_t)


def _sgu_kernel(x1p_ref, moe_ref, nw3p_ref, g2p_ref, nw0_ref, sh1_ref, sc1_ref, win_ref, lnw_ref, lnb_ref,
                ws_ref, bs_ref, wout_ref, nw1_ref, g1_ref, nw2_ref, sh2_ref, sc2_ref, rt_ref,
                x1_ref, h2_ref, lg_ref, m_scr):
    tm, d = x1p_ref.shape[1], x1p_ref.shape[2]
    w = lnw_ref.shape[-1]
    gd = w // SGU_GROUPS
    x = x1p_ref[0] + g2p_ref[0] * _rms(moe_ref[0], nw3p_ref[...])
    h = _rms(x, nw0_ref[...]) * (1.0 + sc1_ref[0]) + sh1_ref[0]
    hb = h.astype(BF16)

    def gelu(z):
        return 0.5 * z * (1.0 + jnp.tanh(math.sqrt(2.0 / math.pi) * (z + 0.044715 * (z * z * z))))

    u = gelu(_dot(hb, win_ref[:, 0:w]))
    v = gelu(_dot(hb, win_ref[:, w:2 * w]))
    mu = jnp.mean(v, axis=-1, keepdims=True)
    vc = v - mu
    vn = vc * lax.rsqrt(jnp.mean(vc * vc, axis=-1, keepdims=True) + EPS) * lnw_ref[...] + lnb_ref[...]
    vnb = vn.astype(BF16)
    for ck in range(tm // SGU_CHUNK):
        rs = slice(ck * SGU_CHUNK, (ck + 1) * SGU_CHUNK)
        for g in range(SGU_GROUPS):
            cs = slice(g * gd, (g + 1) * gd)
            mixed = _dot(ws_ref[g], vnb[rs, cs]) + bs_ref[:, g:g + 1]
            m_scr[rs, cs] = (u[rs, cs] * mixed).astype(BF16)
    y = _dot(m_scr[...], wout_ref[...])
    _post_mixer(y, x, nw1_ref, g1_ref, nw2_ref, sh2_ref, sc2_ref, rt_ref, x1_ref, h2_ref, lg_ref)


def _sgu(x1p, moe, nw3p, g2p, nw0, sh1, sc1, w_in_bf, ln_w, ln_b, w_s_bf, b_s_t, w_out_bf,
         nw1, g1, nw2, sh2, sc2, router_t, tm):
    bsz, t, d = x1p.shape
    ne = router_t.shape[0]
    w = ln_w.shape[-1]
    tok = pl.BlockSpec((1, tm, d), lambda b, i: (b, i, 0))
    vec = pl.BlockSpec((1, d), lambda b, i: (0, 0))
    wvec = pl.BlockSpec((1, w), lambda b, i: (0, 0))
    bvec = pl.BlockSpec((1, 1, d), lambda b, i: (b, 0, 0))

    def full(a):
        return pl.BlockSpec(a.shape, lambda b, i: (0,) * a.ndim)

    return pl.pallas_call(
        _sgu_kernel,
        grid=(bsz, t // tm),
        in_specs=[tok, tok, vec, bvec, vec, bvec, bvec, full(w_in_bf), wvec, wvec, full(w_s_bf), full(b_s_t),
                  full(w_out_bf), vec, bvec, vec, bvec, bvec, full(router_t)],
        out_specs=[tok, tok, pl.BlockSpec((1, ne, tm), lambda b, i: (b, 0, i))],
        out_shape=[jax.ShapeDtypeStruct((bsz, t, d), F32), jax.ShapeDtypeStruct((bsz, t, d), F32),
                   jax.ShapeDtypeStruct((bsz, ne, t), F32)],
        scratch_shapes=[pltpu.VMEM((tm, w), BF16)],
        compiler_params=_cparams(("arbitrary", "arbitrary")),
        name="sgu",
    )(x1p, moe, nw3p, g2p, nw0, sh1, sc1, w_in_bf, ln_w, ln_b, w_s_bf, b_s_t, w_out_bf,
      nw1, g1, nw2, sh2, sc2, router_t)


def _token_prefix(mask, triu, slow):
    local = _dot(mask.astype(BF16), triu)
    rowtot = jnp.broadcast_to(local[:, LANES - 1:LANES], local.shape)
    prev = _dot(slow, rowtot.astype(BF16))
    return local, prev, rowtot


def _route_kernel(lg_ref, triu_ref, slow_ref, idx_ref, gate_ref, a_scr, thr_scr, rhs_scr, *, cap):
    ne, nr = a_scr.shape[0], a_scr.shape[1]
    lg = lg_ref[0]
    ex = jnp.exp(lg - jnp.max(lg, axis=0, keepdims=True))
    a = ex / jnp.sum(ex, axis=0, keepdims=True)
    a_scr[...] = a
    capf = jnp.float32(cap)

    def count(m):
        return jnp.sum(jnp.sum(m.astype(F32), axis=1, keepdims=True), axis=2, keepdims=True)

    def bis(_, lohi):
        lo, hi = lohi
        mid = 0.5 * (lo + hi)
        ge = count(a >= mid) >= capf
        return jnp.where(ge, mid, lo), jnp.where(ge, hi, mid)

    lo, _ = lax.fori_loop(0, BISECT_ITERS, bis, (jnp.zeros((ne, 1, 1), F32), jnp.full((ne, 1, 1), 2.0, F32)))
    thr_scr[...] = jnp.broadcast_to(lo, thr_scr.shape)

    idx_ref[0] = jnp.zeros(idx_ref.shape[1:], I32)
    gate_ref[0] = jnp.zeros(gate_ref.shape[1:], F32)
    triu = triu_ref[...]
    slow = slow_ref[...]
    lane = lax.broadcasted_iota(I32, (LANES, LANES), 1)
    lane_f = lane.astype(F32)
    sub_f = lax.broadcasted_iota(I32, (LANES, LANES), 0).astype(F32)
    rowid = lax.broadcasted_iota(I32, (nr, LANES), 0).astype(F32)

    def per_expert(e, carry):
        ae = a_scr[e]
        v = thr_scr[e][0:1, :]
        gt = ae > v
        eq = ae == v
        need = capf - jnp.sum(jnp.sum(gt.astype(F32), axis=0, keepdims=True), axis=1, keepdims=True)
        eql, eqp, _ = _token_prefix(eq, triu, slow)
        sel = gt | (eq & ((eql + eqp - eq.astype(F32)) < need))
        local, prev, rowtot = _token_prefix(sel, triu, slow)
        rowcum = prev + rowtot
        prev_hi = jnp.floor(prev * (1.0 / LANES))
        a0, a1, a2 = _split3(ae)
        for k, piece in enumerate((local, prev_hi, prev - LANES * prev_hi, rowid, a0, a1, a2)):
            rhs_scr[:, k * LANES:(k + 1) * LANES] = piece.astype(BF16)

        for p in range(cap // LANES):
            base = float(p * LANES + 1)
            slot_row = base + lane_f[0:1, :]
            onehot_t = ((prev < slot_row) & (rowcum >= slot_row)).astype(BF16)
            g = _dot_tn(onehot_t, rhs_scr[...])
            g_local = g[:, 0:LANES]
            g_prev = LANES * g[:, LANES:2 * LANES] + g[:, 2 * LANES:3 * LANES]
            g_row = g[:, 3 * LANES:4 * LANES]
            g_a = g[:, 4 * LANES:5 * LANES] + g[:, 5 * LANES:6 * LANES] + g[:, 6 * LANES:7 * LANES]
            slot_col = base + sub_f
            lstar = jnp.sum(((g_local + g_prev) < slot_col).astype(F32), axis=1, keepdims=True)
            tok = LANES * g_row[:, 0:1] + lstar
            gat = jnp.sum(jnp.where(lane_f == lstar, g_a, 0.0), axis=1, keepdims=True)
            rs = slice(p * LANES, (p + 1) * LANES)
            idx_ref[0, rs, :] = jnp.where(lane == e, tok.astype(I32), idx_ref[0, rs, :])
            gate_ref[0, rs, :] = jnp.where(lane == e, gat, gate_ref[0, rs, :])
        return carry

    lax.fori_loop(0, ne, per_expert, 0)


def _route(logits_t, cap):
    bsz, ne, t = logits_t.shape
    nr = t // LANES
    assert cap % LANES == 0 and nr % SUBLANES == 0
    triu = (jnp.arange(LANES)[:, None] <= jnp.arange(LANES)[None, :]).astype(BF16)
    slow = (jnp.arange(nr)[:, None] > jnp.arange(nr)[None, :]).astype(BF16)
    idx, gate = pl.pallas_call(
        functools.partial(_route_kernel, cap=cap),
        grid=(bsz,),
        in_specs=[pl.BlockSpec((1, ne, nr, LANES), lambda b: (b, 0, 0, 0)),
                  pl.BlockSpec((LANES, LANES), lambda b: (0, 0)), pl.BlockSpec((nr, nr), lambda b: (0, 0))],
        out_specs=[pl.BlockSpec((1, cap, LANES), lambda b: (b, 0, 0))] * 2,
        out_shape=[jax.ShapeDtypeStruct((bsz, cap, LANES), I32), jax.ShapeDtypeStruct((bsz, cap, LANES), F32)],
        scratch_shapes=[pltpu.VMEM((ne, nr, LANES), F32), pltpu.VMEM((ne, SUBLANES, LANES), F32),
                        pltpu.VMEM((nr, 7 * LANES), BF16)],
        compiler_params=_cparams(("arbitrary",)),
        name="route",
    )(logits_t.reshape(bsz, ne, nr, LANES), triu, slow)
    idx = jnp.swapaxes(idx[:, :, :ne], 1, 2)
    gate = jnp.swapaxes(gate[:, :, :ne], 1, 2)
    return idx, gate


ROW_UNROLL = 8


def _gather_kernel(idx_ref, h_hbm, o_ref, h_scr, sem, *, cap, ne):
    b, e = pl.program_id(0), pl.program_id(1)

    @pl.when(e == 0)
    def _():
        cp = pltpu.make_async_copy(h_hbm.at[b], h_scr, sem.at[0])
        cp.start()
        cp.wait()

    base = (b * ne + e) * cap

    def body(j, carry):
        tkn = idx_ref[base + j]
        o_ref[0, 0, pl.ds(j, 1), :] = h_scr[pl.ds(tkn, 1), :]
        return carry

    lax.fori_loop(0, cap, body, 0, unroll=ROW_UNROLL)


def _gather(idx_flat, h2, ne, cap):
    bsz, t, d = h2.shape
    return pl.pallas_call(
        functools.partial(_gather_kernel, cap=cap, ne=ne),
        grid_spec=pltpu.PrefetchScalarGridSpec(
            num_scalar_prefetch=1,
            grid=(bsz, ne),
            in_specs=[pl.BlockSpec(memory_space=pl.ANY)],
            out_specs=pl.BlockSpec((1, 1, cap, d), lambda b, e, idx: (b, e, 0, 0)),
            scratch_shapes=[pltpu.VMEM((t, d), F32), pltpu.SemaphoreType.DMA((1,))],
        ),
        out_shape=jax.ShapeDtypeStruct((bsz, ne, cap, d), F32),
        compiler_params=_cparams(("arbitrary", "arbitrary")),
        name="gather",
    )(idx_flat, h2)


def _ffn_kernel(xs_ref, wg_ref, wu_ref, wd_ref, y_ref, xb_scr, acc_scr):
    f = pl.program_id(2)

    @pl.when(f == 0)
    def _():
        xb_scr[...] = xs_ref[0, 0].astype(BF16)

    xb = xb_scr[...]
    g = _dot(xb, wg_ref[0, 0].astype(BF16))
    u = _dot(xb, wu_ref[0, 0].astype(BF16))
    hid = (g * jax.nn.sigmoid(g) * u).astype(BF16)
    part = _dot(hid, wd_ref[0, 0].astype(BF16))

    @pl.when(f == 0)
    def _():
        acc_scr[...] = part

    @pl.when(f > 0)
    def _():
        acc_scr[...] += part

    @pl.when(f == pl.num_programs(2) - 1)
    def _():
        y_ref[0, 0] = acc_scr[...]


def _ffn(xs, w_gate, w_up, w_down, layer, tf):
    bsz, ne, cap, d = xs.shape
    ff = w_gate.shape[-1]
    tf = min(tf, ff)
    return pl.pallas_call(
        _ffn_kernel,
        grid=(bsz, ne, ff // tf),
        in_specs=[pl.BlockSpec((1, 1, cap, d), lambda b, e, f: (b, e, 0, 0)),
                  pl.BlockSpec((1, 1, d, tf), lambda b, e, f: (layer, e, 0, f)),
                  pl.BlockSpec((1, 1, d, tf), lambda b, e, f: (layer, e, 0, f)),
                  pl.BlockSpec((1, 1, tf, d), lambda b, e, f: (layer, e, f, 0))],
        out_specs=pl.BlockSpec((1, 1, cap, d), lambda b, e, f: (b, e, 0, 0)),
        out_shape=jax.ShapeDtypeStruct((bsz, ne, cap, d), F32),
        scratch_shapes=[pltpu.VMEM((cap, d), BF16), pltpu.VMEM((cap, d), F32)],
        compiler_params=_cparams(("arbitrary", "arbitrary", "arbitrary")),
        name="ffn",
    )(xs, w_gate, w_up, w_down)


SCATTER_SPLIT = 4


def _scatter_kernel(idx_ref, gate_ref, cut_ref, y_ref, o_hbm, *scr, cap, ne, tpart):
    accs, sem = scr[:SCATTER_SPLIT], scr[SCATTER_SPLIT]
    b, e = pl.program_id(0), pl.program_id(1)

    @pl.when(e == 0)
    def _():
        for acc in accs:
            acc[...] = jnp.zeros(acc.shape, F32)

    base = (b * ne + e) * cap
    cbase = (b * ne + e) * (SCATTER_SPLIT + 1)
    starts = [cut_ref[cbase + k] for k in range(SCATTER_SPLIT)]
    counts = [cut_ref[cbase + k + 1] - starts[k] for k in range(SCATTER_SPLIT)]
    shortest = functools.reduce(jnp.minimum, counts)
    longest = functools.reduce(jnp.maximum, counts)

    def common(i, carry):
        for k, acc in enumerate(accs):
            j = starts[k] + i
            row = idx_ref[base + j] - k * tpart
            acc[pl.ds(row, 1), :] += gate_ref[base + j] * y_ref[0, 0, pl.ds(j, 1), :]
        return carry

    def tail(i, carry):
        for k, acc in enumerate(accs):
            valid = i < counts[k]
            j = jnp.minimum(starts[k] + i, cap - 1)
            row = jnp.where(valid, idx_ref[base + j] - k * tpart, 0)
            gt = jnp.where(valid, gate_ref[base + j], 0.0)
            acc[pl.ds(row, 1), :] += gt * y_ref[0, 0, pl.ds(j, 1), :]
        return carry

    lax.fori_loop(0, shortest, common, 0)
    lax.fori_loop(shortest, longest, tail, 0)

    @pl.when(e == ne - 1)
    def _():
        copies = [pltpu.make_async_copy(acc, o_hbm.at[b, pl.ds(k * tpart, tpart), :], sem.at[k])
                  for k, acc in enumerate(accs)]
        for cp in copies:
            cp.start()
        for cp in copies:
            cp.wait()


def _scatter(idx_flat, gate_flat, cuts_flat, y, t):
    bsz, ne, cap, d = y.shape
    tpart = t // SCATTER_SPLIT
    return pl.pallas_call(
        functools.partial(_scatter_kernel, cap=cap, ne=ne, tpart=tpart),
        grid_spec=pltpu.PrefetchScalarGridSpec(
            num_scalar_prefetch=3,
            grid=(bsz, ne),
            in_specs=[pl.BlockSpec((1, 1, cap, d), lambda b, e, i, g, c: (b, e, 0, 0))],
            out_specs=pl.BlockSpec(memory_space=pl.ANY),
            scratch_shapes=[pltpu.VMEM((tpart, d), F32)] * SCATTER_SPLIT + [pltpu.SemaphoreType.DMA((SCATTER_SPLIT,))],
        ),
        out_shape=jax.ShapeDtypeStruct((bsz, t, d), F32),
        compiler_params=_cparams(("arbitrary", "arbitrary")),
        name="scatter",
    )(idx_flat, gate_flat, cuts_flat, y)


def _moe(h2, logits_t, w_gate, w_up, w_down, layer, tf):
    bsz, t, d = h2.shape
    ne = logits_t.shape[1]
    cap = EC_CAPACITY_FACTOR * t // ne
    idx, gate = _route(logits_t, cap)
    idx_flat = idx.reshape(-1)
    xs = _gather(idx_flat, h2, ne, cap)
    y = _ffn(xs, w_gate, w_up, w_down, layer, tf)
    edges = jnp.arange(SCATTER_SPLIT + 1, dtype=I32) * (t // SCATTER_SPLIT)
    cuts = jnp.sum(idx[..., None] < edges, axis=2).astype(I32)
    return _scatter(idx_flat, gate.reshape(-1), cuts.reshape(-1), y, t)


def _final_kernel(x_ref, moe_ref, nw_ref, g_ref, o_ref):
    o_ref[0] = x_ref[0] + g_ref[0] * _rms(moe_ref[0], nw_ref[...])


def _final(x1, moe, nw3, g2, tm):
    bsz, t, d = x1.shape
    tok = pl.BlockSpec((1, tm, d), lambda b, i: (b, i, 0))
    return pl.pallas_call(
        _final_kernel,
        grid=(bsz, t // tm),
        in_specs=[tok, tok, pl.BlockSpec((1, d), lambda b, i: (0, 0)), pl.BlockSpec((1, 1, d), lambda b, i: (b, 0, 0))],
        out_specs=tok,
        out_shape=jax.ShapeDtypeStruct((bsz, t, d), F32),
        compiler_params=_cparams(("arbitrary", "arbitrary")),
        name="final",
    )(x1, moe, nw3, g2)


def _sincos_2d(rows, dim):
    quarter = dim // 4
    half = dim // 2
    freqs = jnp.exp(-math.log(10000.0) * jnp.arange(quarter, dtype=F32) / quarter)

    def emb1d(n):
        ang = jnp.arange(n, dtype=F32)[:, None] * freqs[None, :]
        return jnp.concatenate([jnp.sin(ang), jnp.cos(ang)], axis=-1)

    er = emb1d(rows)
    ec = emb1d(GRID_W)
    pos = jnp.concatenate([jnp.broadcast_to(er[:, None, :], (rows, GRID_W, half)),
                           jnp.broadcast_to(ec[None, :, :], (rows, GRID_W, half))], axis=-1)
    return pos.reshape(rows * GRID_W, dim)


TOKEN_TILE = 256
FFN_TILE = 512


def kernel(x, c, ctx, c_ctx, w_ada, b_ada, norm_w, hg_w_in, hg_lb, hg_gnorm, hg_w_out, sg_w_in, sg_ln_w, sg_ln_b,
           sg_w_s, sg_b_s, sg_w_out, moe_router, moe_w_gate, moe_w_up, moe_w_down):
    bsz, t, d = x.shape
    depth = w_ada.shape[0]
    assert depth == 2 and d == N_HEADS * HEAD_DIM and t % GLA_CHUNK == 0 and ctx.shape[1] % GLA_CHUNK == 0
    tm = min(TOKEN_TILE, t)

    cvec = jnp.zeros((SUBLANES, d), F32).at[:bsz].set(c).at[bsz].set(c_ctx)
    mod = _ada(cvec, w_ada, b_ada)

    def mods(layer, rows):
        m = mod[layer, rows].reshape(-1, N_ADA, 1, d)
        return [m[:, k] for k in range(N_ADA)]

    nw = norm_w.reshape(depth, 4, 1, d)
    router_t = jnp.swapaxes(moe_router, 1, 2)

    sh1, sc1, g1, sh2, sc2, g2 = mods(0, slice(0, bsz))
    sh1c, sc1c = [jnp.broadcast_to(m, (bsz, 1, d)) for m in mods(0, slice(bsz, bsz + 1))[:2]]
    w_in_bf = hg_w_in[0].astype(BF16)
    pos = _sincos_2d(t // GRID_W, d)
    tri, masks = _gla_consts()

    cv, ckf, cgf, ckb, cgb, cq, _ = _hg_in(ctx, None, nw[0, 0], sh1c, sc1c, w_in_bf, hg_lb, 1, tm)
    s_zero = jnp.zeros((bsz, 2, N_HEADS, HEAD_DIM, HEAD_DIM), F32)
    _, _, s_ctx = _gla(ckf, cgf, ckb, cgb, cv, cq, s_zero, tri, masks)

    xp, v, kf, gf, kb, gb, q, sg = _hg_in(x, pos, nw[0, 0], sh1, sc1, w_in_bf, hg_lb, 1, tm)
    o_f, o_b, _ = _gla(kf, gf, kb, gb, v, q, s_ctx, tri, masks)
    x1, h2, lg = _hg_out(o_f, o_b, sg, hg_gnorm[0:1], hg_w_out[0].astype(BF16), xp, nw[0, 1], g1, nw[0, 2],
                         sh2, sc2, router_t[0], tm)
    moe = _moe(h2, lg, moe_w_gate, moe_w_up, moe_w_down, 0, FFN_TILE)

    sh1b, sc1b, g1b, sh2b, sc2b, g2b = mods(1, slice(0, bsz))
    x1b, h2b, lgb = _sgu(x1, moe, nw[0, 3], g2, nw[1, 0], sh1b, sc1b, sg_w_in[0].astype(BF16),
                         sg_ln_w[0:1], sg_ln_b[0:1], sg_w_s[0].astype(BF16), sg_b_s[0].T, sg_w_out[0].astype(BF16),
                         nw[1, 1], g1b, nw[1, 2], sh2b, sc2b, router_t[1], tm)
    moe_b = _moe(h2b, lgb, moe_w_gate, moe_w_up, moe_w_down, 1, FFN_TILE)
    return _final(x1b, moe_b, nw[1, 3], g2b, tm)
```

```python
import functools
import math

import jax
import jax.numpy as jnp
from jax import lax
from jax.experimental import pallas as pl
from jax.experimental.pallas import tpu as pltpu

F32 = jnp.float32
BF16 = jnp.bfloat16
I32 = jnp.int32
HIGHEST = lax.Precision.HIGHEST

EPS = 1e-6
GRID_W = 64
N_ADA = 6
N_HEADS = 8
HEAD_DIM = 128
N_EXPERTS = 16
EC_CAPACITY_FACTOR = 2
SGU_CHUNK = 128
SGU_GROUPS = 8

LANES = 128
SUBLANES = 8
GLA_CHUNK = 128
GLA_LEVELS = (64, 32, 16, 8, 4, 2, 1)
GLA_SHORT_BLOCK = 32
GLA_SHORT_MAX_EXPONENT = 60.0
BISECT_ITERS = 160
VMEM_LIMIT = 52 * 1024 * 1024


def _cparams(sem):
    return pltpu.CompilerParams(dimension_semantics=sem, vmem_limit_bytes=VMEM_LIMIT)


def _rms(x, w):
    ms = jnp.mean(x * x, axis=-1, keepdims=True)
    return x * lax.rsqrt(ms + EPS) * w


def _dot(a, b):
    return jnp.dot(a, b, preferred_element_type=F32)


def _dot_nt(a, b):
    return lax.dot_general(a, b, (((1,), (1,)), ((), ())), preferred_element_type=F32)


def _dot_tn(a, b):
    return lax.dot_general(a, b, (((0,), (0,)), ((), ())), preferred_element_type=F32)


def _ada_kernel(c_ref, w_ref, b_ref, o_ref):
    c = c_ref[...]
    s = c * jax.nn.sigmoid(c)
    o_ref[0] = jnp.dot(s, w_ref[0], precision=HIGHEST, preferred_element_type=F32) + b_ref[0]


def _ada(cvec, w_ada, b_ada):
    depth, d, nd = w_ada.shape
    rows = cvec.shape[0]
    return pl.pallas_call(
        _ada_kernel,
        grid=(depth, nd // d),
        in_specs=[pl.BlockSpec((rows, d), lambda l, n: (0, 0)),
                  pl.BlockSpec((1, d, d), lambda l, n: (l, 0, n)),
                  pl.BlockSpec((1, 1, d), lambda l, n: (l, 0, n))],
        out_specs=pl.BlockSpec((1, rows, d), lambda l, n: (l, 0, n)),
        out_shape=jax.ShapeDtypeStruct((depth, rows, nd), F32),
        compiler_params=_cparams(("arbitrary", "arbitrary")),
        name="ada",
    )(cvec, w_ada, b_ada.reshape(depth, 1, nd))


def _hgin_body(x, nw_ref, sh_ref, sc_ref, w_ref, lb_ref, outs, n_lb):
    v_ref, kf_ref, gf_ref, kb_ref, gb_ref, q_ref, sg_ref = outs
    d = x.shape[-1]
    h = _rms(x, nw_ref[...]) * (1.0 + sc_ref[0]) + sh_ref[0]
    hb = h.astype(BF16)
    lbs = lb_ref[...]
    e = jnp.exp(lbs - jnp.max(lbs, axis=0, keepdims=True))
    lb = jnp.sum(e[:n_lb], axis=0) / jnp.sum(e, axis=0)

    v_ref[0] = _dot(hb, w_ref[:, 0:d]).astype(BF16)
    for j, (k_ref, g_ref) in enumerate(((kf_ref, gf_ref), (kb_ref, gb_ref))):
        raw = _dot(hb, w_ref[:, (1 + j) * d:(2 + j) * d])
        lbj = lb[j:j + 1]
        sig = jax.nn.sigmoid(raw)
        f = lbj + (1.0 - lbj) * sig
        k_ref[0] = ((1.0 - lbj) * (1.0 - sig)).astype(BF16)
        g_ref[0] = jnp.log(f)
    qr = _dot(hb, w_ref[:, 3 * d:4 * d])
    q_ref[0] = (qr * jax.nn.sigmoid(qr)).astype(BF16)
    gr = _dot(hb, w_ref[:, 4 * d:5 * d])
    sg_ref[0] = (gr * jax.nn.sigmoid(gr)).astype(BF16)


def _hgin_pos_kernel(x_ref, pos_ref, nw_ref, sh_ref, sc_ref, w_ref, lb_ref, xp_ref, *outs, n_lb):
    x = x_ref[0] + pos_ref[...]
    xp_ref[0] = x
    _hgin_body(x, nw_ref, sh_ref, sc_ref, w_ref, lb_ref, outs, n_lb)


def _hgin_kernel(x_ref, nw_ref, sh_ref, sc_ref, w_ref, lb_ref, *outs, n_lb):
    _hgin_body(x_ref[0], nw_ref, sh_ref, sc_ref, w_ref, lb_ref, outs, n_lb)


def _hg_in(x, pos, nw, sh, sc, w_bf, lb_raw, n_lb, tm):
    bsz, t, d = x.shape
    tm = min(tm, t)
    tok = pl.BlockSpec((1, tm, d), lambda b, i: (b, i, 0))
    vec = pl.BlockSpec((1, d), lambda b, i: (0, 0))
    bvec = pl.BlockSpec((1, 1, d), lambda b, i: (b, 0, 0))
    wspec = pl.BlockSpec(w_bf.shape, lambda b, i: (0, 0))
    lbspec = pl.BlockSpec(lb_raw.shape, lambda b, i: (0, 0, 0))
    gate_shapes = [jax.ShapeDtypeStruct((bsz, t, d), dt) for dt in (BF16, BF16, F32, BF16, F32, BF16, BF16)]
    if pos is not None:
        kern = functools.partial(_hgin_pos_kernel, n_lb=n_lb)
        in_specs = [tok, pl.BlockSpec((tm, d), lambda b, i: (i, 0)), vec, bvec, bvec, wspec, lbspec]
        args = (x, pos, nw, sh, sc, w_bf, lb_raw)
        out_shape = [jax.ShapeDtypeStruct((bsz, t, d), F32)] + gate_shapes
    else:
        kern = functools.partial(_hgin_kernel, n_lb=n_lb)
        in_specs = [tok, vec, bvec, bvec, wspec, lbspec]
        args = (x, nw, sh, sc, w_bf, lb_raw)
        out_shape = gate_shapes
    return pl.pallas_call(
        kern,
        grid=(bsz, t // tm),
        in_specs=in_specs,
        out_specs=[tok] * len(out_shape),
        out_shape=out_shape,
        compiler_params=_cparams(("arbitrary", "arbitrary")),
        name="hg_in",
    )(*args)


def _gla_consts():
    c = GLA_CHUNK
    t = jnp.arange(c)[:, None]
    s = jnp.arange(c)[None, :]
    tri = jnp.stack([(s <= t), (s >= t)]).astype(BF16)
    masks = []
    for rev in (False, True):
        lv = []
        for m in GLA_LEVELS:
            same = (t // (2 * m)) == (s // (2 * m))
            tq = ((t // m) % 2) == (0 if rev else 1)
            sk = ((s // m) % 2) == (1 if rev else 0)
            lv.append(same & tq & sk)
        lv.append(t == s)
        lv.append(((t // GLA_SHORT_BLOCK) == (s // GLA_SHORT_BLOCK)) & ((s >= t) if rev else (s <= t)))
        masks.append(jnp.stack(lv))
    return tri, jnp.stack(masks).astype(F32)


def _split3(x):
    p0 = x.astype(BF16)
    r1 = x - p0.astype(F32)
    p1 = r1.astype(BF16)
    p2 = (r1 - p1.astype(F32)).astype(BF16)
    return p0, p1, p2


def _level_ref(b_scr, hs, m, rev):
    c = GLA_CHUNK
    off = m if rev else m - 1

    def row8(i):
        return jnp.broadcast_to(b_scr[pl.ds(i, 1), hs], (SUBLANES, HEAD_DIM))

    pieces = []
    if m >= SUBLANES:
        for blk in range(c // (2 * m)):
            r8 = row8(blk * 2 * m + off)
            pieces.extend([r8] * (2 * m // SUBLANES))
    else:
        sub = lax.broadcasted_iota(I32, (SUBLANES, HEAD_DIM), 0) // (2 * m)
        for grp in range(c // SUBLANES):
            piece = row8(grp * SUBLANES + off)
            for cls in range(1, SUBLANES // (2 * m)):
                piece = jnp.where(sub == cls, row8(grp * SUBLANES + cls * 2 * m + off), piece)
            pieces.append(piece)
    return jnp.concatenate(pieces, axis=0)


def _block_ref(b_scr, hs, rev):
    c, blk = GLA_CHUNK, GLA_SHORT_BLOCK
    pieces = []
    for i in range(c // blk):
        r = (i + 1) * blk if rev else i * blk - 1
        if 0 <= r < c:
            pieces.append(jnp.broadcast_to(b_scr[pl.ds(r, 1), hs], (blk, HEAD_DIM)))
        else:
            pieces.append(jnp.zeros((blk, HEAD_DIM), F32))
    return jnp.concatenate(pieces, axis=0)


def _block_decay_bound(b_scr, rev):
    c, blk = GLA_CHUNK, GLA_SHORT_BLOCK
    worst = None
    for i in range(c // blk):
        inner = b_scr[pl.ds(i * blk if rev else (i + 1) * blk - 1, 1), :]
        r = (i + 1) * blk if rev else i * blk - 1
        span = jnp.abs(inner - b_scr[pl.ds(r, 1), :]) if 0 <= r < c else jnp.abs(inner)
        worst = span if worst is None else jnp.maximum(worst, span)
    return jnp.max(worst)


def _gla_direction(q_ref, k_ref, v_ref, g_ref, o_ref, tri_ref, mask_ref, st_scr, b_scr, d_idx, rev):
    c = GLA_CHUNK
    g = g_ref[0]
    tri = tri_ref[d_idx]
    p0, p1, p2 = _split3(g)
    b_scr[...] = _dot(tri, p0) + _dot(tri, p1) + _dot(tri, p2)
    n_lv = len(GLA_LEVELS)

    def head(h, short):
        hs = pl.ds(pl.multiple_of(h * HEAD_DIM, HEAD_DIM), HEAD_DIM)
        b = b_scr[:, hs]
        q = q_ref[0, :, hs]
        k = k_ref[0, :, hs]
        vb = v_ref[0, :, hs]
        bl = b_scr[pl.ds(0 if rev else c - 1, 1), hs]
        qt = q * jnp.exp(b).astype(BF16)
        kt = k * jnp.exp(bl - b).astype(BF16)
        st = st_scr[d_idx, h]
        o = _dot_nt(qt, st.astype(BF16))

        def level(m):
            e = jnp.exp(-jnp.abs(b - _level_ref(b_scr, hs, m, rev))).astype(BF16)
            return _dot_nt(q * e, k * e)

        if short:
            u = _block_ref(b_scr, hs, rev) - b
            within = _dot_nt(q * jnp.exp(-u).astype(BF16), k * jnp.exp(u).astype(BF16))
            a = jnp.where(mask_ref[d_idx, n_lv + 1] > 0.0, within, 0.0)
            for li, m in enumerate(GLA_LEVELS):
                if m >= GLA_SHORT_BLOCK:
                    a = a + mask_ref[d_idx, li] * level(m)
        else:
            a = mask_ref[d_idx, n_lv] * _dot_nt(q, k)
            for li, m in enumerate(GLA_LEVELS):
                a = a + mask_ref[d_idx, li] * level(m)
        o = o + _dot(a.astype(BF16), vb)
        o_ref[0, :, hs] = o
        st_scr[d_idx, h] = st * jnp.exp(bl) + _dot_tn(vb, kt)

    bounded = _block_decay_bound(b_scr, rev) <= GLA_SHORT_MAX_EXPONENT

    @pl.when(bounded)
    def _():
        lax.fori_loop(0, N_HEADS, lambda h, cr: (head(h, True), cr)[1], 0, unroll=8)

    @pl.when(jnp.logical_not(bounded))
    def _():
        lax.fori_loop(0, N_HEADS, lambda h, cr: (head(h, False), cr)[1], 0, unroll=2)


def _gla_kernel(kf_ref, gf_ref, vf_ref, qf_ref, kb_ref, gb_ref, vb_ref, qb_ref, s0_ref, tri_ref, mask_ref,
                of_ref, ob_ref, sout_ref, st_scr, bf_scr, bb_scr):
    n = pl.program_id(1)

    @pl.when(n == 0)
    def _():
        st_scr[...] = s0_ref[0]

    _gla_direction(qf_ref, kf_ref, vf_ref, gf_ref, of_ref, tri_ref, mask_ref, st_scr, bf_scr, 0, False)
    _gla_direction(qb_ref, kb_ref, vb_ref, gb_ref, ob_ref, tri_ref, mask_ref, st_scr, bb_scr, 1, True)

    @pl.when(n == pl.num_programs(1) - 1)
    def _():
        sout_ref[0] = st_scr[...]


def _gla(kf, gf, kb, gb, v, q, s0, tri, masks):
    bsz, t, d = v.shape
    c = GLA_CHUNK
    n = t // c
    fwd = pl.BlockSpec((1, c, d), lambda b, i: (b, i, 0))
    bwd = pl.BlockSpec((1, c, d), lambda b, i: (b, n - 1 - i, 0))
    sspec = pl.BlockSpec((1,) + s0.shape[1:], lambda b, i: (b, 0, 0, 0, 0))
    return pl.pallas_call(
        _gla_kernel,
        grid=(bsz, n),
        in_specs=[fwd, fwd, fwd, fwd, bwd, bwd, bwd, bwd, sspec,
                  pl.BlockSpec(tri.shape, lambda b, i: (0, 0, 0)),
                  pl.BlockSpec(masks.shape, lambda b, i: (0, 0, 0, 0))],
        out_specs=[fwd, bwd, sspec],
        out_shape=[jax.ShapeDtypeStruct((bsz, t, d), F32), jax.ShapeDtypeStruct((bsz, t, d), F32),
                   jax.ShapeDtypeStruct(s0.shape, F32)],
        scratch_shapes=[pltpu.VMEM(s0.shape[1:], F32), pltpu.VMEM((c, d), F32), pltpu.VMEM((c, d), F32)],
        compiler_params=_cparams(("arbitrary", "arbitrary")),
        name="gla",
    )(kf, gf, v, q, kb, gb, v, q, s0, tri, masks)


def _post_mixer(y, xres, nw1_ref, g1_ref, nw2_ref, sh2_ref, sc2_ref, rt_ref, x1_ref, h2_ref, lg_ref):
    x1 = xres + g1_ref[0] * _rms(y, nw1_ref[...])
    x1_ref[0] = x1
    h2 = _rms(x1, nw2_ref[...]) * (1.0 + sc2_ref[0]) + sh2_ref[0]
    h2_ref[0] = h2
    lg_ref[0] = lax.dot_general(rt_ref[...], h2, (((1,), (1,)), ((), ())),
                                precision=HIGHEST, preferred_element_type=F32)


def _hgout_kernel(of_ref, ob_ref, sg_ref, gn_ref, w_ref, xp_ref, nw1_ref, g1_ref, nw2_ref, sh2_ref, sc2_ref,
                  rt_ref, x1_ref, h2_ref, lg_ref, z_scr):
    o = of_ref[0] + ob_ref[0]
    for h in range(N_HEADS):
        hs = slice(h * HEAD_DIM, (h + 1) * HEAD_DIM)
        oh = o[:, hs]
        ms = jnp.mean(oh * oh, axis=-1, keepdims=True)
        z = oh * lax.rsqrt(ms + EPS) * gn_ref[:, hs] * sg_ref[0, :, hs].astype(F32)
        z_scr[:, hs] = z.astype(BF16)
    y = _dot(z_scr[...], w_ref[...])
    _post_mixer(y, xp_ref[0], nw1_ref, g1_ref, nw2_ref, sh2_ref, sc2_ref, rt_ref, x1_ref, h2_ref, lg_ref)


def _hg_out(o_f, o_b, sg, gnorm, w_out_bf, xp, nw1, g1, nw2, sh2, sc2, router_t, tm):
    bsz, t, d = xp.shape
    ne = router_t.shape[0]
    tok = pl.BlockSpec((1, tm, d), lambda b, i: (b, i, 0))
    vec = pl.BlockSpec((1, d), lambda b, i: (0, 0))
    bvec = pl.BlockSpec((1, 1, d), lambda b, i: (b, 0, 0))
    return pl.pallas_call(
        _hgout_kernel,
        grid=(bsz, t // tm),
        in_specs=[tok, tok, tok, vec, pl.BlockSpec((d, d), lambda b, i: (0, 0)), tok, vec, bvec, vec, bvec, bvec,
                  pl.BlockSpec((ne, d), lambda b, i: (0, 0))],
        out_specs=[tok, tok, pl.BlockSpec((1, ne, tm), lambda b, i: (b, 0, i))],
        out_shape=[jax.ShapeDtypeStruct((bsz, t, d), F32), jax.ShapeDtypeStruct((bsz, t, d), F32),
                   jax.ShapeDtypeStruct((bsz, ne, t), F32)],
        scratch_shapes=[pltpu.VMEM((tm, d), BF16)],
        compiler_params=_cparams(("arbitrary", "arbitrary")),
        name="hg_out",
    )(o_f, o_b, sg, gnorm, w_out_bf, xp, nw1, g1, nw2, sh2, sc2, router_t)


def _sgu_kernel(x1p_ref, moe_ref, nw3p_ref, g2p_ref, nw0_ref, sh1_ref, sc1_ref, win_ref, lnw_ref, lnb_ref,
                ws_ref, bs_ref, wout_ref, nw1_ref, g1_ref, nw2_ref, sh2_ref, sc2_ref, rt_ref,
                x1_ref, h2_ref, lg_ref, m_scr):
    tm, d = x1p_ref.shape[1], x1p_ref.shape[2]
    w = lnw_ref.shape[-1]
    gd = w // SGU_GROUPS
    x = x1p_ref[0] + g2p_ref[0] * _rms(moe_ref[0], nw3p_ref[...])
    h = _rms(x, nw0_ref[...]) * (1.0 + sc1_ref[0]) + sh1_ref[0]
    hb = h.astype(BF16)

    def gelu(z):
        return 0.5 * z * (1.0 + jnp.tanh(math.sqrt(2.0 / math.pi) * (z + 0.044715 * (z * z * z))))

    u = gelu(_dot(hb, win_ref[:, 0:w]))
    v = gelu(_dot(hb, win_ref[:, w:2 * w]))
    mu = jnp.mean(v, axis=-1, keepdims=True)
    vc = v - mu
    vn = vc * lax.rsqrt(jnp.mean(vc * vc, axis=-1, keepdims=True) + EPS) * lnw_ref[...] + lnb_ref[...]
    vnb = vn.astype(BF16)
    for ck in range(tm // SGU_CHUNK):
        rs = slice(ck * SGU_CHUNK, (ck + 1) * SGU_CHUNK)
        for g in range(SGU_GROUPS):
            cs = slice(g * gd, (g + 1) * gd)
            mixed = _dot(ws_ref[g], vnb[rs, cs]) + bs_ref[:, g:g + 1]
            m_scr[rs, cs] = (u[rs, cs] * mixed).astype(BF16)
    y = _dot(m_scr[...], wout_ref[...])
    _post_mixer(y, x, nw1_ref, g1_ref, nw2_ref, sh2_ref, sc2_ref, rt_ref, x1_ref, h2_ref, lg_ref)


def _sgu(x1p, moe, nw3p, g2p, nw0, sh1, sc1, w_in_bf, ln_w, ln_b, w_s_bf, b_s_t, w_out_bf,
         nw1, g1, nw2, sh2, sc2, router_t, tm):
    bsz, t, d = x1p.shape
    ne = router_t.shape[0]
    w = ln_w.shape[-1]
    tok = pl.BlockSpec((1, tm, d), lambda b, i: (b, i, 0))
    vec = pl.BlockSpec((1, d), lambda b, i: (0, 0))
    wvec = pl.BlockSpec((1, w), lambda b, i: (0, 0))
    bvec = pl.BlockSpec((1, 1, d), lambda b, i: (b, 0, 0))

    def full(a):
        return pl.BlockSpec(a.shape, lambda b, i: (0,) * a.ndim)

    return pl.pallas_call(
        _sgu_kernel,
        grid=(bsz, t // tm),
        in_specs=[tok, tok, vec, bvec, vec, bvec, bvec, full(w_in_bf), wvec, wvec, full(w_s_bf), full(b_s_t),
                  full(w_out_bf), vec, bvec, vec, bvec, bvec, full(router_t)],
        out_specs=[tok, tok, pl.BlockSpec((1, ne, tm), lambda b, i: (b, 0, i))],
        out_shape=[jax.ShapeDtypeStruct((bsz, t, d), F32), jax.ShapeDtypeStruct((bsz, t, d), F32),
                   jax.ShapeDtypeStruct((bsz, ne, t), F32)],
        scratch_shapes=[pltpu.VMEM((tm, w), BF16)],
        compiler_params=_cparams(("arbitrary", "arbitrary")),
        name="sgu",
    )(x1p, moe, nw3p, g2p, nw0, sh1, sc1, w_in_bf, ln_w, ln_b, w_s_bf, b_s_t, w_out_bf,
      nw1, g1, nw2, sh2, sc2, router_t)


def _token_prefix(mask, triu, slow):
    local = _dot(mask.astype(BF16), triu)
    rowtot = jnp.broadcast_to(local[:, LANES - 1:LANES], local.shape)
    prev = _dot(slow, rowtot.astype(BF16))
    return local, prev, rowtot


def _route_kernel(lg_ref, triu_ref, slow_ref, idx_ref, gate_ref, a_scr, thr_scr, rhs_scr, *, cap):
    ne, nr = a_scr.shape[0], a_scr.shape[1]
    lg = lg_ref[0]
    ex = jnp.exp(lg - jnp.max(lg, axis=0, keepdims=True))
    a = ex / jnp.sum(ex, axis=0, keepdims=True)
    a_scr[...] = a
    capf = jnp.float32(cap)

    def count(m):
        return jnp.sum(jnp.sum(m.astype(F32), axis=1, keepdims=True), axis=2, keepdims=True)

    def bis(_, lohi):
        lo, hi = lohi
        mid = 0.5 * (lo + hi)
        ge = count(a >= mid) >= capf
        return jnp.where(ge, mid, lo), jnp.where(ge, hi, mid)

    lo, _ = lax.fori_loop(0, BISECT_ITERS, bis, (jnp.zeros((ne, 1, 1), F32), jnp.full((ne, 1, 1), 2.0, F32)))
    thr_scr[...] = jnp.broadcast_to(lo, thr_scr.shape)

    idx_ref[0] = jnp.zeros(idx_ref.shape[1:], I32)
    gate_ref[0] = jnp.zeros(gate_ref.shape[1:], F32)
    triu = triu_ref[...]
    slow = slow_ref[...]
    lane = lax.broadcasted_iota(I32, (LANES, LANES), 1)
    lane_f = lane.astype(F32)
    sub_f = lax.broadcasted_iota(I32, (LANES, LANES), 0).astype(F32)
    rowid = lax.broadcasted_iota(I32, (nr, LANES), 0).astype(F32)

    def per_expert(e, carry):
        ae = a_scr[e]
        v = thr_scr[e][0:1, :]
        gt = ae > v
        eq = ae == v
        need = capf - jnp.sum(jnp.sum(gt.astype(F32), axis=0, keepdims=True), axis=1, keepdims=True)
        eql, eqp, _ = _token_prefix(eq, triu, slow)
        sel = gt | (eq & ((eql + eqp - eq.astype(F32)) < need))
        local, prev, rowtot = _token_prefix(sel, triu, slow)
        rowcum = prev + rowtot
        prev_hi = jnp.floor(prev * (1.0 / LANES))
        a0, a1, a2 = _split3(ae)
        for k, piece in enumerate((local, prev_hi, prev - LANES * prev_hi, rowid, a0, a1, a2)):
            rhs_scr[:, k * LANES:(k + 1) * LANES] = piece.astype(BF16)

        for p in range(cap // LANES):
            base = float(p * LANES + 1)
            slot_row = base + lane_f[0:1, :]
            onehot_t = ((prev < slot_row) & (rowcum >= slot_row)).astype(BF16)
            g = _dot_tn(onehot_t, rhs_scr[...])
            g_local = g[:, 0:LANES]
            g_prev = LANES * g[:, LANES:2 * LANES] + g[:, 2 * LANES:3 * LANES]
            g_row = g[:, 3 * LANES:4 * LANES]
            g_a = g[:, 4 * LANES:5 * LANES] + g[:, 5 * LANES:6 * LANES] + g[:, 6 * LANES:7 * LANES]
            slot_col = base + sub_f
            lstar = jnp.sum(((g_local + g_prev) < slot_col).astype(F32), axis=1, keepdims=True)
            tok = LANES * g_row[:, 0:1] + lstar
            gat = jnp.sum(jnp.where(lane_f == lstar, g_a, 0.0), axis=1, keepdims=True)
            rs = slice(p * LANES, (p + 1) * LANES)
            idx_ref[0, rs, :] = jnp.where(lane == e, tok.astype(I32), idx_ref[0, rs, :])
            gate_ref[0, rs, :] = jnp.where(lane == e, gat, gate_ref[0, rs, :])
        return carry

    lax.fori_loop(0, ne, per_expert, 0)


def _route(logits_t, cap):
    bsz, ne, t = logits_t.shape
    nr = t // LANES
    assert cap % LANES == 0 and nr % SUBLANES == 0
    triu = (jnp.arange(LANES)[:, None] <= jnp.arange(LANES)[None, :]).astype(BF16)
    slow = (jnp.arange(nr)[:, None] > jnp.arange(nr)[None, :]).astype(BF16)
    idx, gate = pl.pallas_call(
        functools.partial(_route_kernel, cap=cap),
        grid=(bsz,),
        in_specs=[pl.BlockSpec((1, ne, nr, LANES), lambda b: (b, 0, 0, 0)),
                  pl.BlockSpec((LANES, LANES), lambda b: (0, 0)), pl.BlockSpec((nr, nr), lambda b: (0, 0))],
        out_specs=[pl.BlockSpec((1, cap, LANES), lambda b: (b, 0, 0))] * 2,
        out_shape=[jax.ShapeDtypeStruct((bsz, cap, LANES), I32), jax.ShapeDtypeStruct((bsz, cap, LANES), F32)],
        scratch_shapes=[pltpu.VMEM((ne, nr, LANES), F32), pltpu.VMEM((ne, SUBLANES, LANES), F32),
                        pltpu.VMEM((nr, 7 * LANES), BF16)],
        compiler_params=_cparams(("arbitrary",)),
        name="route",
    )(logits_t.reshape(bsz, ne, nr, LANES), triu, slow)
    idx = jnp.swapaxes(idx[:, :, :ne], 1, 2)
    gate = jnp.swapaxes(gate[:, :, :ne], 1, 2)
    return idx, gate


ROW_UNROLL = 8


def _gather_kernel(idx_ref, h_hbm, o_ref, h_scr, row_scr, sem, *, cap, ne):
    b, e = pl.program_id(0), pl.program_id(1)

    @pl.when(e == 0)
    def _():
        cp = pltpu.make_async_copy(h_hbm.at[b], h_scr, sem.at[0])
        cp.start()
        cp.wait()

    base = (b * ne + e) * cap

    def body(j, carry):
        tkn = idx_ref[base + j]
        row_scr[pl.ds(j, 1), :] = h_scr[pl.ds(tkn, 1), :]
        return carry

    lax.fori_loop(0, cap, body, 0, unroll=ROW_UNROLL)
    o_ref[0, 0] = row_scr[...].astype(BF16)


def _gather(idx_flat, h2, ne, cap):
    bsz, t, d = h2.shape
    return pl.pallas_call(
        functools.partial(_gather_kernel, cap=cap, ne=ne),
        grid_spec=pltpu.PrefetchScalarGridSpec(
            num_scalar_prefetch=1,
            grid=(bsz, ne),
            in_specs=[pl.BlockSpec(memory_space=pl.ANY)],
            out_specs=pl.BlockSpec((1, 1, cap, d), lambda b, e, idx: (b, e, 0, 0)),
            scratch_shapes=[pltpu.VMEM((t, d), F32), pltpu.VMEM((cap, d), F32), pltpu.SemaphoreType.DMA((1,))],
        ),
        out_shape=jax.ShapeDtypeStruct((bsz, ne, cap, d), BF16),
        compiler_params=_cparams(("arbitrary", "arbitrary")),
        name="gather",
    )(idx_flat, h2)


def _ffn_kernel(xs_ref, wg_ref, wu_ref, wd_ref, y_ref):
    f = pl.program_id(1)
    wg = wg_ref[0, 0].astype(BF16)
    wu = wu_ref[0, 0].astype(BF16)
    wd = wd_ref[0, 0].astype(BF16)
    for b in range(xs_ref.shape[0]):
        xb = xs_ref[b, 0]
        g = _dot(xb, wg)
        u = _dot(xb, wu)
        hid = (g * jax.nn.sigmoid(g) * u).astype(BF16)
        part = _dot(hid, wd)

        @pl.when(f == 0)
        def _():
            y_ref[b, 0] = part

        @pl.when(f > 0)
        def _():
            y_ref[b, 0] += part


def _ffn(xs, w_gate, w_up, w_down, layer, tf):
    bsz, ne, cap, d = xs.shape
    ff = w_gate.shape[-1]
    tf = min(tf, ff)
    tok = pl.BlockSpec((bsz, 1, cap, d), lambda e, f: (0, e, 0, 0))
    return pl.pallas_call(
        _ffn_kernel,
        grid=(ne, ff // tf),
        in_specs=[tok,
                  pl.BlockSpec((1, 1, d, tf), lambda e, f: (layer, e, 0, f)),
                  pl.BlockSpec((1, 1, d, tf), lambda e, f: (layer, e, 0, f)),
                  pl.BlockSpec((1, 1, tf, d), lambda e, f: (layer, e, f, 0))],
        out_specs=tok,
        out_shape=jax.ShapeDtypeStruct((bsz, ne, cap, d), F32),
        compiler_params=_cparams(("arbitrary", "arbitrary")),
        name="ffn",
    )(xs, w_gate, w_up, w_down)


SCATTER_SPLIT = 4


def _scatter_kernel(idx_ref, gate_ref, cut_ref, y_ref, o_hbm, *scr, cap, ne, tpart):
    accs, sem = scr[:SCATTER_SPLIT], scr[SCATTER_SPLIT]
    b, e = pl.program_id(0), pl.program_id(1)

    @pl.when(e == 0)
    def _():
        for acc in accs:
            acc[...] = jnp.zeros(acc.shape, F32)

    base = (b * ne + e) * cap
    cbase = (b * ne + e) * (SCATTER_SPLIT + 1)
    starts = [cut_ref[cbase + k] for k in range(SCATTER_SPLIT)]
    counts = [cut_ref[cbase + k + 1] - starts[k] for k in range(SCATTER_SPLIT)]
    shortest = functools.reduce(jnp.minimum, counts)
    longest = functools.reduce(jnp.maximum, counts)

    def common(i, carry):
        for k, acc in enumerate(accs):
            j = starts[k] + i
            row = idx_ref[base + j] - k * tpart
            acc[pl.ds(row, 1), :] += gate_ref[base + j] * y_ref[0, 0, pl.ds(j, 1), :]
        return carry

    def tail(i, carry):
        for k, acc in enumerate(accs):
            valid = i < counts[k]
            j = jnp.minimum(starts[k] + i, cap - 1)
            row = jnp.where(valid, idx_ref[base + j] - k * tpart, 0)
            gt = jnp.where(valid, gate_ref[base + j], 0.0)
            acc[pl.ds(row, 1), :] += gt * y_ref[0, 0, pl.ds(j, 1), :]
        return carry

    lax.fori_loop(0, shortest, common, 0)
    lax.fori_loop(shortest, longest, tail, 0)

    @pl.when(e == ne - 1)
    def _():
        copies = [pltpu.make_async_copy(acc, o_hbm.at[b, pl.ds(k * tpart, tpart), :], sem.at[k])
                  for k, acc in enumerate(accs)]
        for cp in copies:
            cp.start()
        for cp in copies:
            cp.wait()


def _scatter(idx_flat, gate_flat, cuts_flat, y, t):
    bsz, ne, cap, d = y.shape
    tpart = t // SCATTER_SPLIT
    return pl.pallas_call(
        functools.partial(_scatter_kernel, cap=cap, ne=ne, tpart=tpart),
        grid_spec=pltpu.PrefetchScalarGridSpec(
            num_scalar_prefetch=3,
            grid=(bsz, ne),
            in_specs=[pl.BlockSpec((1, 1, cap, d), lambda b, e, i, g, c: (b, e, 0, 0))],
            out_specs=pl.BlockSpec(memory_space=pl.ANY),
            scratch_shapes=[pltpu.VMEM((tpart, d), F32)] * SCATTER_SPLIT + [pltpu.SemaphoreType.DMA((SCATTER_SPLIT,))],
        ),
        out_shape=jax.ShapeDtypeStruct((bsz, t, d), F32),
        compiler_params=_cparams(("arbitrary", "arbitrary")),
        name="scatter",
    )(idx_flat, gate_flat, cuts_flat, y)


def _moe(h2, logits_t, w_gate, w_up, w_down, layer, tf):
    bsz, t, d = h2.shape
    ne = logits_t.shape[1]
    cap = EC_CAPACITY_FACTOR * t // ne
    idx, gate = _route(logits_t, cap)
    idx_flat = idx.reshape(-1)
    xs = _gather(idx_flat, h2, ne, cap)
    y = _ffn(xs, w_gate, w_up, w_down, layer, tf)
    edges = jnp.arange(SCATTER_SPLIT + 1, dtype=I32) * (t // SCATTER_SPLIT)
    cuts = jnp.sum(idx[..., None] < edges, axis=2).astype(I32)
    return _scatter(idx_flat, gate.reshape(-1), cuts.reshape(-1), y, t)


def _final_kernel(x_ref, moe_ref, nw_ref, g_ref, o_ref):
    o_ref[0] = x_ref[0] + g_ref[0] * _rms(moe_ref[0], nw_ref[...])


def _final(x1, moe, nw3, g2, tm):
    bsz, t, d = x1.shape
    tok = pl.BlockSpec((1, tm, d), lambda b, i: (b, i, 0))
    return pl.pallas_call(
        _final_kernel,
        grid=(bsz, t // tm),
        in_specs=[tok, tok, pl.BlockSpec((1, d), lambda b, i: (0, 0)), pl.BlockSpec((1, 1, d), lambda b, i: (b, 0, 0))],
        out_specs=tok,
        out_shape=jax.ShapeDtypeStruct((bsz, t, d), F32),
        compiler_params=_cparams(("arbitrary", "arbitrary")),
        name="final",
    )(x1, moe, nw3, g2)


def _sincos_2d(rows, dim):
    quarter = dim // 4
    half = dim // 2
    freqs = jnp.exp(-math.log(10000.0) * jnp.arange(quarter, dtype=F32) / quarter)

    def emb1d(n):
        ang = jnp.arange(n, dtype=F32)[:, None] * freqs[None, :]
        return jnp.concatenate([jnp.sin(ang), jnp.cos(ang)], axis=-1)

    er = emb1d(rows)
    ec = emb1d(GRID_W)
    pos = jnp.concatenate([jnp.broadcast_to(er[:, None, :], (rows, GRID_W, half)),
                           jnp.broadcast_to(ec[None, :, :], (rows, GRID_W, half))], axis=-1)
    return pos.reshape(rows * GRID_W, dim)


TOKEN_TILE = 256
FFN_TILE = 512


def kernel(x, c, ctx, c_ctx, w_ada, b_ada, norm_w, hg_w_in, hg_lb, hg_gnorm, hg_w_out, sg_w_in, sg_ln_w, sg_ln_b,
           sg_w_s, sg_b_s, sg_w_out, moe_router, moe_w_gate, moe_w_up, moe_w_down):
    bsz, t, d = x.shape
    depth = w_ada.shape[0]
    assert depth == 2 and d == N_HEADS * HEAD_DIM and t % GLA_CHUNK == 0 and ctx.shape[1] % GLA_CHUNK == 0
    tm = min(TOKEN_TILE, t)

    cvec = jnp.zeros((SUBLANES, d), F32).at[:bsz].set(c).at[bsz].set(c_ctx)
    mod = _ada(cvec, w_ada, b_ada)

    def mods(layer, rows):
        m = mod[layer, rows].reshape(-1, N_ADA, 1, d)
        return [m[:, k] for k in range(N_ADA)]

    nw = norm_w.reshape(depth, 4, 1, d)
    router_t = jnp.swapaxes(moe_router, 1, 2)

    sh1, sc1, g1, sh2, sc2, g2 = mods(0, slice(0, bsz))
    sh1c, sc1c = [jnp.broadcast_to(m, (bsz, 1, d)) for m in mods(0, slice(bsz, bsz + 1))[:2]]
    w_in_bf = hg_w_in[0].astype(BF16)
    pos = _sincos_2d(t // GRID_W, d)
    tri, masks = _gla_consts()

    cv, ckf, cgf, ckb, cgb, cq, _ = _hg_in(ctx, None, nw[0, 0], sh1c, sc1c, w_in_bf, hg_lb, 1, tm)
    s_zero = jnp.zeros((bsz, 2, N_HEADS, HEAD_DIM, HEAD_DIM), F32)
    _, _, s_ctx = _gla(ckf, cgf, ckb, cgb, cv, cq, s_zero, tri, masks)

    xp, v, kf, gf, kb, gb, q, sg = _hg_in(x, pos, nw[0, 0], sh1, sc1, w_in_bf, hg_lb, 1, tm)
    o_f, o_b, _ = _gla(kf, gf, kb, gb, v, q, s_ctx, tri, masks)
    x1, h2, lg = _hg_out(o_f, o_b, sg, hg_gnorm[0:1], hg_w_out[0].astype(BF16), xp, nw[0, 1], g1, nw[0, 2],
                         sh2, sc2, router_t[0], tm)
    moe = _moe(h2, lg, moe_w_gate, moe_w_up, moe_w_down, 0, FFN_TILE)

    sh1b, sc1b, g1b, sh2b, sc2b, g2b = mods(1, slice(0, bsz))
    x1b, h2b, lgb = _sgu(x1, moe, nw[0, 3], g2, nw[1, 0], sh1b, sc1b, sg_w_in[0].astype(BF16),
                         sg_ln_w[0:1], sg_ln_b[0:1], sg_w_s[0].astype(BF16), sg_b_s[0].T, sg_w_out[0].astype(BF16),
                         nw[1, 1], g1b, nw[1, 2], sh2b, sc2b, router_t[1], tm)
    moe_b = _moe(h2b, lgb, moe_w_gate, moe_w_up, moe_w_down, 1, FFN_TILE)
    return _final(x1b, moe_b, nw[1, 3], g2b, tm)
```

```python
import functools
import math

import jax
import jax.numpy as jnp
from jax import lax
from jax.experimental import pallas as pl
from jax.experimental.pallas import tpu as pltpu

F32 = jnp.float32
BF16 = jnp.bfloat16
I32 = jnp.int32
HIGHEST = lax.Precision.HIGHEST

EPS = 1e-6
GRID_W = 64
N_ADA = 6
N_HEADS = 8
HEAD_DIM = 128
N_EXPERTS = 16
EC_CAPACITY_FACTOR = 2
SGU_CHUNK = 128
SGU_GROUPS = 8

LANES = 128
SUBLANES = 8
GLA_CHUNK = 128
GLA_LEVELS = (64, 32, 16, 8, 4, 2, 1)
GLA_SHORT_BLOCK = 32
GLA_SHORT_MAX_EXPONENT = 60.0
BISECT_ITERS = 160
VMEM_LIMIT = 52 * 1024 * 1024


def _cparams(sem):
    return pltpu.CompilerParams(dimension_semantics=sem, vmem_limit_bytes=VMEM_LIMIT)


def _rms(x, w):
    ms = jnp.mean(x * x, axis=-1, keepdims=True)
    return x * lax.rsqrt(ms + EPS) * w


def _dot(a, b):
    return jnp.dot(a, b, preferred_element_type=F32)


def _dot_nt(a, b):
    return lax.dot_general(a, b, (((1,), (1,)), ((), ())), preferred_element_type=F32)


def _dot_tn(a, b):
    return lax.dot_general(a, b, (((0,), (0,)), ((), ())), preferred_element_type=F32)


def _ada_kernel(c_ref, w_ref, b_ref, o_ref):
    c = c_ref[...]
    s = c * jax.nn.sigmoid(c)
    o_ref[0] = jnp.dot(s, w_ref[0], precision=HIGHEST, preferred_element_type=F32) + b_ref[0]


def _ada(cvec, w_ada, b_ada):
    depth, d, nd = w_ada.shape
    rows = cvec.shape[0]
    return pl.pallas_call(
        _ada_kernel,
        grid=(depth, nd // d),
        in_specs=[pl.BlockSpec((rows, d), lambda l, n: (0, 0)),
                  pl.BlockSpec((1, d, d), lambda l, n: (l, 0, n)),
                  pl.BlockSpec((1, 1, d), lambda l, n: (l, 0, n))],
        out_specs=pl.BlockSpec((1, rows, d), lambda l, n: (l, 0, n)),
        out_shape=jax.ShapeDtypeStruct((depth, rows, nd), F32),
        compiler_params=_cparams(("arbitrary", "arbitrary")),
        name="ada",
    )(cvec, w_ada, b_ada.reshape(depth, 1, nd))


def _hgin_body(x, nw_ref, sh_ref, sc_ref, w_ref, lb_ref, outs, n_lb):
    v_ref, kf_ref, gf_ref, kb_ref, gb_ref, q_ref, sg_ref = outs
    d = x.shape[-1]
    h = _rms(x, nw_ref[...]) * (1.0 + sc_ref[0]) + sh_ref[0]
    hb = h.astype(BF16)
    lbs = lb_ref[...]
    e = jnp.exp(lbs - jnp.max(lbs, axis=0, keepdims=True))
    lb = jnp.sum(e[:n_lb], axis=0) / jnp.sum(e, axis=0)

    v_ref[0] = _dot(hb, w_ref[:, 0:d]).astype(BF16)
    for j, (k_ref, g_ref) in enumerate(((kf_ref, gf_ref), (kb_ref, gb_ref))):
        raw = _dot(hb, w_ref[:, (1 + j) * d:(2 + j) * d])
        lbj = lb[j:j + 1]
        sig = jax.nn.sigmoid(raw)
        f = lbj + (1.0 - lbj) * sig
        k_ref[0] = ((1.0 - lbj) * (1.0 - sig)).astype(BF16)
        g_ref[0] = jnp.log(f)
    qr = _dot(hb, w_ref[:, 3 * d:4 * d])
    q_ref[0] = (qr * jax.nn.sigmoid(qr)).astype(BF16)
    gr = _dot(hb, w_ref[:, 4 * d:5 * d])
    sg_ref[0] = (gr * jax.nn.sigmoid(gr)).astype(BF16)


def _hgin_pos_kernel(x_ref, pos_ref, nw_ref, sh_ref, sc_ref, w_ref, lb_ref, xp_ref, *outs, n_lb):
    x = x_ref[0] + pos_ref[...]
    xp_ref[0] = x
    _hgin_body(x, nw_ref, sh_ref, sc_ref, w_ref, lb_ref, outs, n_lb)


def _hgin_kernel(x_ref, nw_ref, sh_ref, sc_ref, w_ref, lb_ref, *outs, n_lb):
    _hgin_body(x_ref[0], nw_ref, sh_ref, sc_ref, w_ref, lb_ref, outs, n_lb)


def _hg_in(x, pos, nw, sh, sc, w_bf, lb_raw, n_lb, tm):
    bsz, t, d = x.shape
    tm = min(tm, t)
    tok = pl.BlockSpec((1, tm, d), lambda b, i: (b, i, 0))
    vec = pl.BlockSpec((1, d), lambda b, i: (0, 0))
    bvec = pl.BlockSpec((1, 1, d), lambda b, i: (b, 0, 0))
    wspec = pl.BlockSpec(w_bf.shape, lambda b, i: (0, 0))
    lbspec = pl.BlockSpec(lb_raw.shape, lambda b, i: (0, 0, 0))
    gate_shapes = [jax.ShapeDtypeStruct((bsz, t, d), dt) for dt in (BF16, BF16, F32, BF16, F32, BF16, BF16)]
    if pos is not None:
        kern = functools.partial(_hgin_pos_kernel, n_lb=n_lb)
        in_specs = [tok, pl.BlockSpec((tm, d), lambda b, i: (i, 0)), vec, bvec, bvec, wspec, lbspec]
        args = (x, pos, nw, sh, sc, w_bf, lb_raw)
        out_shape = [jax.ShapeDtypeStruct((bsz, t, d), F32)] + gate_shapes
    else:
        kern = functools.partial(_hgin_kernel, n_lb=n_lb)
        in_specs = [tok, vec, bvec, bvec, wspec, lbspec]
        args = (x, nw, sh, sc, w_bf, lb_raw)
        out_shape = gate_shapes
    return pl.pallas_call(
        kern,
        grid=(bsz, t // tm),
        in_specs=in_specs,
        out_specs=[tok] * len(out_shape),
        out_shape=out_shape,
        compiler_params=_cparams(("arbitrary", "arbitrary")),
        name="hg_in",
    )(*args)


def _gla_consts():
    c = GLA_CHUNK
    t = jnp.arange(c)[:, None]
    s = jnp.arange(c)[None, :]
    tri = jnp.stack([(s <= t), (s >= t)]).astype(BF16)
    masks = []
    for rev in (False, True):
        lv = []
        for m in GLA_LEVELS:
            same = (t // (2 * m)) == (s // (2 * m))
            tq = ((t // m) % 2) == (0 if rev else 1)
            sk = ((s // m) % 2) == (1 if rev else 0)
            lv.append(same & tq & sk)
        lv.append(t == s)
        lv.append(((t // GLA_SHORT_BLOCK) == (s // GLA_SHORT_BLOCK)) & ((s >= t) if rev else (s <= t)))
        masks.append(jnp.stack(lv))
    return tri, jnp.stack(masks).astype(F32)


def _split3(x):
    p0 = x.astype(BF16)
    r1 = x - p0.astype(F32)
    p1 = r1.astype(BF16)
    p2 = (r1 - p1.astype(F32)).astype(BF16)
    return p0, p1, p2


def _level_ref(b_scr, hs, m, rev):
    c = GLA_CHUNK
    off = m if rev else m - 1

    def row8(i):
        return jnp.broadcast_to(b_scr[pl.ds(i, 1), hs], (SUBLANES, HEAD_DIM))

    pieces = []
    if m >= SUBLANES:
        for blk in range(c // (2 * m)):
            r8 = row8(blk * 2 * m + off)
            pieces.extend([r8] * (2 * m // SUBLANES))
    else:
        sub = lax.broadcasted_iota(I32, (SUBLANES, HEAD_DIM), 0) // (2 * m)
        for grp in range(c // SUBLANES):
            piece = row8(grp * SUBLANES + off)
            for cls in range(1, SUBLANES // (2 * m)):
                piece = jnp.where(sub == cls, row8(grp * SUBLANES + cls * 2 * m + off), piece)
            pieces.append(piece)
    return jnp.concatenate(pieces, axis=0)


def _block_ref(b_scr, hs, rev):
    c, blk = GLA_CHUNK, GLA_SHORT_BLOCK
    pieces = []
    for i in range(c // blk):
        r = (i + 1) * blk if rev else i * blk - 1
        if 0 <= r < c:
            pieces.append(jnp.broadcast_to(b_scr[pl.ds(r, 1), hs], (blk, HEAD_DIM)))
        else:
            pieces.append(jnp.zeros((blk, HEAD_DIM), F32))
    return jnp.concatenate(pieces, axis=0)


def _block_decay_bound(b_scr, rev):
    c, blk = GLA_CHUNK, GLA_SHORT_BLOCK
    worst = None
    for i in range(c // blk):
        inner = b_scr[pl.ds(i * blk if rev else (i + 1) * blk - 1, 1), :]
        r = (i + 1) * blk if rev else i * blk - 1
        span = jnp.abs(inner - b_scr[pl.ds(r, 1), :]) if 0 <= r < c else jnp.abs(inner)
        worst = span if worst is None else jnp.maximum(worst, span)
    return jnp.max(worst)


def _gla_direction(q_ref, k_ref, v_ref, g_ref, o_ref, tri_ref, mask_ref, st_scr, b_scr, d_idx, rev):
    c = GLA_CHUNK
    g = g_ref[0]
    tri = tri_ref[d_idx]
    p0, p1, p2 = _split3(g)
    b_scr[...] = _dot(tri, p0) + _dot(tri, p1) + _dot(tri, p2)
    n_lv = len(GLA_LEVELS)

    def head(h, short):
        hs = pl.ds(pl.multiple_of(h * HEAD_DIM, HEAD_DIM), HEAD_DIM)
        b = b_scr[:, hs]
        q = q_ref[0, :, hs]
        k = k_ref[0, :, hs]
        vb = v_ref[0, :, hs]
        bl = b_scr[pl.ds(0 if rev else c - 1, 1), hs]
        qt = q * jnp.exp(b).astype(BF16)
        kt = k * jnp.exp(bl - b).astype(BF16)
        st = st_scr[d_idx, h]
        o = _dot_nt(qt, st.astype(BF16))

        def level(m):
            e = jnp.exp(-jnp.abs(b - _level_ref(b_scr, hs, m, rev))).astype(BF16)
            return _dot_nt(q * e, k * e)

        if short:
            u = _block_ref(b_scr, hs, rev) - b
            within = _dot_nt(q * jnp.exp(-u).astype(BF16), k * jnp.exp(u).astype(BF16))
            a = jnp.where(mask_ref[d_idx, n_lv + 1] > 0.0, within, 0.0)
            for li, m in enumerate(GLA_LEVELS):
                if m >= GLA_SHORT_BLOCK:
                    a = a + mask_ref[d_idx, li] * level(m)
        else:
            a = mask_ref[d_idx, n_lv] * _dot_nt(q, k)
            for li, m in enumerate(GLA_LEVELS):
                a = a + mask_ref[d_idx, li] * level(m)
        o = o + _dot(a.astype(BF16), vb)
        o_ref[0, :, hs] = o
        st_scr[d_idx, h] = st * jnp.exp(bl) + _dot_tn(vb, kt)

    bounded = _block_decay_bound(b_scr, rev) <= GLA_SHORT_MAX_EXPONENT

    @pl.when(bounded)
    def _():
        lax.fori_loop(0, N_HEADS, lambda h, cr: (head(h, True), cr)[1], 0, unroll=8)

    @pl.when(jnp.logical_not(bounded))
    def _():
        lax.fori_loop(0, N_HEADS, lambda h, cr: (head(h, False), cr)[1], 0, unroll=2)


def _gla_kernel(kf_ref, gf_ref, vf_ref, qf_ref, kb_ref, gb_ref, vb_ref, qb_ref, s0_ref, tri_ref, mask_ref,
                of_ref, ob_ref, sout_ref, st_scr, bf_scr, bb_scr):
    n = pl.program_id(1)

    @pl.when(n == 0)
    def _():
        st_scr[...] = s0_ref[0]

    _gla_direction(qf_ref, kf_ref, vf_ref, gf_ref, of_ref, tri_ref, mask_ref, st_scr, bf_scr, 0, False)
    _gla_direction(qb_ref, kb_ref, vb_ref, gb_ref, ob_ref, tri_ref, mask_ref, st_scr, bb_scr, 1, True)

    @pl.when(n == pl.num_programs(1) - 1)
    def _():
        sout_ref[0] = st_scr[...]


def _gla(kf, gf, kb, gb, v, q, s0, tri, masks):
    bsz, t, d = v.shape
    c = GLA_CHUNK
    n = t // c
    fwd = pl.BlockSpec((1, c, d), lambda b, i: (b, i, 0))
    bwd = pl.BlockSpec((1, c, d), lambda b, i: (b, n - 1 - i, 0))
    sspec = pl.BlockSpec((1,) + s0.shape[1:], lambda b, i: (b, 0, 0, 0, 0))
    return pl.pallas_call(
        _gla_kernel,
        grid=(bsz, n),
        in_specs=[fwd, fwd, fwd, fwd, bwd, bwd, bwd, bwd, sspec,
                  pl.BlockSpec(tri.shape, lambda b, i: (0, 0, 0)),
                  pl.BlockSpec(masks.shape, lambda b, i: (0, 0, 0, 0))],
        out_specs=[fwd, bwd, sspec],
        out_shape=[jax.ShapeDtypeStruct((bsz, t, d), F32), jax.ShapeDtypeStruct((bsz, t, d), F32),
                   jax.ShapeDtypeStruct(s0.shape, F32)],
        scratch_shapes=[pltpu.VMEM(s0.shape[1:], F32), pltpu.VMEM((c, d), F32), pltpu.VMEM((c, d), F32)],
        compiler_params=_cparams(("arbitrary", "arbitrary")),
        name="gla",
    )(kf, gf, v, q, kb, gb, v, q, s0, tri, masks)


def _store_tile_rows(ref2, x):
    n, d = x.shape
    nt = d // LANES
    for i in range(n // SUBLANES):
        for c in range(nt):
            r0 = (i * nt + c) * SUBLANES
            ref2[r0:r0 + SUBLANES, :] = x[i * SUBLANES:(i + 1) * SUBLANES, c * LANES:(c + 1) * LANES]


def _load_tile_rows(ref2, n, d):
    nt = d // LANES
    rows = []
    for i in range(n // SUBLANES):
        rows.append(jnp.concatenate(
            [ref2[(i * nt + c) * SUBLANES:(i * nt + c + 1) * SUBLANES, :] for c in range(nt)], axis=1))
    return jnp.concatenate(rows, axis=0)


def _tile_row_base(t, nt):
    return (t >> 3) * (nt * SUBLANES) + (t & (SUBLANES - 1))


def _tile_rows_spec(tm, d, index_map):
    return pl.BlockSpec((1, tm * (d // LANES), LANES), index_map)


def _post_mixer(y, xres, rows, nw1_ref, g1_ref, nw2_ref, sh2_ref, sc2_ref, rhi_ref, rlo_ref, x1_ref, h2_ref, lg_ref):
    d = y.shape[-1]
    nt = d // LANES
    x1 = xres + g1_ref[0] * _rms(y, nw1_ref[...])
    x1_ref[0, rows, :] = x1
    h2 = _rms(x1, nw2_ref[...]) * (1.0 + sc2_ref[0]) + sh2_ref[0]
    _store_tile_rows(h2_ref.at[0, pl.ds(rows.start * nt, (rows.stop - rows.start) * nt)], h2)
    h_hi = h2.astype(BF16)
    h_lo = (h2 - h_hi.astype(F32)).astype(BF16)
    lg_ref[0, rows, :] = _dot(h_hi, rhi_ref[...]) + (_dot(h_lo, rhi_ref[...]) + _dot(h_hi, rlo_ref[...]))


def _hgout_kernel(of_ref, ob_ref, sg_ref, gn_ref, w_ref, xp_ref, nw1_ref, g1_ref, nw2_ref, sh2_ref, sc2_ref,
                  rhi_ref, rlo_ref, x1_ref, h2_ref, lg_ref, z_scr):
    o = of_ref[0] + ob_ref[0]
    for h in range(N_HEADS):
        hs = slice(h * HEAD_DIM, (h + 1) * HEAD_DIM)
        oh = o[:, hs]
        ms = jnp.mean(oh * oh, axis=-1, keepdims=True)
        z = oh * lax.rsqrt(ms + EPS) * gn_ref[:, hs] * sg_ref[0, :, hs].astype(F32)
        z_scr[:, hs] = z.astype(BF16)
    y = _dot(z_scr[...], w_ref[...])
    _post_mixer(y, xp_ref[0], slice(0, y.shape[0]), nw1_ref, g1_ref, nw2_ref, sh2_ref, sc2_ref, rhi_ref, rlo_ref,
                x1_ref, h2_ref, lg_ref)


def _hg_out(o_f, o_b, sg, gnorm, w_out_bf, xp, nw1, g1, nw2, sh2, sc2, r_hi, r_lo, tm):
    bsz, t, d = xp.shape
    rspec = pl.BlockSpec((d, LANES), lambda b, i: (0, 0))
    tok = pl.BlockSpec((1, tm, d), lambda b, i: (b, i, 0))
    vec = pl.BlockSpec((1, d), lambda b, i: (0, 0))
    bvec = pl.BlockSpec((1, 1, d), lambda b, i: (b, 0, 0))
    return pl.pallas_call(
        _hgout_kernel,
        grid=(bsz, t // tm),
        in_specs=[tok, tok, tok, vec, pl.BlockSpec((d, d), lambda b, i: (0, 0)), tok, vec, bvec, vec, bvec, bvec,
                  rspec, rspec],
        out_specs=[tok, _tile_rows_spec(tm, d, lambda b, i: (b, i, 0)),
                   pl.BlockSpec((1, tm, LANES), lambda b, i: (b, i, 0))],
        out_shape=[jax.ShapeDtypeStruct((bsz, t, d), F32), jax.ShapeDtypeStruct((bsz, t * (d // LANES), LANES), F32),
                   jax.ShapeDtypeStruct((bsz, t, LANES), F32)],
        scratch_shapes=[pltpu.VMEM((tm, d), BF16)],
        compiler_params=_cparams(("arbitrary", "arbitrary")),
        name="hg_out",
    )(o_f, o_b, sg, gnorm, w_out_bf, xp, nw1, g1, nw2, sh2, sc2, r_hi, r_lo)


def _sgu_kernel(x1p_ref, moe_ref, nw3p_ref, g2p_ref, nw0_ref, sh1_ref, sc1_ref, win_ref, lnw_ref, lnb_ref,
                ws_ref, bs_ref, wout_ref, nw1_ref, g1_ref, nw2_ref, sh2_ref, sc2_ref, rhi_ref, rlo_ref,
                x1_ref, h2_ref, lg_ref):
    tm, d = x1p_ref.shape[1], x1p_ref.shape[2]
    w = lnw_ref.shape[-1]
    gd = w // SGU_GROUPS
    nt = d // LANES

    def gelu(z):
        return 0.5 * z * (1.0 + jnp.tanh(math.sqrt(2.0 / math.pi) * (z + 0.044715 * (z * z * z))))

    for ck in range(tm // SGU_CHUNK):
        rs = slice(ck * SGU_CHUNK, (ck + 1) * SGU_CHUNK)
        moe = _load_tile_rows(moe_ref.at[0, pl.ds(ck * SGU_CHUNK * nt, SGU_CHUNK * nt)], SGU_CHUNK, d)
        x = x1p_ref[0, rs, :] + g2p_ref[0] * _rms(moe, nw3p_ref[...])
        h = _rms(x, nw0_ref[...]) * (1.0 + sc1_ref[0]) + sh1_ref[0]
        hb = h.astype(BF16)
        u = gelu(_dot(hb, win_ref[:, 0:w]))
        v = gelu(_dot(hb, win_ref[:, w:2 * w]))
        mu = jnp.mean(v, axis=-1, keepdims=True)
        vc = v - mu
        vn = vc * lax.rsqrt(jnp.mean(vc * vc, axis=-1, keepdims=True) + EPS) * lnw_ref[...] + lnb_ref[...]
        vnb = vn.astype(BF16)
        gated = []
        for g in range(SGU_GROUPS):
            cs = slice(g * gd, (g + 1) * gd)
            mixed = _dot(ws_ref[g], vnb[:, cs]) + bs_ref[:, g:g + 1]
            gated.append((u[:, cs] * mixed).astype(BF16))
        y = _dot(jnp.concatenate(gated, axis=1), wout_ref[...])
        _post_mixer(y, x, rs, nw1_ref, g1_ref, nw2_ref, sh2_ref, sc2_ref, rhi_ref, rlo_ref, x1_ref, h2_ref, lg_ref)


def _sgu(x1p, moe, nw3p, g2p, nw0, sh1, sc1, w_in_bf, ln_w, ln_b, w_s_bf, b_s_t, w_out_bf,
         nw1, g1, nw2, sh2, sc2, r_hi, r_lo, tm):
    bsz, t, d = x1p.shape
    w = ln_w.shape[-1]
    tok = pl.BlockSpec((1, tm, d), lambda b, i: (b, i, 0))
    tiles = _tile_rows_spec(tm, d, lambda b, i: (b, i, 0))
    vec = pl.BlockSpec((1, d), lambda b, i: (0, 0))
    wvec = pl.BlockSpec((1, w), lambda b, i: (0, 0))
    bvec = pl.BlockSpec((1, 1, d), lambda b, i: (b, 0, 0))

    def full(a):
        return pl.BlockSpec(a.shape, lambda b, i: (0,) * a.ndim)

    return pl.pallas_call(
        _sgu_kernel,
        grid=(bsz, t // tm),
        in_specs=[tok, tiles, vec, bvec, vec, bvec, bvec, full(w_in_bf), wvec, wvec, full(w_s_bf), full(b_s_t),
                  full(w_out_bf), vec, bvec, vec, bvec, bvec, full(r_hi), full(r_lo)],
        out_specs=[tok, tiles, pl.BlockSpec((1, tm, LANES), lambda b, i: (b, i, 0))],
        out_shape=[jax.ShapeDtypeStruct((bsz, t, d), F32), jax.ShapeDtypeStruct((bsz, t * (d // LANES), LANES), F32),
                   jax.ShapeDtypeStruct((bsz, t, LANES), F32)],
        compiler_params=_cparams(("arbitrary", "arbitrary")),
        name="sgu",
    )(x1p, moe, nw3p, g2p, nw0, sh1, sc1, w_in_bf, ln_w, ln_b, w_s_bf, b_s_t, w_out_bf,
      nw1, g1, nw2, sh2, sc2, r_hi, r_lo)


def _token_prefix(mask, triu, slow):
    local = _dot(mask.astype(BF16), triu)
    rowtot = jnp.broadcast_to(local[:, LANES - 1:LANES], local.shape)
    prev = _dot(slow, rowtot.astype(BF16))
    return local, prev, rowtot


def _route_kernel(lg_ref, triu_ref, slow_ref, idx_ref, gate_ref, a_scr, thr_scr, rhs_scr, *, cap):
    ne, nr = a_scr.shape[0], a_scr.shape[1]
    for r in range(nr):
        a_scr[:, r, :] = lg_ref[0, r * LANES:(r + 1) * LANES, :].T[0:ne, :]
    lg = a_scr[...]
    ex = jnp.exp(lg - jnp.max(lg, axis=0, keepdims=True))
    a = ex / jnp.sum(ex, axis=0, keepdims=True)
    a_scr[...] = a
    capf = jnp.float32(cap)

    def count(m):
        return jnp.sum(jnp.sum(m.astype(F32), axis=1, keepdims=True), axis=2, keepdims=True)

    def bis(_, lohi):
        lo, hi = lohi
        mid = 0.5 * (lo + hi)
        ge = count(a >= mid) >= capf
        return jnp.where(ge, mid, lo), jnp.where(ge, hi, mid)

    lo, _ = lax.fori_loop(0, BISECT_ITERS, bis, (jnp.zeros((ne, 1, 1), F32), jnp.full((ne, 1, 1), 2.0, F32)))
    thr_scr[...] = jnp.broadcast_to(lo, thr_scr.shape)

    idx_ref[0] = jnp.zeros(idx_ref.shape[1:], I32)
    gate_ref[0] = jnp.zeros(gate_ref.shape[1:], F32)
    triu = triu_ref[...]
    slow = slow_ref[...]
    lane = lax.broadcasted_iota(I32, (LANES, LANES), 1)
    lane_f = lane.astype(F32)
    sub_f = lax.broadcasted_iota(I32, (LANES, LANES), 0).astype(F32)
    rowid = lax.broadcasted_iota(I32, (nr, LANES), 0).astype(F32)

    def per_expert(e, carry):
        ae = a_scr[e]
        v = thr_scr[e][0:1, :]
        gt = ae > v
        eq = ae == v
        need = capf - jnp.sum(jnp.sum(gt.astype(F32), axis=0, keepdims=True), axis=1, keepdims=True)
        eql, eqp, _ = _token_prefix(eq, triu, slow)
        sel = gt | (eq & ((eql + eqp - eq.astype(F32)) < need))
        local, prev, rowtot = _token_prefix(sel, triu, slow)
        rowcum = prev + rowtot
        prev_hi = jnp.floor(prev * (1.0 / LANES))
        a0, a1, a2 = _split3(ae)
        for k, piece in enumerate((local, prev_hi, prev - LANES * prev_hi, rowid, a0, a1, a2)):
            rhs_scr[:, k * LANES:(k + 1) * LANES] = piece.astype(BF16)

        for p in range(cap // LANES):
            base = float(p * LANES + 1)
            slot_row = base + lane_f[0:1, :]
            onehot_t = ((prev < slot_row) & (rowcum >= slot_row)).astype(BF16)
            g = _dot_tn(onehot_t, rhs_scr[...])
            g_local = g[:, 0:LANES]
            g_prev = LANES * g[:, LANES:2 * LANES] + g[:, 2 * LANES:3 * LANES]
            g_row = g[:, 3 * LANES:4 * LANES]
            g_a = g[:, 4 * LANES:5 * LANES] + g[:, 5 * LANES:6 * LANES] + g[:, 6 * LANES:7 * LANES]
            slot_col = base + sub_f
            lstar = jnp.sum(((g_local + g_prev) < slot_col).astype(F32), axis=1, keepdims=True)
            tok = LANES * g_row[:, 0:1] + lstar
            gat = jnp.sum(jnp.where(lane_f == lstar, g_a, 0.0), axis=1, keepdims=True)
            rs = slice(p * LANES, (p + 1) * LANES)
            idx_ref[0, rs, :] = jnp.where(lane == e, tok.astype(I32), idx_ref[0, rs, :])
            gate_ref[0, rs, :] = jnp.where(lane == e, gat, gate_ref[0, rs, :])
        return carry

    lax.fori_loop(0, ne, per_expert, 0)


def _route(logits, ne, cap):
    bsz, t, _ = logits.shape
    nr = t // LANES
    assert cap % LANES == 0 and nr % SUBLANES == 0
    triu = (jnp.arange(LANES)[:, None] <= jnp.arange(LANES)[None, :]).astype(BF16)
    slow = (jnp.arange(nr)[:, None] > jnp.arange(nr)[None, :]).astype(BF16)
    idx, gate = pl.pallas_call(
        functools.partial(_route_kernel, cap=cap),
        grid=(bsz,),
        in_specs=[pl.BlockSpec((1, t, LANES), lambda b: (b, 0, 0)),
                  pl.BlockSpec((LANES, LANES), lambda b: (0, 0)), pl.BlockSpec((nr, nr), lambda b: (0, 0))],
        out_specs=[pl.BlockSpec((1, cap, LANES), lambda b: (b, 0, 0))] * 2,
        out_shape=[jax.ShapeDtypeStruct((bsz, cap, LANES), I32), jax.ShapeDtypeStruct((bsz, cap, LANES), F32)],
        scratch_shapes=[pltpu.VMEM((ne, nr, LANES), F32), pltpu.VMEM((ne, SUBLANES, LANES), F32),
                        pltpu.VMEM((nr, 7 * LANES), BF16)],
        compiler_params=_cparams(("arbitrary",)),
        name="route",
    )(logits, triu, slow)
    return idx, gate


ROW_UNROLL = 8


def _gather_kernel(src_ref, h_hbm, o_ref, h_scr, row_scr, sem, *, cap, ne, nt):
    b, e = pl.program_id(0), pl.program_id(1)

    @pl.when(e == 0)
    def _():
        cp = pltpu.make_async_copy(h_hbm.at[b], h_scr, sem.at[0])
        cp.start()
        cp.wait()

    base = (b * ne + e) * cap
    group = nt * SUBLANES

    def body(jj, carry):
        for u in range(SUBLANES):
            row_scr[pl.ds(jj * group + u, nt, stride=SUBLANES), :] = \
                h_scr[pl.ds(src_ref[base + jj * SUBLANES + u], nt, stride=SUBLANES), :]
        return carry

    lax.fori_loop(0, cap // SUBLANES, body, 0)

    def emit(g, carry):
        rows = [jnp.concatenate([row_scr[pl.ds((g * 2 + i) * group + c * SUBLANES, SUBLANES), :] for c in range(nt)],
                                axis=1) for i in range(2)]
        o_ref[0, 0, pl.ds(g * 2 * SUBLANES, 2 * SUBLANES), :] = jnp.concatenate(rows, axis=0).astype(BF16)
        return carry

    lax.fori_loop(0, cap // (2 * SUBLANES), emit, 0)


def _gather(src_flat, h2_rows, ne, cap, t):
    bsz, rows, _ = h2_rows.shape
    nt = rows // t
    return pl.pallas_call(
        functools.partial(_gather_kernel, cap=cap, ne=ne, nt=nt),
        grid_spec=pltpu.PrefetchScalarGridSpec(
            num_scalar_prefetch=1,
            grid=(bsz, ne),
            in_specs=[pl.BlockSpec(memory_space=pl.ANY)],
            out_specs=pl.BlockSpec((1, 1, cap, nt * LANES), lambda b, e, src: (b, e, 0, 0)),
            scratch_shapes=[pltpu.VMEM((rows, LANES), F32), pltpu.VMEM((cap * nt, LANES), F32),
                            pltpu.SemaphoreType.DMA((1,))],
        ),
        out_shape=jax.ShapeDtypeStruct((bsz, ne, cap, nt * LANES), BF16),
        compiler_params=_cparams(("arbitrary", "arbitrary")),
        name="gather",
    )(src_flat, h2_rows)


def _ffn_kernel(xs_ref, wg_ref, wu_ref, wd_ref, y_ref):
    f = pl.program_id(1)
    wg = wg_ref[0, 0].astype(BF16)
    wu = wu_ref[0, 0].astype(BF16)
    wd = wd_ref[0, 0].astype(BF16)
    for b in range(xs_ref.shape[0]):
        xb = xs_ref[b, 0]
        g = _dot(xb, wg)
        u = _dot(xb, wu)
        hid = (g * jax.nn.sigmoid(g) * u).astype(BF16)
        part = _dot(hid, wd)

        @pl.when(f == 0)
        def _():
            y_ref[b, 0] = part

        @pl.when(f > 0)
        def _():
            y_ref[b, 0] += part


def _ffn(xs, w_gate, w_up, w_down, layer, tf):
    bsz, ne, cap, d = xs.shape
    ff = w_gate.shape[-1]
    tf = min(tf, ff)
    tok = pl.BlockSpec((bsz, 1, cap, d), lambda e, f: (0, e, 0, 0))
    return pl.pallas_call(
        _ffn_kernel,
        grid=(ne, ff // tf),
        in_specs=[tok,
                  pl.BlockSpec((1, 1, d, tf), lambda e, f: (layer, e, 0, f)),
                  pl.BlockSpec((1, 1, d, tf), lambda e, f: (layer, e, 0, f)),
                  pl.BlockSpec((1, 1, tf, d), lambda e, f: (layer, e, f, 0))],
        out_specs=tok,
        out_shape=jax.ShapeDtypeStruct((bsz, ne, cap, d), F32),
        compiler_params=_cparams(("arbitrary", "arbitrary")),
        name="ffn",
    )(xs, w_gate, w_up, w_down)


SCATTER_SPLIT = 8


def _scatter_kernel(dst_ref, cut_ref, y_ref, gate_ref, o_hbm, *scr, cap, ne, nt):
    accs, y_scr, sem = scr[:SCATTER_SPLIT], scr[SCATTER_SPLIT], scr[SCATTER_SPLIT + 1]
    b, e = pl.program_id(0), pl.program_id(1)
    group = nt * SUBLANES

    @pl.when(e == 0)
    def _():
        for acc in accs:
            acc[...] = jnp.zeros(acc.shape, F32)

    lane = lax.broadcasted_iota(I32, (SUBLANES, LANES), 1)

    def gated(i, carry):
        rs = pl.ds(pl.multiple_of(i * SUBLANES, SUBLANES), SUBLANES)
        gate = jnp.sum(jnp.where(lane == e, gate_ref[0, rs, :], 0.0), axis=1, keepdims=True)
        rows = y_ref[0, 0, rs, :]
        for c in range(nt):
            y_scr[pl.ds(i * group + c * SUBLANES, SUBLANES), :] = gate * rows[:, c * LANES:(c + 1) * LANES]
        return carry

    lax.fori_loop(0, cap // SUBLANES, gated, 0, unroll=8)

    base = (b * ne + e) * cap
    cbase = (b * ne + e) * (SCATTER_SPLIT + 1)
    starts = [cut_ref[cbase + k] for k in range(SCATTER_SPLIT)]
    counts = [cut_ref[cbase + k + 1] - starts[k] for k in range(SCATTER_SPLIT)]
    shortest = functools.reduce(jnp.minimum, counts)
    longest = functools.reduce(jnp.maximum, counts)

    def update(acc, j, dst, scale=None):
        src = _tile_row_base(j, nt)
        row = y_scr[pl.ds(src, nt, stride=SUBLANES), :]
        acc[pl.ds(dst, nt, stride=SUBLANES), :] += row if scale is None else scale * row

    def common(i, carry):
        for k, acc in enumerate(accs):
            j = starts[k] + i
            update(acc, j, dst_ref[base + j])
        return carry

    def tail(i, carry):
        for k, acc in enumerate(accs):
            valid = i < counts[k]
            j = jnp.minimum(starts[k] + i, cap - 1)
            update(acc, j, jnp.where(valid, dst_ref[base + j], 0), jnp.where(valid, 1.0, 0.0))
        return carry

    lax.fori_loop(0, shortest, common, 0)
    lax.fori_loop(shortest, longest, tail, 0)

    @pl.when(e == ne - 1)
    def _():
        part = accs[0].shape[0]
        copies = [pltpu.make_async_copy(acc, o_hbm.at[b, pl.ds(k * part, part)], sem.at[k])
                  for k, acc in enumerate(accs)]
        for cp in copies:
            cp.start()
        for cp in copies:
            cp.wait()


def _scatter(dst_flat, cuts_flat, y, gate_cols, t):
    bsz, ne, cap, d = y.shape
    nt = d // LANES
    part_rows = (t // SCATTER_SPLIT) * nt
    return pl.pallas_call(
        functools.partial(_scatter_kernel, cap=cap, ne=ne, nt=nt),
        grid_spec=pltpu.PrefetchScalarGridSpec(
            num_scalar_prefetch=2,
            grid=(bsz, ne),
            in_specs=[pl.BlockSpec((1, 1, cap, d), lambda b, e, i, c: (b, e, 0, 0)),
                      pl.BlockSpec((1, cap, LANES), lambda b, e, i, c: (b, 0, 0))],
            out_specs=pl.BlockSpec(memory_space=pl.ANY),
            scratch_shapes=[pltpu.VMEM((part_rows, LANES), F32)] * SCATTER_SPLIT
            + [pltpu.VMEM((cap * nt, LANES), F32), pltpu.SemaphoreType.DMA((SCATTER_SPLIT,))],
        ),
        out_shape=jax.ShapeDtypeStruct((bsz, t * nt, LANES), F32),
        compiler_params=_cparams(("arbitrary", "arbitrary")),
        name="scatter",
    )(dst_flat, cuts_flat, y, gate_cols)


def _moe(h2_rows, logits, ne, w_gate, w_up, w_down, layer, tf):
    bsz, t, _ = logits.shape
    nt = h2_rows.shape[1] // t
    cap = EC_CAPACITY_FACTOR * t // ne
    tpart = t // SCATTER_SPLIT
    idx_cols, gate_cols = _route(logits, ne, cap)
    idx = jnp.swapaxes(idx_cols[:, :, :ne], 1, 2)
    xs = _gather(_tile_row_base(idx, nt).reshape(-1), h2_rows, ne, cap, t)
    y = _ffn(xs, w_gate, w_up, w_down, layer, tf)
    edges = jnp.arange(SCATTER_SPLIT + 1, dtype=I32) * tpart
    cuts = jnp.sum(idx[..., None] < edges, axis=2).astype(I32)
    return _scatter(_tile_row_base(idx % tpart, nt).reshape(-1), cuts.reshape(-1), y, gate_cols, t)


def _final_kernel(x_ref, moe_ref, nw_ref, g_ref, o_ref):
    x = x_ref[0]
    o_ref[0] = x + g_ref[0] * _rms(_load_tile_rows(moe_ref.at[0], x.shape[0], x.shape[1]), nw_ref[...])


def _final(x1, moe, nw3, g2, tm):
    bsz, t, d = x1.shape
    tok = pl.BlockSpec((1, tm, d), lambda b, i: (b, i, 0))
    return pl.pallas_call(
        _final_kernel,
        grid=(bsz, t // tm),
        in_specs=[tok, _tile_rows_spec(tm, d, lambda b, i: (b, i, 0)), pl.BlockSpec((1, d), lambda b, i: (0, 0)),
                  pl.BlockSpec((1, 1, d), lambda b, i: (b, 0, 0))],
        out_specs=tok,
        out_shape=jax.ShapeDtypeStruct((bsz, t, d), F32),
        compiler_params=_cparams(("arbitrary", "arbitrary")),
        name="final",
    )(x1, moe, nw3, g2)


def _sincos_2d(rows, dim):
    quarter = dim // 4
    half = dim // 2
    freqs = jnp.exp(-math.log(10000.0) * jnp.arange(quarter, dtype=F32) / quarter)

    def emb1d(n):
        ang = jnp.arange(n, dtype=F32)[:, None] * freqs[None, :]
        return jnp.concatenate([jnp.sin(ang), jnp.cos(ang)], axis=-1)

    er = emb1d(rows)
    ec = emb1d(GRID_W)
    pos = jnp.concatenate([jnp.broadcast_to(er[:, None, :], (rows, GRID_W, half)),
                           jnp.broadcast_to(ec[None, :, :], (rows, GRID_W, half))], axis=-1)
    return pos.reshape(rows * GRID_W, dim)


TOKEN_TILE = 256
FFN_TILE = 512


def kernel(x, c, ctx, c_ctx, w_ada, b_ada, norm_w, hg_w_in, hg_lb, hg_gnorm, hg_w_out, sg_w_in, sg_ln_w, sg_ln_b,
           sg_w_s, sg_b_s, sg_w_out, moe_router, moe_w_gate, moe_w_up, moe_w_down):
    bsz, t, d = x.shape
    depth = w_ada.shape[0]
    assert depth == 2 and d == N_HEADS * HEAD_DIM and t % GLA_CHUNK == 0 and ctx.shape[1] % GLA_CHUNK == 0
    tm = min(TOKEN_TILE, t)

    cvec = jnp.zeros((SUBLANES, d), F32).at[:bsz].set(c).at[bsz].set(c_ctx)
    mod = _ada(cvec, w_ada, b_ada)

    def mods(layer, rows):
        m = mod[layer, rows].reshape(-1, N_ADA, 1, d)
        return [m[:, k] for k in range(N_ADA)]

    nw = norm_w.reshape(depth, 4, 1, d)
    ne = moe_router.shape[-1]
    router = jnp.zeros((depth, d, LANES), F32).at[:, :, :ne].set(moe_router)
    r_hi = router.astype(BF16)
    r_lo = (router - r_hi.astype(F32)).astype(BF16)

    sh1, sc1, g1, sh2, sc2, g2 = mods(0, slice(0, bsz))
    sh1c, sc1c = [jnp.broadcast_to(m, (bsz, 1, d)) for m in mods(0, slice(bsz, bsz + 1))[:2]]
    w_in_bf = hg_w_in[0].astype(BF16)
    pos = _sincos_2d(t // GRID_W, d)
    tri, masks = _gla_consts()

    cv, ckf, cgf, ckb, cgb, cq, _ = _hg_in(ctx, None, nw[0, 0], sh1c, sc1c, w_in_bf, hg_lb, 1, tm)
    s_zero = jnp.zeros((bsz, 2, N_HEADS, HEAD_DIM, HEAD_DIM), F32)
    _, _, s_ctx = _gla(ckf, cgf, ckb, cgb, cv, cq, s_zero, tri, masks)

    xp, v, kf, gf, kb, gb, q, sg = _hg_in(x, pos, nw[0, 0], sh1, sc1, w_in_bf, hg_lb, 1, tm)
    o_f, o_b, _ = _gla(kf, gf, kb, gb, v, q, s_ctx, tri, masks)
    x1, h2, lg = _hg_out(o_f, o_b, sg, hg_gnorm[0:1], hg_w_out[0].astype(BF16), xp, nw[0, 1], g1, nw[0, 2],
                         sh2, sc2, r_hi[0], r_lo[0], tm)
    moe = _moe(h2, lg, ne, moe_w_gate, moe_w_up, moe_w_down, 0, FFN_TILE)

    sh1b, sc1b, g1b, sh2b, sc2b, g2b = mods(1, slice(0, bsz))
    x1b, h2b, lgb = _sgu(x1, moe, nw[0, 3], g2, nw[1, 0], sh1b, sc1b, sg_w_in[0].astype(BF16),
                         sg_ln_w[0:1], sg_ln_b[0:1], sg_w_s[0].astype(BF16), sg_b_s[0].T, sg_w_out[0].astype(BF16),
                         nw[1, 1], g1b, nw[1, 2], sh2b, sc2b, r_hi[1], r_lo[1], tm)
    moe_b = _moe(h2b, lgb, ne, moe_w_gate, moe_w_up, moe_w_down, 1, FFN_TILE)
    return _final(x1b, moe_b, nw[1, 3], g2b, tm)
```

```python
import functools
import math

import jax
import jax.numpy as jnp
from jax import lax
from jax.experimental import pallas as pl
from jax.experimental.pallas import tpu as pltpu

F32 = jnp.float32
BF16 = jnp.bfloat16
I32 = jnp.int32
HIGHEST = lax.Precision.HIGHEST

EPS = 1e-6
GRID_W = 64
N_ADA = 6
N_HEADS = 8
HEAD_DIM = 128
N_EXPERTS = 16
EC_CAPACITY_FACTOR = 2
SGU_CHUNK = 128
SGU_GROUPS = 8

LANES = 128
SUBLANES = 8
GLA_CHUNK = 128
GLA_LEVELS = (64, 32, 16, 8, 4, 2, 1)
GLA_SHORT_BLOCK = 32
GLA_SHORT_MAX_EXPONENT = 60.0
BISECT_ITERS = 160
VMEM_LIMIT = 52 * 1024 * 1024


def _cparams(sem):
    return pltpu.CompilerParams(dimension_semantics=sem, vmem_limit_bytes=VMEM_LIMIT)


def _rms(x, w):
    ms = jnp.mean(x * x, axis=-1, keepdims=True)
    return x * lax.rsqrt(ms + EPS) * w


def _dot(a, b):
    return jnp.dot(a, b, preferred_element_type=F32)


def _dot_nt(a, b):
    return lax.dot_general(a, b, (((1,), (1,)), ((), ())), preferred_element_type=F32)


def _dot_tn(a, b):
    return lax.dot_general(a, b, (((0,), (0,)), ((), ())), preferred_element_type=F32)


def _ada_kernel(c_ref, w_ref, b_ref, o_ref):
    c = c_ref[...]
    s = c * jax.nn.sigmoid(c)
    o_ref[0] = jnp.dot(s, w_ref[0], precision=HIGHEST, preferred_element_type=F32) + b_ref[0]


def _ada(cvec, w_ada, b_ada):
    depth, d, nd = w_ada.shape
    rows = cvec.shape[0]
    return pl.pallas_call(
        _ada_kernel,
        grid=(depth, nd // d),
        in_specs=[pl.BlockSpec((rows, d), lambda l, n: (0, 0)),
                  pl.BlockSpec((1, d, d), lambda l, n: (l, 0, n)),
                  pl.BlockSpec((1, 1, d), lambda l, n: (l, 0, n))],
        out_specs=pl.BlockSpec((1, rows, d), lambda l, n: (l, 0, n)),
        out_shape=jax.ShapeDtypeStruct((depth, rows, nd), F32),
        compiler_params=_cparams(("arbitrary", "arbitrary")),
        name="ada",
    )(cvec, w_ada, b_ada.reshape(depth, 1, nd))


def _hgin_body(x, nw_ref, sh_ref, sc_ref, w_ref, lb_ref, outs, n_lb):
    v_ref, kf_ref, gf_ref, kb_ref, gb_ref, q_ref, sg_ref = outs
    d = x.shape[-1]
    h = _rms(x, nw_ref[...]) * (1.0 + sc_ref[0]) + sh_ref[0]
    hb = h.astype(BF16)
    lbs = lb_ref[...]
    e = jnp.exp(lbs - jnp.max(lbs, axis=0, keepdims=True))
    lb = jnp.sum(e[:n_lb], axis=0) / jnp.sum(e, axis=0)

    v_ref[0] = _dot(hb, w_ref[:, 0:d]).astype(BF16)
    for j, (k_ref, g_ref) in enumerate(((kf_ref, gf_ref), (kb_ref, gb_ref))):
        raw = _dot(hb, w_ref[:, (1 + j) * d:(2 + j) * d])
        lbj = lb[j:j + 1]
        sig = jax.nn.sigmoid(raw)
        f = lbj + (1.0 - lbj) * sig
        k_ref[0] = ((1.0 - lbj) * (1.0 - sig)).astype(BF16)
        g_ref[0] = jnp.log(f)
    qr = _dot(hb, w_ref[:, 3 * d:4 * d])
    q_ref[0] = (qr * jax.nn.sigmoid(qr)).astype(BF16)
    gr = _dot(hb, w_ref[:, 4 * d:5 * d])
    sg_ref[0] = (gr * jax.nn.sigmoid(gr)).astype(BF16)


def _hgin_pos_kernel(x_ref, pos_ref, nw_ref, sh_ref, sc_ref, w_ref, lb_ref, xp_ref, *outs, n_lb):
    x = x_ref[0] + pos_ref[...]
    xp_ref[0] = x
    _hgin_body(x, nw_ref, sh_ref, sc_ref, w_ref, lb_ref, outs, n_lb)


def _hgin_kernel(x_ref, nw_ref, sh_ref, sc_ref, w_ref, lb_ref, *outs, n_lb):
    _hgin_body(x_ref[0], nw_ref, sh_ref, sc_ref, w_ref, lb_ref, outs, n_lb)


def _hg_in(x, pos, nw, sh, sc, w_bf, lb_raw, n_lb, tm):
    bsz, t, d = x.shape
    tm = min(tm, t)
    tok = pl.BlockSpec((1, tm, d), lambda b, i: (b, i, 0))
    vec = pl.BlockSpec((1, d), lambda b, i: (0, 0))
    bvec = pl.BlockSpec((1, 1, d), lambda b, i: (b, 0, 0))
    wspec = pl.BlockSpec(w_bf.shape, lambda b, i: (0, 0))
    lbspec = pl.BlockSpec(lb_raw.shape, lambda b, i: (0, 0, 0))
    gate_shapes = [jax.ShapeDtypeStruct((bsz, t, d), dt) for dt in (BF16, BF16, F32, BF16, F32, BF16, BF16)]
    if pos is not None:
        kern = functools.partial(_hgin_pos_kernel, n_lb=n_lb)
        in_specs = [tok, pl.BlockSpec((tm, d), lambda b, i: (i, 0)), vec, bvec, bvec, wspec, lbspec]
        args = (x, pos, nw, sh, sc, w_bf, lb_raw)
        out_shape = [jax.ShapeDtypeStruct((bsz, t, d), F32)] + gate_shapes
    else:
        kern = functools.partial(_hgin_kernel, n_lb=n_lb)
        in_specs = [tok, vec, bvec, bvec, wspec, lbspec]
        args = (x, nw, sh, sc, w_bf, lb_raw)
        out_shape = gate_shapes
    return pl.pallas_call(
        kern,
        grid=(bsz, t // tm),
        in_specs=in_specs,
        out_specs=[tok] * len(out_shape),
        out_shape=out_shape,
        compiler_params=_cparams(("arbitrary", "arbitrary")),
        name="hg_in",
    )(*args)


def _gla_consts():
    c = GLA_CHUNK
    t = jnp.arange(c)[:, None]
    s = jnp.arange(c)[None, :]
    tri = jnp.stack([(s <= t), (s >= t)]).astype(BF16)
    masks = []
    for rev in (False, True):
        lv = []
        for m in GLA_LEVELS:
            same = (t // (2 * m)) == (s // (2 * m))
            tq = ((t // m) % 2) == (0 if rev else 1)
            sk = ((s // m) % 2) == (1 if rev else 0)
            lv.append(same & tq & sk)
        lv.append(t == s)
        lv.append(((t // GLA_SHORT_BLOCK) == (s // GLA_SHORT_BLOCK)) & ((s >= t) if rev else (s <= t)))
        masks.append(jnp.stack(lv))
    return tri, jnp.stack(masks).astype(F32)


def _split3(x):
    p0 = x.astype(BF16)
    r1 = x - p0.astype(F32)
    p1 = r1.astype(BF16)
    p2 = (r1 - p1.astype(F32)).astype(BF16)
    return p0, p1, p2


def _level_ref(b_scr, hs, m, rev):
    c = GLA_CHUNK
    off = m if rev else m - 1

    def row8(i):
        return jnp.broadcast_to(b_scr[pl.ds(i, 1), hs], (SUBLANES, HEAD_DIM))

    pieces = []
    if m >= SUBLANES:
        for blk in range(c // (2 * m)):
            r8 = row8(blk * 2 * m + off)
            pieces.extend([r8] * (2 * m // SUBLANES))
    else:
        sub = lax.broadcasted_iota(I32, (SUBLANES, HEAD_DIM), 0) // (2 * m)
        for grp in range(c // SUBLANES):
            piece = row8(grp * SUBLANES + off)
            for cls in range(1, SUBLANES // (2 * m)):
                piece = jnp.where(sub == cls, row8(grp * SUBLANES + cls * 2 * m + off), piece)
            pieces.append(piece)
    return jnp.concatenate(pieces, axis=0)


def _block_ref(b_scr, hs, rev):
    c, blk = GLA_CHUNK, GLA_SHORT_BLOCK
    pieces = []
    for i in range(c // blk):
        r = (i + 1) * blk if rev else i * blk - 1
        if 0 <= r < c:
            pieces.append(jnp.broadcast_to(b_scr[pl.ds(r, 1), hs], (blk, HEAD_DIM)))
        else:
            pieces.append(jnp.zeros((blk, HEAD_DIM), F32))
    return jnp.concatenate(pieces, axis=0)


def _block_decay_bound(b_scr, rev):
    c, blk = GLA_CHUNK, GLA_SHORT_BLOCK
    worst = None
    for i in range(c // blk):
        inner = b_scr[pl.ds(i * blk if rev else (i + 1) * blk - 1, 1), :]
        r = (i + 1) * blk if rev else i * blk - 1
        span = jnp.abs(inner - b_scr[pl.ds(r, 1), :]) if 0 <= r < c else jnp.abs(inner)
        worst = span if worst is None else jnp.maximum(worst, span)
    return jnp.max(worst)


def _gla_direction(q_ref, k_ref, v_ref, g_ref, o_ref, tri_ref, mask_ref, st_scr, b_scr, d_idx, rev):
    c = GLA_CHUNK
    g = g_ref[0]
    tri = tri_ref[d_idx]
    p0, p1, p2 = _split3(g)
    b_scr[...] = _dot(tri, p0) + _dot(tri, p1) + _dot(tri, p2)
    n_lv = len(GLA_LEVELS)

    def head(h, short):
        hs = pl.ds(pl.multiple_of(h * HEAD_DIM, HEAD_DIM), HEAD_DIM)
        b = b_scr[:, hs]
        q = q_ref[0, :, hs]
        k = k_ref[0, :, hs]
        vb = v_ref[0, :, hs]
        bl = b_scr[pl.ds(0 if rev else c - 1, 1), hs]
        qt = q * jnp.exp(b).astype(BF16)
        kt = k * jnp.exp(bl - b).astype(BF16)
        st = st_scr[d_idx, h]

        def level(m):
            e = jnp.exp(-jnp.abs(b - _level_ref(b_scr, hs, m, rev))).astype(BF16)
            return _dot_nt(q * e, k * e)

        if short:
            u = _block_ref(b_scr, hs, rev) - b
            within = _dot_nt(q * jnp.exp(-u).astype(BF16), k * jnp.exp(u).astype(BF16))
            a = jnp.where(mask_ref[d_idx, n_lv + 1] > 0.0, within, 0.0)
            for li, m in enumerate(GLA_LEVELS):
                if m >= GLA_SHORT_BLOCK:
                    a = a + mask_ref[d_idx, li] * level(m)
        else:
            a = mask_ref[d_idx, n_lv] * _dot_nt(q, k)
            for li, m in enumerate(GLA_LEVELS):
                a = a + mask_ref[d_idx, li] * level(m)
        o_ref[0, :, hs] = _dot_nt(qt, st.astype(BF16)) + _dot(a.astype(BF16), vb)
        st_scr[d_idx, h] = st * jnp.exp(bl) + _dot_tn(vb, kt)

    bounded = _block_decay_bound(b_scr, rev) <= GLA_SHORT_MAX_EXPONENT

    @pl.when(bounded)
    def _():
        lax.fori_loop(0, N_HEADS, lambda h, cr: (head(h, True), cr)[1], 0, unroll=8)

    @pl.when(jnp.logical_not(bounded))
    def _():
        lax.fori_loop(0, N_HEADS, lambda h, cr: (head(h, False), cr)[1], 0, unroll=2)


def _gla_kernel(kf_ref, gf_ref, vf_ref, qf_ref, kb_ref, gb_ref, vb_ref, qb_ref, s0_ref, tri_ref, mask_ref,
                of_ref, ob_ref, sout_ref, st_scr, bf_scr, bb_scr):
    n = pl.program_id(1)

    @pl.when(n == 0)
    def _():
        st_scr[...] = s0_ref[0]

    _gla_direction(qf_ref, kf_ref, vf_ref, gf_ref, of_ref, tri_ref, mask_ref, st_scr, bf_scr, 0, False)
    _gla_direction(qb_ref, kb_ref, vb_ref, gb_ref, ob_ref, tri_ref, mask_ref, st_scr, bb_scr, 1, True)

    @pl.when(n == pl.num_programs(1) - 1)
    def _():
        sout_ref[0] = st_scr[...]


def _gla(kf, gf, kb, gb, v, q, s0, tri, masks):
    bsz, t, d = v.shape
    c = GLA_CHUNK
    n = t // c
    fwd = pl.BlockSpec((1, c, d), lambda b, i: (b, i, 0))
    bwd = pl.BlockSpec((1, c, d), lambda b, i: (b, n - 1 - i, 0))
    sspec = pl.BlockSpec((1,) + s0.shape[1:], lambda b, i: (b, 0, 0, 0, 0))
    return pl.pallas_call(
        _gla_kernel,
        grid=(bsz, n),
        in_specs=[fwd, fwd, fwd, fwd, bwd, bwd, bwd, bwd, sspec,
                  pl.BlockSpec(tri.shape, lambda b, i: (0, 0, 0)),
                  pl.BlockSpec(masks.shape, lambda b, i: (0, 0, 0, 0))],
        out_specs=[fwd, bwd, sspec],
        out_shape=[jax.ShapeDtypeStruct((bsz, t, d), F32), jax.ShapeDtypeStruct((bsz, t, d), F32),
                   jax.ShapeDtypeStruct(s0.shape, F32)],
        scratch_shapes=[pltpu.VMEM(s0.shape[1:], F32), pltpu.VMEM((c, d), F32), pltpu.VMEM((c, d), F32)],
        compiler_params=_cparams(("arbitrary", "arbitrary")),
        name="gla",
    )(kf, gf, v, q, kb, gb, v, q, s0, tri, masks)


def _store_tile_rows(ref2, x):
    n, d = x.shape
    nt = d // LANES
    for i in range(n // SUBLANES):
        for c in range(nt):
            r0 = (i * nt + c) * SUBLANES
            ref2[r0:r0 + SUBLANES, :] = x[i * SUBLANES:(i + 1) * SUBLANES, c * LANES:(c + 1) * LANES]


def _load_tile_rows(ref2, n, d):
    nt = d // LANES
    rows = []
    for i in range(n // SUBLANES):
        rows.append(jnp.concatenate(
            [ref2[(i * nt + c) * SUBLANES:(i * nt + c + 1) * SUBLANES, :] for c in range(nt)], axis=1))
    return jnp.concatenate(rows, axis=0)


def _tile_row_base(t, nt):
    return (t >> 3) * (nt * SUBLANES) + (t & (SUBLANES - 1))


def _tile_rows_spec(tm, d, index_map):
    return pl.BlockSpec((1, tm * (d // LANES), LANES), index_map)


def _post_mixer(y, xres, rows, nw1_ref, g1_ref, nw2_ref, sh2_ref, sc2_ref, rhi_ref, rlo_ref, x1_ref, h2_ref, lg_ref):
    d = y.shape[-1]
    nt = d // LANES
    x1 = xres + g1_ref[0] * _rms(y, nw1_ref[...])
    x1_ref[0, rows, :] = x1
    h2 = _rms(x1, nw2_ref[...]) * (1.0 + sc2_ref[0]) + sh2_ref[0]
    _store_tile_rows(h2_ref.at[0, pl.ds(rows.start * nt, (rows.stop - rows.start) * nt)], h2)
    h_hi = h2.astype(BF16)
    h_lo = (h2 - h_hi.astype(F32)).astype(BF16)
    lg_ref[0, rows, :] = _dot(h_hi, rhi_ref[...]) + (_dot(h_lo, rhi_ref[...]) + _dot(h_hi, rlo_ref[...]))


def _hgout_kernel(of_ref, ob_ref, sg_ref, gn_ref, w_ref, xp_ref, nw1_ref, g1_ref, nw2_ref, sh2_ref, sc2_ref,
                  rhi_ref, rlo_ref, x1_ref, h2_ref, lg_ref, z_scr):
    o = of_ref[0] + ob_ref[0]
    for h in range(N_HEADS):
        hs = slice(h * HEAD_DIM, (h + 1) * HEAD_DIM)
        oh = o[:, hs]
        ms = jnp.mean(oh * oh, axis=-1, keepdims=True)
        z = oh * lax.rsqrt(ms + EPS) * gn_ref[:, hs] * sg_ref[0, :, hs].astype(F32)
        z_scr[:, hs] = z.astype(BF16)
    y = _dot(z_scr[...], w_ref[...])
    _post_mixer(y, xp_ref[0], slice(0, y.shape[0]), nw1_ref, g1_ref, nw2_ref, sh2_ref, sc2_ref, rhi_ref, rlo_ref,
                x1_ref, h2_ref, lg_ref)


def _hg_out(o_f, o_b, sg, gnorm, w_out_bf, xp, nw1, g1, nw2, sh2, sc2, r_hi, r_lo, tm):
    bsz, t, d = xp.shape
    rspec = pl.BlockSpec((d, LANES), lambda b, i: (0, 0))
    tok = pl.BlockSpec((1, tm, d), lambda b, i: (b, i, 0))
    vec = pl.BlockSpec((1, d), lambda b, i: (0, 0))
    bvec = pl.BlockSpec((1, 1, d), lambda b, i: (b, 0, 0))
    return pl.pallas_call(
        _hgout_kernel,
        grid=(bsz, t // tm),
        in_specs=[tok, tok, tok, vec, pl.BlockSpec((d, d), lambda b, i: (0, 0)), tok, vec, bvec, vec, bvec, bvec,
                  rspec, rspec],
        out_specs=[tok, _tile_rows_spec(tm, d, lambda b, i: (b, i, 0)),
                   pl.BlockSpec((1, tm, LANES), lambda b, i: (b, i, 0))],
        out_shape=[jax.ShapeDtypeStruct((bsz, t, d), F32), jax.ShapeDtypeStruct((bsz, t * (d // LANES), LANES), F32),
                   jax.ShapeDtypeStruct((bsz, t, LANES), F32)],
        scratch_shapes=[pltpu.VMEM((tm, d), BF16)],
        compiler_params=_cparams(("arbitrary", "arbitrary")),
        name="hg_out",
    )(o_f, o_b, sg, gnorm, w_out_bf, xp, nw1, g1, nw2, sh2, sc2, r_hi, r_lo)


def _sgu_kernel(x1p_ref, moe_ref, nw3p_ref, g2p_ref, nw0_ref, sh1_ref, sc1_ref, win_ref, lnw_ref, lnb_ref,
                ws_ref, bs_ref, wout_ref, nw1_ref, g1_ref, nw2_ref, sh2_ref, sc2_ref, rhi_ref, rlo_ref,
                x1_ref, h2_ref, lg_ref):
    tm, d = x1p_ref.shape[1], x1p_ref.shape[2]
    w = lnw_ref.shape[-1]
    gd = w // SGU_GROUPS
    nt = d // LANES

    def gelu(z):
        return 0.5 * z * (1.0 + jnp.tanh(math.sqrt(2.0 / math.pi) * (z + 0.044715 * (z * z * z))))

    for ck in range(tm // SGU_CHUNK):
        rs = slice(ck * SGU_CHUNK, (ck + 1) * SGU_CHUNK)
        moe = _load_tile_rows(moe_ref.at[0, pl.ds(ck * SGU_CHUNK * nt, SGU_CHUNK * nt)], SGU_CHUNK, d)
        x = x1p_ref[0, rs, :] + g2p_ref[0] * _rms(moe, nw3p_ref[...])
        h = _rms(x, nw0_ref[...]) * (1.0 + sc1_ref[0]) + sh1_ref[0]
        hb = h.astype(BF16)
        u = gelu(_dot(hb, win_ref[:, 0:w]))
        v = gelu(_dot(hb, win_ref[:, w:2 * w]))
        mu = jnp.mean(v, axis=-1, keepdims=True)
        vc = v - mu
        vn = vc * lax.rsqrt(jnp.mean(vc * vc, axis=-1, keepdims=True) + EPS) * lnw_ref[...] + lnb_ref[...]
        vnb = vn.astype(BF16)
        gated = []
        for g in range(SGU_GROUPS):
            cs = slice(g * gd, (g + 1) * gd)
            mixed = _dot(ws_ref[g], vnb[:, cs]) + bs_ref[:, g:g + 1]
            gated.append((u[:, cs] * mixed).astype(BF16))
        y = _dot(jnp.concatenate(gated, axis=1), wout_ref[...])
        _post_mixer(y, x, rs, nw1_ref, g1_ref, nw2_ref, sh2_ref, sc2_ref, rhi_ref, rlo_ref, x1_ref, h2_ref, lg_ref)


def _sgu(x1p, moe, nw3p, g2p, nw0, sh1, sc1, w_in_bf, ln_w, ln_b, w_s_bf, b_s_t, w_out_bf,
         nw1, g1, nw2, sh2, sc2, r_hi, r_lo, tm):
    bsz, t, d = x1p.shape
    w = ln_w.shape[-1]
    tok = pl.BlockSpec((1, tm, d), lambda b, i: (b, i, 0))
    tiles = _tile_rows_spec(tm, d, lambda b, i: (b, i, 0))
    vec = pl.BlockSpec((1, d), lambda b, i: (0, 0))
    wvec = pl.BlockSpec((1, w), lambda b, i: (0, 0))
    bvec = pl.BlockSpec((1, 1, d), lambda b, i: (b, 0, 0))

    def full(a):
        return pl.BlockSpec(a.shape, lambda b, i: (0,) * a.ndim)

    return pl.pallas_call(
        _sgu_kernel,
        grid=(bsz, t // tm),
        in_specs=[tok, tiles, vec, bvec, vec, bvec, bvec, full(w_in_bf), wvec, wvec, full(w_s_bf), full(b_s_t),
                  full(w_out_bf), vec, bvec, vec, bvec, bvec, full(r_hi), full(r_lo)],
        out_specs=[tok, tiles, pl.BlockSpec((1, tm, LANES), lambda b, i: (b, i, 0))],
        out_shape=[jax.ShapeDtypeStruct((bsz, t, d), F32), jax.ShapeDtypeStruct((bsz, t * (d // LANES), LANES), F32),
                   jax.ShapeDtypeStruct((bsz, t, LANES), F32)],
        compiler_params=_cparams(("arbitrary", "arbitrary")),
        name="sgu",
    )(x1p, moe, nw3p, g2p, nw0, sh1, sc1, w_in_bf, ln_w, ln_b, w_s_bf, b_s_t, w_out_bf,
      nw1, g1, nw2, sh2, sc2, r_hi, r_lo)


def _token_prefix(mask, triu, slow):
    local = _dot(mask.astype(BF16), triu)
    rowtot = jnp.broadcast_to(local[:, LANES - 1:LANES], local.shape)
    prev = _dot(slow, rowtot.astype(BF16))
    return local, prev, rowtot


def _route_kernel(lg_ref, triu_ref, slow_ref, idx_ref, gate_ref, a_scr, thr_scr, rhs_scr, *, cap):
    ne, nr = a_scr.shape[0], a_scr.shape[1]
    for r in range(nr):
        a_scr[:, r, :] = lg_ref[0, r * LANES:(r + 1) * LANES, :].T[0:ne, :]
    lg = a_scr[...]
    ex = jnp.exp(lg - jnp.max(lg, axis=0, keepdims=True))
    a = ex / jnp.sum(ex, axis=0, keepdims=True)
    a_scr[...] = a
    capf = jnp.float32(cap)

    def count(m):
        return jnp.sum(jnp.sum(m.astype(F32), axis=1, keepdims=True), axis=2, keepdims=True)

    def unresolved(state):
        it, lo, hi = state
        mid = 0.5 * (lo + hi)
        return jnp.logical_and(it < BISECT_ITERS, jnp.max(((mid != lo) & (mid != hi)).astype(F32)) > 0.0)

    def bis(state):
        it, lo, hi = state
        mid = 0.5 * (lo + hi)
        ge = count(a >= mid) >= capf
        return it + 1, jnp.where(ge, mid, lo), jnp.where(ge, hi, mid)

    _, lo, _ = lax.while_loop(unresolved, bis, (jnp.int32(0), jnp.zeros((ne, 1, 1), F32), jnp.full((ne, 1, 1), 2.0, F32)))
    thr_scr[...] = jnp.broadcast_to(lo, thr_scr.shape)

    idx_ref[0] = jnp.zeros(idx_ref.shape[1:], I32)
    gate_ref[0] = jnp.zeros(gate_ref.shape[1:], F32)
    triu = triu_ref[...]
    slow = slow_ref[...]
    lane = lax.broadcasted_iota(I32, (LANES, LANES), 1)
    lane_f = lane.astype(F32)
    sub_f = lax.broadcasted_iota(I32, (LANES, LANES), 0).astype(F32)
    rowid = lax.broadcasted_iota(I32, (nr, LANES), 0).astype(F32)

    def per_expert(e, carry):
        ae = a_scr[e]
        v = thr_scr[e][0:1, :]
        gt = ae > v
        eq = ae == v
        need = capf - jnp.sum(jnp.sum(gt.astype(F32), axis=0, keepdims=True), axis=1, keepdims=True)
        eql, eqp, _ = _token_prefix(eq, triu, slow)
        sel = gt | (eq & ((eql + eqp - eq.astype(F32)) < need))
        local, prev, rowtot = _token_prefix(sel, triu, slow)
        rowcum = prev + rowtot
        prev_hi = jnp.floor(prev * (1.0 / LANES))
        a0, a1, a2 = _split3(ae)
        for k, piece in enumerate((local, prev_hi, prev - LANES * prev_hi, rowid, a0, a1, a2)):
            rhs_scr[:, k * LANES:(k + 1) * LANES] = piece.astype(BF16)

        for p in range(cap // LANES):
            base = float(p * LANES + 1)
            slot_row = base + lane_f[0:1, :]
            onehot_t = ((prev < slot_row) & (rowcum >= slot_row)).astype(BF16)
            g = _dot_tn(onehot_t, rhs_scr[...])
            g_local = g[:, 0:LANES]
            g_prev = LANES * g[:, LANES:2 * LANES] + g[:, 2 * LANES:3 * LANES]
            g_row = g[:, 3 * LANES:4 * LANES]
            g_a = g[:, 4 * LANES:5 * LANES] + g[:, 5 * LANES:6 * LANES] + g[:, 6 * LANES:7 * LANES]
            slot_col = base + sub_f
            lstar = jnp.sum(((g_local + g_prev) < slot_col).astype(F32), axis=1, keepdims=True)
            tok = LANES * g_row[:, 0:1] + lstar
            gat = jnp.sum(jnp.where(lane_f == lstar, g_a, 0.0), axis=1, keepdims=True)
            rs = slice(p * LANES, (p + 1) * LANES)
            idx_ref[0, rs, :] = jnp.where(lane == e, tok.astype(I32), idx_ref[0, rs, :])
            gate_ref[0, rs, :] = jnp.where(lane == e, gat, gate_ref[0, rs, :])
        return carry

    lax.fori_loop(0, ne, per_expert, 0)


def _route(logits, ne, cap):
    bsz, t, _ = logits.shape
    nr = t // LANES
    assert cap % LANES == 0 and nr % SUBLANES == 0
    triu = (jnp.arange(LANES)[:, None] <= jnp.arange(LANES)[None, :]).astype(BF16)
    slow = (jnp.arange(nr)[:, None] > jnp.arange(nr)[None, :]).astype(BF16)
    idx, gate = pl.pallas_call(
        functools.partial(_route_kernel, cap=cap),
        grid=(bsz,),
        in_specs=[pl.BlockSpec((1, t, LANES), lambda b: (b, 0, 0)),
                  pl.BlockSpec((LANES, LANES), lambda b: (0, 0)), pl.BlockSpec((nr, nr), lambda b: (0, 0))],
        out_specs=[pl.BlockSpec((1, cap, LANES), lambda b: (b, 0, 0))] * 2,
        out_shape=[jax.ShapeDtypeStruct((bsz, cap, LANES), I32), jax.ShapeDtypeStruct((bsz, cap, LANES), F32)],
        scratch_shapes=[pltpu.VMEM((ne, nr, LANES), F32), pltpu.VMEM((ne, SUBLANES, LANES), F32),
                        pltpu.VMEM((nr, 7 * LANES), BF16)],
        compiler_params=_cparams(("arbitrary",)),
        name="route",
    )(logits, triu, slow)
    return idx, gate


ROW_UNROLL = 8


def _gather_kernel(src_ref, h_hbm, o_ref, h_scr, row_scr, sem, *, cap, ne, nt):
    b, e = pl.program_id(0), pl.program_id(1)

    @pl.when(e == 0)
    def _():
        cp = pltpu.make_async_copy(h_hbm.at[b], h_scr, sem.at[0])
        cp.start()
        cp.wait()

    base = (b * ne + e) * cap
    group = nt * SUBLANES

    def body(jj, carry):
        for u in range(SUBLANES):
            row_scr[pl.ds(jj * group + u, nt, stride=SUBLANES), :] = \
                h_scr[pl.ds(src_ref[base + jj * SUBLANES + u], nt, stride=SUBLANES), :]
        return carry

    lax.fori_loop(0, cap // SUBLANES, body, 0)

    def emit(g, carry):
        rows = [jnp.concatenate([row_scr[pl.ds((g * 2 + i) * group + c * SUBLANES, SUBLANES), :] for c in range(nt)],
                                axis=1) for i in range(2)]
        o_ref[0, 0, pl.ds(g * 2 * SUBLANES, 2 * SUBLANES), :] = jnp.concatenate(rows, axis=0).astype(BF16)
        return carry

    lax.fori_loop(0, cap // (2 * SUBLANES), emit, 0)


def _gather(src_flat, h2_rows, ne, cap, t):
    bsz, rows, _ = h2_rows.shape
    nt = rows // t
    return pl.pallas_call(
        functools.partial(_gather_kernel, cap=cap, ne=ne, nt=nt),
        grid_spec=pltpu.PrefetchScalarGridSpec(
            num_scalar_prefetch=1,
            grid=(bsz, ne),
            in_specs=[pl.BlockSpec(memory_space=pl.ANY)],
            out_specs=pl.BlockSpec((1, 1, cap, nt * LANES), lambda b, e, src: (b, e, 0, 0)),
            scratch_shapes=[pltpu.VMEM((rows, LANES), F32), pltpu.VMEM((cap * nt, LANES), F32),
                            pltpu.SemaphoreType.DMA((1,))],
        ),
        out_shape=jax.ShapeDtypeStruct((bsz, ne, cap, nt * LANES), BF16),
        compiler_params=_cparams(("arbitrary", "arbitrary")),
        name="gather",
    )(src_flat, h2_rows)


def _ffn_kernel(xs_ref, gate_ref, wg_ref, wu_ref, wd_ref, y_ref):
    e, f = pl.program_id(0), pl.program_id(1)
    cap, d = xs_ref.shape[2], xs_ref.shape[3]
    nt = d // LANES

    @pl.when(f == 0)
    def _():
        y_ref[...] = jnp.zeros(y_ref.shape, F32)

    wg = wg_ref[0, 0].astype(BF16)
    wu = wu_ref[0, 0].astype(BF16)
    wd = wd_ref[0, 0].astype(BF16)
    lane = lax.broadcasted_iota(I32, (cap, LANES), 1)
    for b in range(xs_ref.shape[0]):
        xb = xs_ref[b, 0]
        g = _dot(xb, wg)
        u = _dot(xb, wu)
        gate = jnp.sum(jnp.where(lane == e, gate_ref[b], 0.0), axis=1, keepdims=True)
        hid = (g * jax.nn.sigmoid(g) * u * gate).astype(BF16)
        part = _dot(hid, wd)
        for i in range(cap // SUBLANES):
            for c in range(nt):
                r0 = (i * nt + c) * SUBLANES
                y_ref[b, 0, r0:r0 + SUBLANES, :] += part[i * SUBLANES:(i + 1) * SUBLANES, c * LANES:(c + 1) * LANES]


def _ffn(xs, gate_cols, w_gate, w_up, w_down, layer, tf):
    bsz, ne, cap, d = xs.shape
    ff = w_gate.shape[-1]
    tf = min(tf, ff)
    nt = d // LANES
    return pl.pallas_call(
        _ffn_kernel,
        grid=(ne, ff // tf),
        in_specs=[pl.BlockSpec((bsz, 1, cap, d), lambda e, f: (0, e, 0, 0)),
                  pl.BlockSpec((bsz, cap, LANES), lambda e, f: (0, 0, 0)),
                  pl.BlockSpec((1, 1, d, tf), lambda e, f: (layer, e, 0, f)),
                  pl.BlockSpec((1, 1, d, tf), lambda e, f: (layer, e, 0, f)),
                  pl.BlockSpec((1, 1, tf, d), lambda e, f: (layer, e, f, 0))],
        out_specs=pl.BlockSpec((bsz, 1, cap * nt, LANES), lambda e, f: (0, e, 0, 0)),
        out_shape=jax.ShapeDtypeStruct((bsz, ne, cap * nt, LANES), F32),
        compiler_params=_cparams(("arbitrary", "arbitrary")),
        name="ffn",
    )(xs, gate_cols, w_gate, w_up, w_down)


SCATTER_SPLIT = 8


def _scatter_kernel(dst_ref, cut_ref, y_ref, zero_hbm, o_hbm, *scr, cap, ne, nt):
    accs, sem = scr[:SCATTER_SPLIT], scr[SCATTER_SPLIT]
    b, e = pl.program_id(0), pl.program_id(1)

    @pl.when(e == 0)
    def _():
        fills = [pltpu.make_async_copy(zero_hbm, acc, sem.at[k]) for k, acc in enumerate(accs)]
        for cp in fills:
            cp.start()
        for cp in fills:
            cp.wait()

    base = (b * ne + e) * cap
    cbase = (b * ne + e) * (SCATTER_SPLIT + 1)
    starts = [cut_ref[cbase + k] for k in range(SCATTER_SPLIT)]
    counts = [cut_ref[cbase + k + 1] - starts[k] for k in range(SCATTER_SPLIT)]
    shortest = functools.reduce(jnp.minimum, counts)
    longest = functools.reduce(jnp.maximum, counts)

    def update(acc, j, dst, scale=None):
        src = _tile_row_base(j, nt)
        row = y_ref[0, 0, pl.ds(src, nt, stride=SUBLANES), :]
        acc[pl.ds(dst, nt, stride=SUBLANES), :] += row if scale is None else scale * row

    def common(i, carry):
        for k, acc in enumerate(accs):
            j = starts[k] + i
            update(acc, j, dst_ref[base + j])
        return carry

    def tail(i, carry):
        for k, acc in enumerate(accs):
            valid = i < counts[k]
            j = jnp.minimum(starts[k] + i, cap - 1)
            update(acc, j, jnp.where(valid, dst_ref[base + j], 0), jnp.where(valid, 1.0, 0.0))
        return carry

    lax.fori_loop(0, shortest, common, 0)
    lax.fori_loop(shortest, longest, tail, 0)

    @pl.when(e == ne - 1)
    def _():
        part = accs[0].shape[0]
        copies = [pltpu.make_async_copy(acc, o_hbm.at[b, pl.ds(k * part, part)], sem.at[k])
                  for k, acc in enumerate(accs)]
        for cp in copies:
            cp.start()
        for cp in copies:
            cp.wait()


def _scatter(dst_flat, cuts_flat, y_rows, t, nt):
    bsz, ne, rows, _ = y_rows.shape
    cap = rows // nt
    part_rows = (t // SCATTER_SPLIT) * nt
    zeros = jnp.zeros((part_rows, LANES), F32)
    return pl.pallas_call(
        functools.partial(_scatter_kernel, cap=cap, ne=ne, nt=nt),
        grid_spec=pltpu.PrefetchScalarGridSpec(
            num_scalar_prefetch=2,
            grid=(bsz, ne),
            in_specs=[pl.BlockSpec((1, 1, rows, LANES), lambda b, e, i, c: (b, e, 0, 0)),
                      pl.BlockSpec(memory_space=pl.ANY)],
            out_specs=pl.BlockSpec(memory_space=pl.ANY),
            scratch_shapes=[pltpu.VMEM((part_rows, LANES), F32)] * SCATTER_SPLIT
            + [pltpu.SemaphoreType.DMA((SCATTER_SPLIT,))],
        ),
        out_shape=jax.ShapeDtypeStruct((bsz, t * nt, LANES), F32),
        compiler_params=_cparams(("arbitrary", "arbitrary")),
        name="scatter",
    )(dst_flat, cuts_flat, y_rows, zeros)


def _moe(h2_rows, logits, ne, w_gate, w_up, w_down, layer, tf):
    bsz, t, _ = logits.shape
    nt = h2_rows.shape[1] // t
    cap = EC_CAPACITY_FACTOR * t // ne
    tpart = t // SCATTER_SPLIT
    idx_cols, gate_cols = _route(logits, ne, cap)
    idx = jnp.swapaxes(idx_cols[:, :, :ne], 1, 2)
    xs = _gather(_tile_row_base(idx, nt).reshape(-1), h2_rows, ne, cap, t)
    y = _ffn(xs, gate_cols, w_gate, w_up, w_down, layer, tf)
    edges = jnp.arange(SCATTER_SPLIT + 1, dtype=I32) * tpart
    cuts = jnp.sum(idx[..., None] < edges, axis=2).astype(I32)
    return _scatter(_tile_row_base(idx % tpart, nt).reshape(-1), cuts.reshape(-1), y, t, nt)


def _final_kernel(x_ref, moe_ref, nw_ref, g_ref, o_ref):
    x = x_ref[0]
    o_ref[0] = x + g_ref[0] * _rms(_load_tile_rows(moe_ref.at[0], x.shape[0], x.shape[1]), nw_ref[...])


def _final(x1, moe, nw3, g2, tm):
    bsz, t, d = x1.shape
    tok = pl.BlockSpec((1, tm, d), lambda b, i: (b, i, 0))
    return pl.pallas_call(
        _final_kernel,
        grid=(bsz, t // tm),
        in_specs=[tok, _tile_rows_spec(tm, d, lambda b, i: (b, i, 0)), pl.BlockSpec((1, d), lambda b, i: (0, 0)),
                  pl.BlockSpec((1, 1, d), lambda b, i: (b, 0, 0))],
        out_specs=tok,
        out_shape=jax.ShapeDtypeStruct((bsz, t, d), F32),
        compiler_params=_cparams(("arbitrary", "arbitrary")),
        name="final",
    )(x1, moe, nw3, g2)


def _sincos_2d(rows, dim):
    quarter = dim // 4
    half = dim // 2
    freqs = jnp.exp(-math.log(10000.0) * jnp.arange(quarter, dtype=F32) / quarter)

    def emb1d(n):
        ang = jnp.arange(n, dtype=F32)[:, None] * freqs[None, :]
        return jnp.concatenate([jnp.sin(ang), jnp.cos(ang)], axis=-1)

    er = emb1d(rows)
    ec = emb1d(GRID_W)
    pos = jnp.concatenate([jnp.broadcast_to(er[:, None, :], (rows, GRID_W, half)),
                           jnp.broadcast_to(ec[None, :, :], (rows, GRID_W, half))], axis=-1)
    return pos.reshape(rows * GRID_W, dim)


TOKEN_TILE = 256
FFN_TILE = 512


def kernel(x, c, ctx, c_ctx, w_ada, b_ada, norm_w, hg_w_in, hg_lb, hg_gnorm, hg_w_out, sg_w_in, sg_ln_w, sg_ln_b,
           sg_w_s, sg_b_s, sg_w_out, moe_router, moe_w_gate, moe_w_up, moe_w_down):
    bsz, t, d = x.shape
    depth = w_ada.shape[0]
    assert depth == 2 and d == N_HEADS * HEAD_DIM and t % GLA_CHUNK == 0 and ctx.shape[1] % GLA_CHUNK == 0
    tm = min(TOKEN_TILE, t)

    cvec = jnp.zeros((SUBLANES, d), F32).at[:bsz].set(c).at[bsz].set(c_ctx)
    mod = _ada(cvec, w_ada, b_ada)

    def mods(layer, rows):
        m = mod[layer, rows].reshape(-1, N_ADA, 1, d)
        return [m[:, k] for k in range(N_ADA)]

    nw = norm_w.reshape(depth, 4, 1, d)
    ne = moe_router.shape[-1]
    router = jnp.zeros((depth, d, LANES), F32).at[:, :, :ne].set(moe_router)
    r_hi = router.astype(BF16)
    r_lo = (router - r_hi.astype(F32)).astype(BF16)

    sh1, sc1, g1, sh2, sc2, g2 = mods(0, slice(0, bsz))
    sh1c, sc1c = [jnp.broadcast_to(m, (bsz, 1, d)) for m in mods(0, slice(bsz, bsz + 1))[:2]]
    w_in_bf = hg_w_in[0].astype(BF16)
    pos = _sincos_2d(t // GRID_W, d)
    tri, masks = _gla_consts()

    cv, ckf, cgf, ckb, cgb, cq, _ = _hg_in(ctx, None, nw[0, 0], sh1c, sc1c, w_in_bf, hg_lb, 1, tm)
    s_zero = jnp.zeros((bsz, 2, N_HEADS, HEAD_DIM, HEAD_DIM), F32)
    _, _, s_ctx = _gla(ckf, cgf, ckb, cgb, cv, cq, s_zero, tri, masks)

    xp, v, kf, gf, kb, gb, q, sg = _hg_in(x, pos, nw[0, 0], sh1, sc1, w_in_bf, hg_lb, 1, tm)
    o_f, o_b, _ = _gla(kf, gf, kb, gb, v, q, s_ctx, tri, masks)
    x1, h2, lg = _hg_out(o_f, o_b, sg, hg_gnorm[0:1], hg_w_out[0].astype(BF16), xp, nw[0, 1], g1, nw[0, 2],
                         sh2, sc2, r_hi[0], r_lo[0], tm)
    moe = _moe(h2, lg, ne, moe_w_gate, moe_w_up, moe_w_down, 0, FFN_TILE)

    sh1b, sc1b, g1b, sh2b, sc2b, g2b = mods(1, slice(0, bsz))
    x1b, h2b, lgb = _sgu(x1, moe, nw[0, 3], g2, nw[1, 0], sh1b, sc1b, sg_w_in[0].astype(BF16),
                         sg_ln_w[0:1], sg_ln_b[0:1], sg_w_s[0].astype(BF16), sg_b_s[0].T, sg_w_out[0].astype(BF16),
                         nw[1, 1], g1b, nw[1, 2], sh2b, sc2b, r_hi[1], r_lo[1], tm)
    moe_b = _moe(h2b, lgb, ne, moe_w_gate, moe_w_up, moe_w_down, 1, FFN_TILE)
    return _final(x1b, moe_b, nw[1, 3], g2b, tm)
```

```python
import functools
import math

import jax
import jax.numpy as jnp
from jax import lax
from jax.experimental import pallas as pl
from jax.experimental.pallas import tpu as pltpu

F32 = jnp.float32
BF16 = jnp.bfloat16
I32 = jnp.int32
HIGHEST = lax.Precision.HIGHEST

EPS = 1e-6
GRID_W = 64
N_ADA = 6
N_HEADS = 8
HEAD_DIM = 128
N_EXPERTS = 16
EC_CAPACITY_FACTOR = 2
SGU_CHUNK = 128
SGU_GROUPS = 8

LANES = 128
SUBLANES = 8
GLA_CHUNK = 128
GLA_LEVELS = (64, 32, 16, 8, 4, 2, 1)
GLA_SHORT_BLOCK = 32
GLA_SHORT_MAX_EXPONENT = 60.0
BISECT_ITERS = 160
VMEM_LIMIT = 52 * 1024 * 1024


def _cparams(sem):
    return pltpu.CompilerParams(dimension_semantics=sem, vmem_limit_bytes=VMEM_LIMIT)


def _rms(x, w):
    ms = jnp.mean(x * x, axis=-1, keepdims=True)
    return x * lax.rsqrt(ms + EPS) * w


def _dot(a, b):
    return jnp.dot(a, b, preferred_element_type=F32)


def _dot_nt(a, b):
    return lax.dot_general(a, b, (((1,), (1,)), ((), ())), preferred_element_type=F32)


def _dot_tn(a, b):
    return lax.dot_general(a, b, (((0,), (0,)), ((), ())), preferred_element_type=F32)


def _ada_kernel(c_ref, w_ref, b_ref, o_ref):
    c = c_ref[...]
    s = c * jax.nn.sigmoid(c)
    o_ref[0] = jnp.dot(s, w_ref[0], precision=HIGHEST, preferred_element_type=F32) + b_ref[0]


def _ada(cvec, w_ada, b_ada):
    depth, d, nd = w_ada.shape
    rows = cvec.shape[0]
    return pl.pallas_call(
        _ada_kernel,
        grid=(depth, nd // d),
        in_specs=[pl.BlockSpec((rows, d), lambda l, n: (0, 0)),
                  pl.BlockSpec((1, d, d), lambda l, n: (l, 0, n)),
                  pl.BlockSpec((1, 1, d), lambda l, n: (l, 0, n))],
        out_specs=pl.BlockSpec((1, rows, d), lambda l, n: (l, 0, n)),
        out_shape=jax.ShapeDtypeStruct((depth, rows, nd), F32),
        compiler_params=_cparams(("arbitrary", "arbitrary")),
        name="ada",
    )(cvec, w_ada, b_ada.reshape(depth, 1, nd))


def _hgin_body(x, nw_ref, sh_ref, sc_ref, w_ref, lb_ref, outs, n_lb):
    v_ref, kf_ref, gf_ref, kb_ref, gb_ref, q_ref, sg_ref = outs
    d = x.shape[-1]
    h = _rms(x, nw_ref[...]) * (1.0 + sc_ref[0]) + sh_ref[0]
    hb = h.astype(BF16)
    lbs = lb_ref[...]
    e = jnp.exp(lbs - jnp.max(lbs, axis=0, keepdims=True))
    lb = jnp.sum(e[:n_lb], axis=0) / jnp.sum(e, axis=0)

    v_ref[0] = _dot(hb, w_ref[:, 0:d]).astype(BF16)
    for j, (k_ref, g_ref) in enumerate(((kf_ref, gf_ref), (kb_ref, gb_ref))):
        raw = _dot(hb, w_ref[:, (1 + j) * d:(2 + j) * d])
        lbj = lb[j:j + 1]
        sig = jax.nn.sigmoid(raw)
        f = lbj + (1.0 - lbj) * sig
        k_ref[0] = ((1.0 - lbj) * (1.0 - sig)).astype(BF16)
        g_ref[0] = jnp.log(f)
    qr = _dot(hb, w_ref[:, 3 * d:4 * d])
    q_ref[0] = (qr * jax.nn.sigmoid(qr)).astype(BF16)
    gr = _dot(hb, w_ref[:, 4 * d:5 * d])
    sg_ref[0] = (gr * jax.nn.sigmoid(gr)).astype(BF16)


def _hgin_pos_kernel(x_ref, er_ref, ec_ref, nw_ref, sh_ref, sc_ref, w_ref, lb_ref, xp_ref, *outs, n_lb):
    er, ec = er_ref[0], ec_ref[...]
    pos = jnp.concatenate([jnp.concatenate([jnp.broadcast_to(er[r:r + 1], ec.shape), ec], axis=1)
                           for r in range(er.shape[0])], axis=0)
    x = x_ref[0] + pos
    xp_ref[0] = x
    _hgin_body(x, nw_ref, sh_ref, sc_ref, w_ref, lb_ref, outs, n_lb)


def _hgin_kernel(x_ref, nw_ref, sh_ref, sc_ref, w_ref, lb_ref, *outs, n_lb):
    _hgin_body(x_ref[0], nw_ref, sh_ref, sc_ref, w_ref, lb_ref, outs, n_lb)


def _hg_in(x, pos, nw, sh, sc, w_bf, lb_raw, n_lb, tm):
    bsz, t, d = x.shape
    tm = min(tm, t)
    tok = pl.BlockSpec((1, tm, d), lambda b, i: (b, i, 0))
    vec = pl.BlockSpec((1, d), lambda b, i: (0, 0))
    bvec = pl.BlockSpec((1, 1, d), lambda b, i: (b, 0, 0))
    wspec = pl.BlockSpec(w_bf.shape, lambda b, i: (0, 0))
    lbspec = pl.BlockSpec(lb_raw.shape, lambda b, i: (0, 0, 0))
    gate_shapes = [jax.ShapeDtypeStruct((bsz, t, d), dt) for dt in (BF16, BF16, F32, BF16, F32, BF16, BF16)]
    if pos is not None:
        kern = functools.partial(_hgin_pos_kernel, n_lb=n_lb)
        er, ec = pos
        assert tm % GRID_W == 0
        rpt = tm // GRID_W
        in_specs = [tok, pl.BlockSpec((1, rpt, d // 2), lambda b, i: (i, 0, 0)),
                    pl.BlockSpec(ec.shape, lambda b, i: (0, 0)), vec, bvec, bvec, wspec, lbspec]
        args = (x, er.reshape(-1, rpt, d // 2), ec, nw, sh, sc, w_bf, lb_raw)
        out_shape = [jax.ShapeDtypeStruct((bsz, t, d), F32)] + gate_shapes
    else:
        kern = functools.partial(_hgin_kernel, n_lb=n_lb)
        in_specs = [tok, vec, bvec, bvec, wspec, lbspec]
        args = (x, nw, sh, sc, w_bf, lb_raw)
        out_shape = gate_shapes
    return pl.pallas_call(
        kern,
        grid=(bsz, t // tm),
        in_specs=in_specs,
        out_specs=[tok] * len(out_shape),
        out_shape=out_shape,
        compiler_params=_cparams(("arbitrary", "arbitrary")),
        name="hg_in",
    )(*args)


def _gla_consts():
    c = GLA_CHUNK
    t = jnp.arange(c)[:, None]
    s = jnp.arange(c)[None, :]
    tri = jnp.stack([(s <= t), (s >= t)]).astype(BF16)
    masks = []
    for rev in (False, True):
        lv = []
        for m in GLA_LEVELS:
            same = (t // (2 * m)) == (s // (2 * m))
            tq = ((t // m) % 2) == (0 if rev else 1)
            sk = ((s // m) % 2) == (1 if rev else 0)
            lv.append(same & tq & sk)
        lv.append(t == s)
        lv.append(((t // GLA_SHORT_BLOCK) == (s // GLA_SHORT_BLOCK)) & ((s >= t) if rev else (s <= t)))
        masks.append(jnp.stack(lv))
    return tri, jnp.stack(masks).astype(F32)


def _split3(x):
    p0 = x.astype(BF16)
    r1 = x - p0.astype(F32)
    p1 = r1.astype(BF16)
    p2 = (r1 - p1.astype(F32)).astype(BF16)
    return p0, p1, p2


def _level_ref(b_scr, hs, m, rev):
    c = GLA_CHUNK
    off = m if rev else m - 1

    def row8(i):
        return jnp.broadcast_to(b_scr[pl.ds(i, 1), hs], (SUBLANES, HEAD_DIM))

    pieces = []
    if m >= SUBLANES:
        for blk in range(c // (2 * m)):
            r8 = row8(blk * 2 * m + off)
            pieces.extend([r8] * (2 * m // SUBLANES))
    else:
        sub = lax.broadcasted_iota(I32, (SUBLANES, HEAD_DIM), 0) // (2 * m)
        for grp in range(c // SUBLANES):
            piece = row8(grp * SUBLANES + off)
            for cls in range(1, SUBLANES // (2 * m)):
                piece = jnp.where(sub == cls, row8(grp * SUBLANES + cls * 2 * m + off), piece)
            pieces.append(piece)
    return jnp.concatenate(pieces, axis=0)


def _block_decay_bound(b_scr, rev):
    c, blk = GLA_CHUNK, GLA_SHORT_BLOCK
    worst = None
    for i in range(c // blk):
        inner = b_scr[pl.ds(i * blk if rev else (i + 1) * blk - 1, 1), :]
        r = (i + 1) * blk if rev else i * blk - 1
        span = jnp.abs(inner - b_scr[pl.ds(r, 1), :]) if 0 <= r < c else jnp.abs(inner)
        worst = span if worst is None else jnp.maximum(worst, span)
    return jnp.max(worst)


def _gla_head(q_ref, k_ref, v_ref, o_ref, mask_ref, st_scr, b_scr, d_idx, rev, h, short):
    c = GLA_CHUNK
    n_lv = len(GLA_LEVELS)
    hs = pl.ds(pl.multiple_of(h * HEAD_DIM, HEAD_DIM), HEAD_DIM)
    b = b_scr[:, hs]
    q = q_ref[0, :, hs]
    k = k_ref[0, :, hs]
    vb = v_ref[0, :, hs]
    bl = b_scr[pl.ds(0 if rev else c - 1, 1), hs]
    st = st_scr[d_idx, h]

    if short:
        blk = GLA_SHORT_BLOCK
        nb = c // blk
        own = []
        for i in range(nb):
            r = (i + 1) * blk if rev else i * blk - 1
            own.append(b_scr[pl.ds(r, 1), hs] if 0 <= r < c else jnp.zeros((1, HEAD_DIM), F32))

        def blockwise(rows):
            return jnp.concatenate([jnp.broadcast_to(r, (blk, HEAD_DIM)) for r in rows], axis=0)

        u = blockwise(own) - b
        qh = q * jnp.exp(-u).astype(BF16)
        kh = k * jnp.exp(u).astype(BF16)
        qt = qh * blockwise([jnp.exp(r) for r in own]).astype(BF16)
        kt = kh * blockwise([jnp.exp(bl - r) for r in own]).astype(BF16)
        a = jnp.where(mask_ref[d_idx, n_lv + 1] > 0.0, _dot_nt(qh, kh), 0.0)
        zero = jnp.zeros((1, HEAD_DIM), F32)
        for li, m in enumerate(GLA_LEVELS):
            if m < blk:
                continue
            qf, kf = [], []
            for i in range(nb):
                lvl = b_scr[pl.ds((i * blk) // (2 * m) * 2 * m + (m if rev else m - 1), 1), hs]
                is_query = ((i * blk) // m) % 2 == (0 if rev else 1)
                qf.append(jnp.exp(own[i] - lvl) if is_query else zero)
                kf.append(zero if is_query else jnp.exp(lvl - own[i]))
            a = a + mask_ref[d_idx, li] * _dot_nt(qh * blockwise(qf).astype(BF16), kh * blockwise(kf).astype(BF16))
    else:
        qt = q * jnp.exp(b).astype(BF16)
        kt = k * jnp.exp(bl - b).astype(BF16)

        def level(m):
            e = jnp.exp(-jnp.abs(b - _level_ref(b_scr, hs, m, rev))).astype(BF16)
            return _dot_nt(q * e, k * e)

        a = mask_ref[d_idx, n_lv] * _dot_nt(q, k)
        for li, m in enumerate(GLA_LEVELS):
            a = a + mask_ref[d_idx, li] * level(m)
    o_ref[0, :, hs] = _dot_nt(qt, st.astype(BF16)) + _dot(a.astype(BF16), vb)
    st_scr[d_idx, h] = st * jnp.exp(bl) + _dot_tn(vb, kt)


def _gla_kernel(kf_ref, gf_ref, vf_ref, qf_ref, kb_ref, gb_ref, vb_ref, qb_ref, s0_ref, tri_ref, mask_ref,
                of_ref, ob_ref, sout_ref, st_scr, bf_scr, bb_scr):
    n = pl.program_id(1)

    @pl.when(n == 0)
    def _():
        st_scr[...] = s0_ref[0]

    dirs = ((qf_ref, kf_ref, vf_ref, gf_ref, of_ref, bf_scr, 0, False),
            (qb_ref, kb_ref, vb_ref, gb_ref, ob_ref, bb_scr, 1, True))
    bounded = []
    for q_ref, k_ref, v_ref, g_ref, o_ref, b_scr, d_idx, rev in dirs:
        p0, p1, p2 = _split3(g_ref[0])
        tri = tri_ref[d_idx]
        b_scr[...] = _dot(tri, p0) + _dot(tri, p1) + _dot(tri, p2)
        bounded.append(_block_decay_bound(b_scr, rev) <= GLA_SHORT_MAX_EXPONENT)

    def heads(which, short, unroll):
        def body(h, carry):
            for q_ref, k_ref, v_ref, _, o_ref, b_scr, d_idx, rev in which:
                _gla_head(q_ref, k_ref, v_ref, o_ref, mask_ref, st_scr, b_scr, d_idx, rev, h, short)
            return carry
        lax.fori_loop(0, N_HEADS, body, 0, unroll=unroll)

    both = jnp.logical_and(bounded[0], bounded[1])

    @pl.when(both)
    def _():
        heads(dirs, True, 8)

    for d, ok in zip(dirs, bounded):
        @pl.when(jnp.logical_and(jnp.logical_not(both), ok))
        def _():
            heads((d,), True, 2)

        @pl.when(jnp.logical_not(ok))
        def _():
            heads((d,), False, 2)

    @pl.when(n == pl.num_programs(1) - 1)
    def _():
        sout_ref[0] = st_scr[...]


def _gla(kf, gf, kb, gb, v, q, s0, tri, masks):
    bsz, t, d = v.shape
    c = GLA_CHUNK
    n = t // c
    fwd = pl.BlockSpec((1, c, d), lambda b, i: (b, i, 0))
    bwd = pl.BlockSpec((1, c, d), lambda b, i: (b, n - 1 - i, 0))
    sspec = pl.BlockSpec((1,) + s0.shape[1:], lambda b, i: (b, 0, 0, 0, 0))
    return pl.pallas_call(
        _gla_kernel,
        grid=(bsz, n),
        in_specs=[fwd, fwd, fwd, fwd, bwd, bwd, bwd, bwd, sspec,
                  pl.BlockSpec(tri.shape, lambda b, i: (0, 0, 0)),
                  pl.BlockSpec(masks.shape, lambda b, i: (0, 0, 0, 0))],
        out_specs=[fwd, bwd, sspec],
        out_shape=[jax.ShapeDtypeStruct((bsz, t, d), F32), jax.ShapeDtypeStruct((bsz, t, d), F32),
                   jax.ShapeDtypeStruct(s0.shape, F32)],
        scratch_shapes=[pltpu.VMEM(s0.shape[1:], F32), pltpu.VMEM((c, d), F32), pltpu.VMEM((c, d), F32)],
        compiler_params=_cparams(("arbitrary", "arbitrary")),
        name="gla",
    )(kf, gf, v, q, kb, gb, v, q, s0, tri, masks)


def _store_tile_rows(ref2, x):
    n, d = x.shape
    nt = d // LANES
    for i in range(n // SUBLANES):
        for c in range(nt):
            r0 = (i * nt + c) * SUBLANES
            ref2[r0:r0 + SUBLANES, :] = x[i * SUBLANES:(i + 1) * SUBLANES, c * LANES:(c + 1) * LANES]


def _load_tile_rows(ref2, n, d):
    nt = d // LANES
    rows = []
    for i in range(n // SUBLANES):
        rows.append(jnp.concatenate(
            [ref2[(i * nt + c) * SUBLANES:(i * nt + c + 1) * SUBLANES, :] for c in range(nt)], axis=1))
    return jnp.concatenate(rows, axis=0)


def _tile_row_base(t, nt):
    return (t >> 3) * (nt * SUBLANES) + (t & (SUBLANES - 1))


def _tile_rows_spec(tm, d, index_map):
    return pl.BlockSpec((1, tm * (d // LANES), LANES), index_map)


def _post_mixer(y, xres, rows, nw1_ref, g1_ref, nw2_ref, sh2_ref, sc2_ref, rhi_ref, rlo_ref, x1_ref, h2_ref, lg_ref):
    d = y.shape[-1]
    nt = d // LANES
    x1 = xres + g1_ref[0] * _rms(y, nw1_ref[...])
    x1_ref[0, rows, :] = x1
    h2 = _rms(x1, nw2_ref[...]) * (1.0 + sc2_ref[0]) + sh2_ref[0]
    _store_tile_rows(h2_ref.at[0, pl.ds(rows.start * nt, (rows.stop - rows.start) * nt)], h2)
    h_hi = h2.astype(BF16)
    h_lo = (h2 - h_hi.astype(F32)).astype(BF16)
    lg_ref[0, rows, :] = _dot(h_hi, rhi_ref[...]) + (_dot(h_lo, rhi_ref[...]) + _dot(h_hi, rlo_ref[...]))


def _hgout_kernel(of_ref, ob_ref, sg_ref, gn_ref, w_ref, xp_ref, nw1_ref, g1_ref, nw2_ref, sh2_ref, sc2_ref,
                  rhi_ref, rlo_ref, x1_ref, h2_ref, lg_ref, z_scr):
    o = of_ref[0] + ob_ref[0]
    for h in range(N_HEADS):
        hs = slice(h * HEAD_DIM, (h + 1) * HEAD_DIM)
        oh = o[:, hs]
        ms = jnp.mean(oh * oh, axis=-1, keepdims=True)
        z = oh * lax.rsqrt(ms + EPS) * gn_ref[:, hs] * sg_ref[0, :, hs].astype(F32)
        z_scr[:, hs] = z.astype(BF16)
    y = _dot(z_scr[...], w_ref[...])
    _post_mixer(y, xp_ref[0], slice(0, y.shape[0]), nw1_ref, g1_ref, nw2_ref, sh2_ref, sc2_ref, rhi_ref, rlo_ref,
                x1_ref, h2_ref, lg_ref)


def _hg_out(o_f, o_b, sg, gnorm, w_out_bf, xp, nw1, g1, nw2, sh2, sc2, r_hi, r_lo, tm):
    bsz, t, d = xp.shape
    rspec = pl.BlockSpec((d, LANES), lambda b, i: (0, 0))
    tok = pl.BlockSpec((1, tm, d), lambda b, i: (b, i, 0))
    vec = pl.BlockSpec((1, d), lambda b, i: (0, 0))
    bvec = pl.BlockSpec((1, 1, d), lambda b, i: (b, 0, 0))
    return pl.pallas_call(
        _hgout_kernel,
        grid=(bsz, t // tm),
        in_specs=[tok, tok, tok, vec, pl.BlockSpec((d, d), lambda b, i: (0, 0)), tok, vec, bvec, vec, bvec, bvec,
                  rspec, rspec],
        out_specs=[tok, _tile_rows_spec(tm, d, lambda b, i: (b, i, 0)),
                   pl.BlockSpec((1, tm, LANES), lambda b, i: (b, i, 0))],
        out_shape=[jax.ShapeDtypeStruct((bsz, t, d), F32), jax.ShapeDtypeStruct((bsz, t * (d // LANES), LANES), F32),
                   jax.ShapeDtypeStruct((bsz, t, LANES), F32)],
        scratch_shapes=[pltpu.VMEM((tm, d), BF16)],
        compiler_params=_cparams(("arbitrary", "arbitrary")),
        name="hg_out",
    )(o_f, o_b, sg, gnorm, w_out_bf, xp, nw1, g1, nw2, sh2, sc2, r_hi, r_lo)


def _sgu_kernel(x1p_ref, moe_ref, nw3p_ref, g2p_ref, nw0_ref, sh1_ref, sc1_ref, win_ref, lnw_ref, lnb_ref,
                ws_ref, bs_ref, wout_ref, nw1_ref, g1_ref, nw2_ref, sh2_ref, sc2_ref, rhi_ref, rlo_ref,
                x1_ref, h2_ref, lg_ref):
    tm, d = x1p_ref.shape[1], x1p_ref.shape[2]
    w = lnw_ref.shape[-1]
    gd = w // SGU_GROUPS
    nt = d // LANES

    def gelu(z):
        return 0.5 * z * (1.0 + jnp.tanh(math.sqrt(2.0 / math.pi) * (z + 0.044715 * (z * z * z))))

    for ck in range(tm // SGU_CHUNK):
        rs = slice(ck * SGU_CHUNK, (ck + 1) * SGU_CHUNK)
        moe = _load_tile_rows(moe_ref.at[0, pl.ds(ck * SGU_CHUNK * nt, SGU_CHUNK * nt)], SGU_CHUNK, d)
        x = x1p_ref[0, rs, :] + g2p_ref[0] * _rms(moe, nw3p_ref[...])
        h = _rms(x, nw0_ref[...]) * (1.0 + sc1_ref[0]) + sh1_ref[0]
        hb = h.astype(BF16)
        u = gelu(_dot(hb, win_ref[:, 0:w]))
        v = gelu(_dot(hb, win_ref[:, w:2 * w]))
        mu = jnp.mean(v, axis=-1, keepdims=True)
        vc = v - mu
        vn = vc * lax.rsqrt(jnp.mean(vc * vc, axis=-1, keepdims=True) + EPS) * lnw_ref[...] + lnb_ref[...]
        vnb = vn.astype(BF16)
        gated = []
        for g in range(SGU_GROUPS):
            cs = slice(g * gd, (g + 1) * gd)
            mixed = _dot(ws_ref[g], vnb[:, cs]) + bs_ref[:, g:g + 1]
            gated.append((u[:, cs] * mixed).astype(BF16))
        y = _dot(jnp.concatenate(gated, axis=1), wout_ref[...])
        _post_mixer(y, x, rs, nw1_ref, g1_ref, nw2_ref, sh2_ref, sc2_ref, rhi_ref, rlo_ref, x1_ref, h2_ref, lg_ref)


def _sgu(x1p, moe, nw3p, g2p, nw0, sh1, sc1, w_in_bf, ln_w, ln_b, w_s_bf, b_s_t, w_out_bf,
         nw1, g1, nw2, sh2, sc2, r_hi, r_lo, tm):
    bsz, t, d = x1p.shape
    w = ln_w.shape[-1]
    tok = pl.BlockSpec((1, tm, d), lambda b, i: (b, i, 0))
    tiles = _tile_rows_spec(tm, d, lambda b, i: (b, i, 0))
    vec = pl.BlockSpec((1, d), lambda b, i: (0, 0))
    wvec = pl.BlockSpec((1, w), lambda b, i: (0, 0))
    bvec = pl.BlockSpec((1, 1, d), lambda b, i: (b, 0, 0))

    def full(a):
        return pl.BlockSpec(a.shape, lambda b, i: (0,) * a.ndim)

    return pl.pallas_call(
        _sgu_kernel,
        grid=(bsz, t // tm),
        in_specs=[tok, tiles, vec, bvec, vec, bvec, bvec, full(w_in_bf), wvec, wvec, full(w_s_bf), full(b_s_t),
                  full(w_out_bf), vec, bvec, vec, bvec, bvec, full(r_hi), full(r_lo)],
        out_specs=[tok, tiles, pl.BlockSpec((1, tm, LANES), lambda b, i: (b, i, 0))],
        out_shape=[jax.ShapeDtypeStruct((bsz, t, d), F32), jax.ShapeDtypeStruct((bsz, t * (d // LANES), LANES), F32),
                   jax.ShapeDtypeStruct((bsz, t, LANES), F32)],
        compiler_params=_cparams(("arbitrary", "arbitrary")),
        name="sgu",
    )(x1p, moe, nw3p, g2p, nw0, sh1, sc1, w_in_bf, ln_w, ln_b, w_s_bf, b_s_t, w_out_bf,
      nw1, g1, nw2, sh2, sc2, r_hi, r_lo)


def _token_prefix(mask, triu, slow):
    local = _dot(mask.astype(BF16), triu)
    rowtot = jnp.broadcast_to(local[:, LANES - 1:LANES], local.shape)
    prev = _dot(slow, rowtot.astype(BF16))
    return local, prev, rowtot


def _route_kernel(lg_ref, triu_ref, slow_ref, idx_ref, gate_ref, a_scr, thr_scr, rhs_scr, *, cap):
    ne, nr = a_scr.shape[0], a_scr.shape[1]
    for r in range(nr):
        a_scr[:, r, :] = lg_ref[0, r * LANES:(r + 1) * LANES, :].T[0:ne, :]
    lg = a_scr[...]
    ex = jnp.exp(lg - jnp.max(lg, axis=0, keepdims=True))
    a = ex / jnp.sum(ex, axis=0, keepdims=True)
    a_scr[...] = a
    capf = jnp.float32(cap)

    def count(m):
        return jnp.sum(jnp.sum(m.astype(F32), axis=1, keepdims=True), axis=2, keepdims=True)

    def unresolved(state):
        it, lo, hi = state
        mid = 0.5 * (lo + hi)
        return jnp.logical_and(it < BISECT_ITERS, jnp.max(((mid != lo) & (mid != hi)).astype(F32)) > 0.0)

    def bis(state):
        it, lo, hi = state
        mid = 0.5 * (lo + hi)
        ge = count(a >= mid) >= capf
        return it + 1, jnp.where(ge, mid, lo), jnp.where(ge, hi, mid)

    _, lo, _ = lax.while_loop(unresolved, bis, (jnp.int32(0), jnp.zeros((ne, 1, 1), F32), jnp.full((ne, 1, 1), 2.0, F32)))
    thr_scr[...] = jnp.broadcast_to(lo, thr_scr.shape)

    idx_ref[0] = jnp.zeros(idx_ref.shape[1:], I32)
    gate_ref[0] = jnp.zeros(gate_ref.shape[1:], F32)
    triu = triu_ref[...]
    slow = slow_ref[...]
    lane = lax.broadcasted_iota(I32, (LANES, LANES), 1)
    lane_f = lane.astype(F32)
    sub_f = lax.broadcasted_iota(I32, (LANES, LANES), 0).astype(F32)
    rowid = lax.broadcasted_iota(I32, (nr, LANES), 0).astype(F32)

    def per_expert(e, carry):
        ae = a_scr[e]
        v = thr_scr[e][0:1, :]
        gt = ae > v
        eq = ae == v
        need = capf - jnp.sum(jnp.sum(gt.astype(F32), axis=0, keepdims=True), axis=1, keepdims=True)
        eql, eqp, _ = _token_prefix(eq, triu, slow)
        sel = gt | (eq & ((eql + eqp - eq.astype(F32)) < need))
        local, prev, rowtot = _token_prefix(sel, triu, slow)
        rowcum = prev + rowtot
        prev_hi = jnp.floor(prev * (1.0 / LANES))
        a0, a1, a2 = _split3(ae)
        for k, piece in enumerate((local, prev_hi, prev - LANES * prev_hi, rowid, a0, a1, a2)):
            rhs_scr[:, k * LANES:(k + 1) * LANES] = piece.astype(BF16)

        for p in range(cap // LANES):
            base = float(p * LANES + 1)
            slot_row = base + lane_f[0:1, :]
            onehot_t = ((prev < slot_row) & (rowcum >= slot_row)).astype(BF16)
            g = _dot_tn(onehot_t, rhs_scr[...])
            g_local = g[:, 0:LANES]
            g_prev = LANES * g[:, LANES:2 * LANES] + g[:, 2 * LANES:3 * LANES]
            g_row = g[:, 3 * LANES:4 * LANES]
            g_a = g[:, 4 * LANES:5 * LANES] + g[:, 5 * LANES:6 * LANES] + g[:, 6 * LANES:7 * LANES]
            slot_col = base + sub_f
            lstar = jnp.sum(((g_local + g_prev) < slot_col).astype(F32), axis=1, keepdims=True)
            tok = LANES * g_row[:, 0:1] + lstar
            gat = jnp.sum(jnp.where(lane_f == lstar, g_a, 0.0), axis=1, keepdims=True)
            rs = slice(p * LANES, (p + 1) * LANES)
            idx_ref[0, rs, :] = jnp.where(lane == e, tok.astype(I32), idx_ref[0, rs, :])
            gate_ref[0, rs, :] = jnp.where(lane == e, gat, gate_ref[0, rs, :])
        return carry

    lax.fori_loop(0, ne, per_expert, 0)


def _route(logits, ne, cap):
    bsz, t, _ = logits.shape
    nr = t // LANES
    assert cap % LANES == 0 and nr % SUBLANES == 0
    triu = (jnp.arange(LANES)[:, None] <= jnp.arange(LANES)[None, :]).astype(BF16)
    slow = (jnp.arange(nr)[:, None] > jnp.arange(nr)[None, :]).astype(BF16)
    idx, gate = pl.pallas_call(
        functools.partial(_route_kernel, cap=cap),
        grid=(bsz,),
        in_specs=[pl.BlockSpec((1, t, LANES), lambda b: (b, 0, 0)),
                  pl.BlockSpec((LANES, LANES), lambda b: (0, 0)), pl.BlockSpec((nr, nr), lambda b: (0, 0))],
        out_specs=[pl.BlockSpec((1, cap, LANES), lambda b: (b, 0, 0))] * 2,
        out_shape=[jax.ShapeDtypeStruct((bsz, cap, LANES), I32), jax.ShapeDtypeStruct((bsz, cap, LANES), F32)],
        scratch_shapes=[pltpu.VMEM((ne, nr, LANES), F32), pltpu.VMEM((ne, SUBLANES, LANES), F32),
                        pltpu.VMEM((nr, 7 * LANES), BF16)],
        compiler_params=_cparams(("arbitrary",)),
        name="route",
    )(logits, triu, slow)
    return idx, gate


ROW_UNROLL = 8


def _gather_kernel(src_ref, h_hbm, o_ref, h_scr, row_scr, sem, *, cap, ne, nt):
    b, e = pl.program_id(0), pl.program_id(1)

    @pl.when(e == 0)
    def _():
        cp = pltpu.make_async_copy(h_hbm.at[b], h_scr, sem.at[0])
        cp.start()
        cp.wait()

    base = (b * ne + e) * cap
    group = nt * SUBLANES

    def body(jj, carry):
        for u in range(SUBLANES):
            row_scr[pl.ds(jj * group + u, nt, stride=SUBLANES), :] = \
                h_scr[pl.ds(src_ref[base + jj * SUBLANES + u], nt, stride=SUBLANES), :]
        return carry

    lax.fori_loop(0, cap // SUBLANES, body, 0)

    def emit(g, carry):
        rows = [jnp.concatenate([row_scr[pl.ds((g * 2 + i) * group + c * SUBLANES, SUBLANES), :] for c in range(nt)],
                                axis=1) for i in range(2)]
        o_ref[0, 0, pl.ds(g * 2 * SUBLANES, 2 * SUBLANES), :] = jnp.concatenate(rows, axis=0).astype(BF16)
        return carry

    lax.fori_loop(0, cap // (2 * SUBLANES), emit, 0)


def _gather(src_flat, h2_rows, ne, cap, t):
    bsz, rows, _ = h2_rows.shape
    nt = rows // t
    return pl.pallas_call(
        functools.partial(_gather_kernel, cap=cap, ne=ne, nt=nt),
        grid_spec=pltpu.PrefetchScalarGridSpec(
            num_scalar_prefetch=1,
            grid=(bsz, ne),
            in_specs=[pl.BlockSpec(memory_space=pl.ANY)],
            out_specs=pl.BlockSpec((1, 1, cap, nt * LANES), lambda b, e, src: (b, e, 0, 0)),
            scratch_shapes=[pltpu.VMEM((rows, LANES), F32), pltpu.VMEM((cap * nt, LANES), F32),
                            pltpu.SemaphoreType.DMA((1,))],
        ),
        out_shape=jax.ShapeDtypeStruct((bsz, ne, cap, nt * LANES), BF16),
        compiler_params=_cparams(("arbitrary", "arbitrary")),
        name="gather",
    )(src_flat, h2_rows)


def _ffn_kernel(xs_ref, gate_ref, wg_ref, wu_ref, wd_ref, y_ref):
    e, f = pl.program_id(0), pl.program_id(1)
    cap, d = xs_ref.shape[2], xs_ref.shape[3]
    nt = d // LANES

    @pl.when(f == 0)
    def _():
        y_ref[...] = jnp.zeros(y_ref.shape, F32)

    wg = wg_ref[0, 0].astype(BF16)
    wu = wu_ref[0, 0].astype(BF16)
    wd = wd_ref[0, 0].astype(BF16)
    lane = lax.broadcasted_iota(I32, (cap, LANES), 1)
    for b in range(xs_ref.shape[0]):
        xb = xs_ref[b, 0]
        g = _dot(xb, wg)
        u = _dot(xb, wu)
        gate = jnp.sum(jnp.where(lane == e, gate_ref[b], 0.0), axis=1, keepdims=True)
        hid = (g * jax.nn.sigmoid(g) * u * gate).astype(BF16)
        part = _dot(hid, wd)
        for i in range(cap // SUBLANES):
            for c in range(nt):
                r0 = (i * nt + c) * SUBLANES
                y_ref[b, 0, r0:r0 + SUBLANES, :] += part[i * SUBLANES:(i + 1) * SUBLANES, c * LANES:(c + 1) * LANES]


def _ffn(xs, gate_cols, w_gate, w_up, w_down, layer, tf):
    bsz, ne, cap, d = xs.shape
    ff = w_gate.shape[-1]
    tf = min(tf, ff)
    nt = d // LANES
    return pl.pallas_call(
        _ffn_kernel,
        grid=(ne, ff // tf),
        in_specs=[pl.BlockSpec((bsz, 1, cap, d), lambda e, f: (0, e, 0, 0)),
                  pl.BlockSpec((bsz, cap, LANES), lambda e, f: (0, 0, 0)),
                  pl.BlockSpec((1, 1, d, tf), lambda e, f: (layer, e, 0, f)),
                  pl.BlockSpec((1, 1, d, tf), lambda e, f: (layer, e, 0, f)),
                  pl.BlockSpec((1, 1, tf, d), lambda e, f: (layer, e, f, 0))],
        out_specs=pl.BlockSpec((bsz, 1, cap * nt, LANES), lambda e, f: (0, e, 0, 0)),
        out_shape=jax.ShapeDtypeStruct((bsz, ne, cap * nt, LANES), F32),
        compiler_params=_cparams(("arbitrary", "arbitrary")),
        name="ffn",
    )(xs, gate_cols, w_gate, w_up, w_down)


SCATTER_SPLIT = 8


def _scatter_kernel(dst_ref, cut_ref, y_ref, zero_hbm, o_hbm, *scr, cap, ne, nt):
    accs, sem = scr[:SCATTER_SPLIT], scr[SCATTER_SPLIT]
    b, e = pl.program_id(0), pl.program_id(1)

    @pl.when(e == 0)
    def _():
        fills = [pltpu.make_async_copy(zero_hbm, acc, sem.at[k]) for k, acc in enumerate(accs)]
        for cp in fills:
            cp.start()
        for cp in fills:
            cp.wait()

    base = (b * ne + e) * cap
    cbase = (b * ne + e) * (SCATTER_SPLIT + 1)
    starts = [cut_ref[cbase + k] for k in range(SCATTER_SPLIT)]
    counts = [cut_ref[cbase + k + 1] - starts[k] for k in range(SCATTER_SPLIT)]
    shortest = functools.reduce(jnp.minimum, counts)
    longest = functools.reduce(jnp.maximum, counts)

    def update(acc, j, dst, scale=None):
        src = _tile_row_base(j, nt)
        row = y_ref[0, 0, pl.ds(src, nt, stride=SUBLANES), :]
        acc[pl.ds(dst, nt, stride=SUBLANES), :] += row if scale is None else scale * row

    def common(i, carry):
        for k, acc in enumerate(accs):
            j = starts[k] + i
            update(acc, j, dst_ref[base + j])
        return carry

    def tail(i, carry):
        for k, acc in enumerate(accs):
            valid = i < counts[k]
            j = jnp.minimum(starts[k] + i, cap - 1)
            update(acc, j, jnp.where(valid, dst_ref[base + j], 0), jnp.where(valid, 1.0, 0.0))
        return carry

    lax.fori_loop(0, shortest, common, 0)
    lax.fori_loop(shortest, longest, tail, 0)

    @pl.when(e == ne - 1)
    def _():
        part = accs[0].shape[0]
        copies = [pltpu.make_async_copy(acc, o_hbm.at[b, pl.ds(k * part, part)], sem.at[k])
                  for k, acc in enumerate(accs)]
        for cp in copies:
            cp.start()
        for cp in copies:
            cp.wait()


def _scatter(dst_flat, cuts_flat, y_rows, t, nt):
    bsz, ne, rows, _ = y_rows.shape
    cap = rows // nt
    part_rows = (t // SCATTER_SPLIT) * nt
    zeros = jnp.zeros((part_rows, LANES), F32)
    return pl.pallas_call(
        functools.partial(_scatter_kernel, cap=cap, ne=ne, nt=nt),
        grid_spec=pltpu.PrefetchScalarGridSpec(
            num_scalar_prefetch=2,
            grid=(bsz, ne),
            in_specs=[pl.BlockSpec((1, 1, rows, LANES), lambda b, e, i, c: (b, e, 0, 0)),
                      pl.BlockSpec(memory_space=pl.ANY)],
            out_specs=pl.BlockSpec(memory_space=pl.ANY),
            scratch_shapes=[pltpu.VMEM((part_rows, LANES), F32)] * SCATTER_SPLIT
            + [pltpu.SemaphoreType.DMA((SCATTER_SPLIT,))],
        ),
        out_shape=jax.ShapeDtypeStruct((bsz, t * nt, LANES), F32),
        compiler_params=_cparams(("arbitrary", "arbitrary")),
        name="scatter",
    )(dst_flat, cuts_flat, y_rows, zeros)


def _moe(h2_rows, logits, ne, w_gate, w_up, w_down, layer, tf):
    bsz, t, _ = logits.shape
    nt = h2_rows.shape[1] // t
    cap = EC_CAPACITY_FACTOR * t // ne
    tpart = t // SCATTER_SPLIT
    idx_cols, gate_cols = _route(logits, ne, cap)
    idx = jnp.swapaxes(idx_cols[:, :, :ne], 1, 2)
    xs = _gather(_tile_row_base(idx, nt).reshape(-1), h2_rows, ne, cap, t)
    y = _ffn(xs, gate_cols, w_gate, w_up, w_down, layer, tf)
    edges = jnp.arange(SCATTER_SPLIT + 1, dtype=I32) * tpart
    cuts = jnp.sum(idx[..., None] < edges, axis=2).astype(I32)
    return _scatter(_tile_row_base(idx % tpart, nt).reshape(-1), cuts.reshape(-1), y, t, nt)


def _final_kernel(x_ref, moe_ref, nw_ref, g_ref, o_ref):
    x = x_ref[0]
    o_ref[0] = x + g_ref[0] * _rms(_load_tile_rows(moe_ref.at[0], x.shape[0], x.shape[1]), nw_ref[...])


def _final(x1, moe, nw3, g2, tm):
    bsz, t, d = x1.shape
    tok = pl.BlockSpec((1, tm, d), lambda b, i: (b, i, 0))
    return pl.pallas_call(
        _final_kernel,
        grid=(bsz, t // tm),
        in_specs=[tok, _tile_rows_spec(tm, d, lambda b, i: (b, i, 0)), pl.BlockSpec((1, d), lambda b, i: (0, 0)),
                  pl.BlockSpec((1, 1, d), lambda b, i: (b, 0, 0))],
        out_specs=tok,
        out_shape=jax.ShapeDtypeStruct((bsz, t, d), F32),
        compiler_params=_cparams(("arbitrary", "arbitrary")),
        name="final",
    )(x1, moe, nw3, g2)


def _sincos_tables(rows, dim):
    quarter = dim // 4
    freqs = jnp.exp(-math.log(10000.0) * jnp.arange(quarter, dtype=F32) / quarter)

    def emb1d(n):
        ang = jnp.arange(n, dtype=F32)[:, None] * freqs[None, :]
        return jnp.concatenate([jnp.sin(ang), jnp.cos(ang)], axis=-1)

    return emb1d(rows), emb1d(GRID_W)


TOKEN_TILE = 256
FFN_TILE = 512


def kernel(x, c, ctx, c_ctx, w_ada, b_ada, norm_w, hg_w_in, hg_lb, hg_gnorm, hg_w_out, sg_w_in, sg_ln_w, sg_ln_b,
           sg_w_s, sg_b_s, sg_w_out, moe_router, moe_w_gate, moe_w_up, moe_w_down):
    bsz, t, d = x.shape
    depth = w_ada.shape[0]
    assert depth == 2 and d == N_HEADS * HEAD_DIM and t % GLA_CHUNK == 0 and ctx.shape[1] % GLA_CHUNK == 0
    tm = min(TOKEN_TILE, t)

    cvec = jnp.zeros((SUBLANES, d), F32).at[:bsz].set(c).at[bsz].set(c_ctx)
    mod = _ada(cvec, w_ada, b_ada)

    def mods(layer, rows):
        m = mod[layer, rows].reshape(-1, N_ADA, 1, d)
        return [m[:, k] for k in range(N_ADA)]

    nw = norm_w.reshape(depth, 4, 1, d)
    ne = moe_router.shape[-1]
    router = jnp.zeros((depth, d, LANES), F32).at[:, :, :ne].set(moe_router)
    r_hi = router.astype(BF16)
    r_lo = (router - r_hi.astype(F32)).astype(BF16)

    sh1, sc1, g1, sh2, sc2, g2 = mods(0, slice(0, bsz))
    sh1c, sc1c = [jnp.broadcast_to(m, (bsz, 1, d)) for m in mods(0, slice(bsz, bsz + 1))[:2]]
    w_in_bf = hg_w_in[0].astype(BF16)
    pos = _sincos_tables(t // GRID_W, d)
    tri, masks = _gla_consts()

    cv, ckf, cgf, ckb, cgb, cq, _ = _hg_in(ctx, None, nw[0, 0], sh1c, sc1c, w_in_bf, hg_lb, 1, tm)
    s_zero = jnp.zeros((bsz, 2, N_HEADS, HEAD_DIM, HEAD_DIM), F32)
    _, _, s_ctx = _gla(ckf, cgf, ckb, cgb, cv, cq, s_zero, tri, masks)

    xp, v, kf, gf, kb, gb, q, sg = _hg_in(x, pos, nw[0, 0], sh1, sc1, w_in_bf, hg_lb, 1, tm)
    o_f, o_b, _ = _gla(kf, gf, kb, gb, v, q, s_ctx, tri, masks)
    x1, h2, lg = _hg_out(o_f, o_b, sg, hg_gnorm[0:1], hg_w_out[0].astype(BF16), xp, nw[0, 1], g1, nw[0, 2],
                         sh2, sc2, r_hi[0], r_lo[0], tm)
    moe = _moe(h2, lg, ne, moe_w_gate, moe_w_up, moe_w_down, 0, FFN_TILE)

    sh1b, sc1b, g1b, sh2b, sc2b, g2b = mods(1, slice(0, bsz))
    x1b, h2b, lgb = _sgu(x1, moe, nw[0, 3], g2, nw[1, 0], sh1b, sc1b, sg_w_in[0].astype(BF16),
                         sg_ln_w[0:1], sg_ln_b[0:1], sg_w_s[0].astype(BF16), sg_b_s[0].T, sg_w_out[0].astype(BF16),
                         nw[1, 1], g1b, nw[1, 2], sh2b, sc2b, r_hi[1], r_lo[1], tm)
    moe_b = _moe(h2b, lgb, ne, moe_w_gate, moe_w_up, moe_w_down, 1, FFN_TILE)
    return _final(x1b, moe_b, nw[1, 3], g2b, tm)
```

```python
import functools
import math

import jax
import jax.numpy as jnp
from jax import lax
from jax.experimental import pallas as pl
from jax.experimental.pallas import tpu as pltpu

F32 = jnp.float32
BF16 = jnp.bfloat16
I32 = jnp.int32
HIGHEST = lax.Precision.HIGHEST

EPS = 1e-6
GRID_W = 64
N_ADA = 6
N_HEADS = 8
HEAD_DIM = 128
N_EXPERTS = 16
EC_CAPACITY_FACTOR = 2
SGU_CHUNK = 128
SGU_GROUPS = 8

LANES = 128
SUBLANES = 8
GLA_CHUNK = 128
GLA_LEVELS = (64, 32, 16, 8, 4, 2, 1)
GLA_SHORT_BLOCK = 32
GLA_SHORT_MAX_EXPONENT = 60.0
BISECT_ITERS = 160
VMEM_LIMIT = 52 * 1024 * 1024


def _cparams(sem):
    return pltpu.CompilerParams(dimension_semantics=sem, vmem_limit_bytes=VMEM_LIMIT)


def _rms(x, w):
    ms = jnp.mean(x * x, axis=-1, keepdims=True)
    return x * lax.rsqrt(ms + EPS) * w


def _dot(a, b):
    return jnp.dot(a, b, preferred_element_type=F32)


def _dot_nt(a, b):
    return lax.dot_general(a, b, (((1,), (1,)), ((), ())), preferred_element_type=F32)


def _dot_tn(a, b):
    return lax.dot_general(a, b, (((0,), (0,)), ((), ())), preferred_element_type=F32)


def _ada_kernel(c_ref, w_ref, b_ref, o_ref):
    c = c_ref[...]
    s = c * jax.nn.sigmoid(c)
    o_ref[0] = jnp.dot(s, w_ref[0], precision=HIGHEST, preferred_element_type=F32) + b_ref[0]


def _ada(cvec, w_ada, b_ada):
    depth, d, nd = w_ada.shape
    rows = cvec.shape[0]
    return pl.pallas_call(
        _ada_kernel,
        grid=(depth, nd // d),
        in_specs=[pl.BlockSpec((rows, d), lambda l, n: (0, 0)),
                  pl.BlockSpec((1, d, d), lambda l, n: (l, 0, n)),
                  pl.BlockSpec((1, 1, d), lambda l, n: (l, 0, n))],
        out_specs=pl.BlockSpec((1, rows, d), lambda l, n: (l, 0, n)),
        out_shape=jax.ShapeDtypeStruct((depth, rows, nd), F32),
        compiler_params=_cparams(("arbitrary", "arbitrary")),
        name="ada",
    )(cvec, w_ada, b_ada.reshape(depth, 1, nd))


def _hgin_body(x, nw_ref, sh_ref, sc_ref, w_ref, lb_ref, outs, n_lb):
    v_ref, kf_ref, gf_ref, kb_ref, gb_ref, q_ref, sg_ref = outs
    d = x.shape[-1]
    h = _rms(x, nw_ref[...]) * (1.0 + sc_ref[0]) + sh_ref[0]
    hb = h.astype(BF16)
    lbs = lb_ref[...]
    e = jnp.exp(lbs - jnp.max(lbs, axis=0, keepdims=True))
    lb = jnp.sum(e[:n_lb], axis=0) / jnp.sum(e, axis=0)

    v_ref[0] = _dot(hb, w_ref[:, 0:d]).astype(BF16)
    for j, (k_ref, g_ref) in enumerate(((kf_ref, gf_ref), (kb_ref, gb_ref))):
        raw = _dot(hb, w_ref[:, (1 + j) * d:(2 + j) * d])
        lbj = lb[j:j + 1]
        sig = jax.nn.sigmoid(raw)
        f = lbj + (1.0 - lbj) * sig
        k_ref[0] = ((1.0 - lbj) * (1.0 - sig)).astype(BF16)
        g_ref[0] = jnp.log(f)
    qr = _dot(hb, w_ref[:, 3 * d:4 * d])
    q_ref[0] = (qr * jax.nn.sigmoid(qr)).astype(BF16)
    gr = _dot(hb, w_ref[:, 4 * d:5 * d])
    sg_ref[0] = (gr * jax.nn.sigmoid(gr)).astype(BF16)


def _with_pos(x, er_ref, ec_ref):
    er, ec = er_ref[0], ec_ref[...]
    pos = jnp.concatenate([jnp.concatenate([jnp.broadcast_to(er[r:r + 1], ec.shape), ec], axis=1)
                           for r in range(er.shape[0])], axis=0)
    return x + pos


def _pos_specs(pos, tm, d):
    er, ec = pos
    assert tm % GRID_W == 0
    rpt = tm // GRID_W
    specs = [pl.BlockSpec((1, rpt, d // 2), lambda b, i: (i, 0, 0)), pl.BlockSpec(ec.shape, lambda b, i: (0, 0))]
    return specs, (er.reshape(-1, rpt, d // 2), ec)


def _hgin_pos_kernel(x_ref, er_ref, ec_ref, nw_ref, sh_ref, sc_ref, w_ref, lb_ref, *outs, n_lb):
    _hgin_body(_with_pos(x_ref[0], er_ref, ec_ref), nw_ref, sh_ref, sc_ref, w_ref, lb_ref, outs, n_lb)


def _hgin_kernel(x_ref, nw_ref, sh_ref, sc_ref, w_ref, lb_ref, *outs, n_lb):
    _hgin_body(x_ref[0], nw_ref, sh_ref, sc_ref, w_ref, lb_ref, outs, n_lb)


def _hg_in(x, pos, nw, sh, sc, w_bf, lb_raw, n_lb, tm):
    bsz, t, d = x.shape
    tm = min(tm, t)
    tok = pl.BlockSpec((1, tm, d), lambda b, i: (b, i, 0))
    vec = pl.BlockSpec((1, d), lambda b, i: (0, 0))
    bvec = pl.BlockSpec((1, 1, d), lambda b, i: (b, 0, 0))
    wspec = pl.BlockSpec(w_bf.shape, lambda b, i: (0, 0))
    lbspec = pl.BlockSpec(lb_raw.shape, lambda b, i: (0, 0, 0))
    gate_shapes = [jax.ShapeDtypeStruct((bsz, t, d), dt) for dt in (BF16, BF16, F32, BF16, F32, BF16, BF16)]
    if pos is not None:
        kern = functools.partial(_hgin_pos_kernel, n_lb=n_lb)
        pspecs, pargs = _pos_specs(pos, tm, d)
        in_specs = [tok] + pspecs + [vec, bvec, bvec, wspec, lbspec]
        args = (x,) + pargs + (nw, sh, sc, w_bf, lb_raw)
        out_shape = gate_shapes
    else:
        kern = functools.partial(_hgin_kernel, n_lb=n_lb)
        in_specs = [tok, vec, bvec, bvec, wspec, lbspec]
        args = (x, nw, sh, sc, w_bf, lb_raw)
        out_shape = gate_shapes
    return pl.pallas_call(
        kern,
        grid=(bsz, t // tm),
        in_specs=in_specs,
        out_specs=[tok] * len(out_shape),
        out_shape=out_shape,
        compiler_params=_cparams(("arbitrary", "arbitrary")),
        name="hg_in",
    )(*args)


def _gla_consts():
    c = GLA_CHUNK
    t = jnp.arange(c)[:, None]
    s = jnp.arange(c)[None, :]
    tri = jnp.stack([(s <= t), (s >= t)]).astype(BF16)
    masks = []
    for rev in (False, True):
        lv = []
        for m in GLA_LEVELS:
            same = (t // (2 * m)) == (s // (2 * m))
            tq = ((t // m) % 2) == (0 if rev else 1)
            sk = ((s // m) % 2) == (1 if rev else 0)
            lv.append(same & tq & sk)
        lv.append(t == s)
        lv.append(((t // GLA_SHORT_BLOCK) == (s // GLA_SHORT_BLOCK)) & ((s >= t) if rev else (s <= t)))
        masks.append(jnp.stack(lv))
    return tri, jnp.stack(masks).astype(F32)


def _split3(x):
    p0 = x.astype(BF16)
    r1 = x - p0.astype(F32)
    p1 = r1.astype(BF16)
    p2 = (r1 - p1.astype(F32)).astype(BF16)
    return p0, p1, p2


def _level_ref(b_scr, hs, m, rev):
    c = GLA_CHUNK
    off = m if rev else m - 1

    def row8(i):
        return jnp.broadcast_to(b_scr[pl.ds(i, 1), hs], (SUBLANES, HEAD_DIM))

    pieces = []
    if m >= SUBLANES:
        for blk in range(c // (2 * m)):
            r8 = row8(blk * 2 * m + off)
            pieces.extend([r8] * (2 * m // SUBLANES))
    else:
        sub = lax.broadcasted_iota(I32, (SUBLANES, HEAD_DIM), 0) // (2 * m)
        for grp in range(c // SUBLANES):
            piece = row8(grp * SUBLANES + off)
            for cls in range(1, SUBLANES // (2 * m)):
                piece = jnp.where(sub == cls, row8(grp * SUBLANES + cls * 2 * m + off), piece)
            pieces.append(piece)
    return jnp.concatenate(pieces, axis=0)


def _block_decay_bound(b_scr, rev):
    c, blk = GLA_CHUNK, GLA_SHORT_BLOCK
    worst = None
    for i in range(c // blk):
        inner = b_scr[pl.ds(i * blk if rev else (i + 1) * blk - 1, 1), :]
        r = (i + 1) * blk if rev else i * blk - 1
        span = jnp.abs(inner - b_scr[pl.ds(r, 1), :]) if 0 <= r < c else jnp.abs(inner)
        worst = span if worst is None else jnp.maximum(worst, span)
    return jnp.max(worst)


def _gla_head(q_ref, k_ref, v_ref, o_ref, mask_ref, st_scr, b_scr, d_idx, rev, h, short):
    c = GLA_CHUNK
    n_lv = len(GLA_LEVELS)
    hs = pl.ds(pl.multiple_of(h * HEAD_DIM, HEAD_DIM), HEAD_DIM)
    b = b_scr[:, hs]
    q = q_ref[0, :, hs]
    k = k_ref[0, :, hs]
    vb = v_ref[0, :, hs]
    bl = b_scr[pl.ds(0 if rev else c - 1, 1), hs]
    st = st_scr[d_idx, h]

    if short:
        blk = GLA_SHORT_BLOCK
        nb = c // blk
        own = []
        for i in range(nb):
            r = (i + 1) * blk if rev else i * blk - 1
            own.append(b_scr[pl.ds(r, 1), hs] if 0 <= r < c else jnp.zeros((1, HEAD_DIM), F32))

        def blockwise(rows):
            return jnp.concatenate([jnp.broadcast_to(r, (blk, HEAD_DIM)) for r in rows], axis=0)

        u = blockwise(own) - b
        qh = q * jnp.exp(-u).astype(BF16)
        kh = k * jnp.exp(u).astype(BF16)
        qt = qh * blockwise([jnp.exp(r) for r in own]).astype(BF16)
        kt = kh * blockwise([jnp.exp(bl - r) for r in own]).astype(BF16)
        a = jnp.where(mask_ref[d_idx, n_lv + 1] > 0.0, _dot_nt(qh, kh), 0.0)
        zero = jnp.zeros((1, HEAD_DIM), F32)
        for li, m in enumerate(GLA_LEVELS):
            if m < blk:
                continue
            qf, kf = [], []
            for i in range(nb):
                lvl = b_scr[pl.ds((i * blk) // (2 * m) * 2 * m + (m if rev else m - 1), 1), hs]
                is_query = ((i * blk) // m) % 2 == (0 if rev else 1)
                qf.append(jnp.exp(own[i] - lvl) if is_query else zero)
                kf.append(zero if is_query else jnp.exp(lvl - own[i]))
            a = a + mask_ref[d_idx, li] * _dot_nt(qh * blockwise(qf).astype(BF16), kh * blockwise(kf).astype(BF16))
    else:
        qt = q * jnp.exp(b).astype(BF16)
        kt = k * jnp.exp(bl - b).astype(BF16)

        def level(m):
            e = jnp.exp(-jnp.abs(b - _level_ref(b_scr, hs, m, rev))).astype(BF16)
            return _dot_nt(q * e, k * e)

        a = mask_ref[d_idx, n_lv] * _dot_nt(q, k)
        for li, m in enumerate(GLA_LEVELS):
            a = a + mask_ref[d_idx, li] * level(m)
    o_ref[0, :, hs] = (_dot_nt(qt, st.astype(BF16)) + _dot(a.astype(BF16), vb)).astype(o_ref.dtype)
    st_scr[d_idx, h] = st * jnp.exp(bl) + _dot_tn(vb, kt)


def _gla_kernel(kf_ref, gf_ref, vf_ref, qf_ref, kb_ref, gb_ref, vb_ref, qb_ref, s0_ref, tri_ref, mask_ref,
                of_ref, ob_ref, sout_ref, st_scr, bf_scr, bb_scr):
    n = pl.program_id(1)

    @pl.when(n == 0)
    def _():
        st_scr[...] = s0_ref[0]

    dirs = ((qf_ref, kf_ref, vf_ref, gf_ref, of_ref, bf_scr, 0, False),
            (qb_ref, kb_ref, vb_ref, gb_ref, ob_ref, bb_scr, 1, True))
    bounded = []
    for q_ref, k_ref, v_ref, g_ref, o_ref, b_scr, d_idx, rev in dirs:
        p0, p1, p2 = _split3(g_ref[0])
        tri = tri_ref[d_idx]
        b_scr[...] = _dot(tri, p0) + _dot(tri, p1) + _dot(tri, p2)
        bounded.append(_block_decay_bound(b_scr, rev) <= GLA_SHORT_MAX_EXPONENT)

    def heads(which, short, unroll):
        def body(h, carry):
            for q_ref, k_ref, v_ref, _, o_ref, b_scr, d_idx, rev in which:
                _gla_head(q_ref, k_ref, v_ref, o_ref, mask_ref, st_scr, b_scr, d_idx, rev, h, short)
            return carry
        lax.fori_loop(0, N_HEADS, body, 0, unroll=unroll)

    both = jnp.logical_and(bounded[0], bounded[1])

    @pl.when(both)
    def _():
        heads(dirs, True, 8)

    for d, ok in zip(dirs, bounded):
        @pl.when(jnp.logical_and(jnp.logical_not(both), ok))
        def _():
            heads((d,), True, 2)

        @pl.when(jnp.logical_not(ok))
        def _():
            heads((d,), False, 2)

    @pl.when(n == pl.num_programs(1) - 1)
    def _():
        sout_ref[0] = st_scr[...]


def _gla(kf, gf, kb, gb, v, q, s0, tri, masks):
    bsz, t, d = v.shape
    c = GLA_CHUNK
    n = t // c
    fwd = pl.BlockSpec((1, c, d), lambda b, i: (b, i, 0))
    bwd = pl.BlockSpec((1, c, d), lambda b, i: (b, n - 1 - i, 0))
    sspec = pl.BlockSpec((1,) + s0.shape[1:], lambda b, i: (b, 0, 0, 0, 0))
    return pl.pallas_call(
        _gla_kernel,
        grid=(bsz, n),
        in_specs=[fwd, fwd, fwd, fwd, bwd, bwd, bwd, bwd, sspec,
                  pl.BlockSpec(tri.shape, lambda b, i: (0, 0, 0)),
                  pl.BlockSpec(masks.shape, lambda b, i: (0, 0, 0, 0))],
        out_specs=[fwd, bwd, sspec],
        out_shape=[jax.ShapeDtypeStruct((bsz, t, d), BF16), jax.ShapeDtypeStruct((bsz, t, d), BF16),
                   jax.ShapeDtypeStruct(s0.shape, F32)],
        scratch_shapes=[pltpu.VMEM(s0.shape[1:], F32), pltpu.VMEM((c, d), F32), pltpu.VMEM((c, d), F32)],
        compiler_params=_cparams(("arbitrary", "arbitrary")),
        name="gla",
    )(kf, gf, v, q, kb, gb, v, q, s0, tri, masks)


def _store_tile_rows(ref2, x):
    n, d = x.shape
    nt = d // LANES
    for i in range(n // SUBLANES):
        for c in range(nt):
            r0 = (i * nt + c) * SUBLANES
            ref2[r0:r0 + SUBLANES, :] = x[i * SUBLANES:(i + 1) * SUBLANES, c * LANES:(c + 1) * LANES]


def _load_tile_rows(ref2, n, d):
    nt = d // LANES
    rows = []
    for i in range(n // SUBLANES):
        rows.append(jnp.concatenate(
            [ref2[(i * nt + c) * SUBLANES:(i * nt + c + 1) * SUBLANES, :] for c in range(nt)], axis=1))
    return jnp.concatenate(rows, axis=0)


def _tile_row_base(t, nt):
    return (t >> 3) * (nt * SUBLANES) + (t & (SUBLANES - 1))


def _tile_rows_spec(tm, d, index_map):
    return pl.BlockSpec((1, tm * (d // LANES), LANES), index_map)


def _post_mixer(y, xres, rows, nw1_ref, g1_ref, nw2_ref, sh2_ref, sc2_ref, rhi_ref, rlo_ref, x1_ref, h2_ref, lg_ref):
    d = y.shape[-1]
    nt = d // LANES
    x1 = xres + g1_ref[0] * _rms(y, nw1_ref[...])
    x1_ref[0, rows, :] = x1
    h2 = _rms(x1, nw2_ref[...]) * (1.0 + sc2_ref[0]) + sh2_ref[0]
    _store_tile_rows(h2_ref.at[0, pl.ds(rows.start * nt, (rows.stop - rows.start) * nt)], h2)
    h_hi = h2.astype(BF16)
    h_lo = (h2 - h_hi.astype(F32)).astype(BF16)
    lg_ref[0, rows, :] = _dot(h_hi, rhi_ref[...]) + (_dot(h_lo, rhi_ref[...]) + _dot(h_hi, rlo_ref[...]))


def _hgout_kernel(of_ref, ob_ref, sg_ref, gn_ref, w_ref, x_ref, er_ref, ec_ref, nw1_ref, g1_ref, nw2_ref, sh2_ref,
                  sc2_ref, rhi_ref, rlo_ref, x1_ref, h2_ref, lg_ref, z_scr):
    o = of_ref[0].astype(F32) + ob_ref[0].astype(F32)
    for h in range(N_HEADS):
        hs = slice(h * HEAD_DIM, (h + 1) * HEAD_DIM)
        oh = o[:, hs]
        ms = jnp.mean(oh * oh, axis=-1, keepdims=True)
        z = oh * lax.rsqrt(ms + EPS) * gn_ref[:, hs] * sg_ref[0, :, hs].astype(F32)
        z_scr[:, hs] = z.astype(BF16)
    y = _dot(z_scr[...], w_ref[...])
    _post_mixer(y, _with_pos(x_ref[0], er_ref, ec_ref), slice(0, y.shape[0]), nw1_ref, g1_ref, nw2_ref, sh2_ref,
                sc2_ref, rhi_ref, rlo_ref, x1_ref, h2_ref, lg_ref)


def _hg_out(o_f, o_b, sg, gnorm, w_out_bf, x, pos, nw1, g1, nw2, sh2, sc2, r_hi, r_lo, tm):
    bsz, t, d = x.shape
    rspec = pl.BlockSpec((d, LANES), lambda b, i: (0, 0))
    pspecs, pargs = _pos_specs(pos, tm, d)
    tok = pl.BlockSpec((1, tm, d), lambda b, i: (b, i, 0))
    vec = pl.BlockSpec((1, d), lambda b, i: (0, 0))
    bvec = pl.BlockSpec((1, 1, d), lambda b, i: (b, 0, 0))
    return pl.pallas_call(
        _hgout_kernel,
        grid=(bsz, t // tm),
        in_specs=[tok, tok, tok, vec, pl.BlockSpec((d, d), lambda b, i: (0, 0)), tok] + pspecs
        + [vec, bvec, vec, bvec, bvec, rspec, rspec],
        out_specs=[tok, _tile_rows_spec(tm, d, lambda b, i: (b, i, 0)),
                   pl.BlockSpec((1, tm, LANES), lambda b, i: (b, i, 0))],
        out_shape=[jax.ShapeDtypeStruct((bsz, t, d), F32), jax.ShapeDtypeStruct((bsz, t * (d // LANES), LANES), F32),
                   jax.ShapeDtypeStruct((bsz, t, LANES), F32)],
        scratch_shapes=[pltpu.VMEM((tm, d), BF16)],
        compiler_params=_cparams(("arbitrary", "arbitrary")),
        name="hg_out",
    )(o_f, o_b, sg, gnorm, w_out_bf, x, *pargs, nw1, g1, nw2, sh2, sc2, r_hi, r_lo)


def _sgu_kernel(x1p_ref, moe_ref, nw3p_ref, g2p_ref, nw0_ref, sh1_ref, sc1_ref, win_ref, lnw_ref, lnb_ref,
                ws_ref, bs_ref, wout_ref, nw1_ref, g1_ref, nw2_ref, sh2_ref, sc2_ref, rhi_ref, rlo_ref,
                x1_ref, h2_ref, lg_ref):
    tm, d = x1p_ref.shape[1], x1p_ref.shape[2]
    w = lnw_ref.shape[-1]
    gd = w // SGU_GROUPS
    nt = d // LANES

    def gelu(z):
        return 0.5 * z * (1.0 + jnp.tanh(math.sqrt(2.0 / math.pi) * (z + 0.044715 * (z * z * z))))

    for ck in range(tm // SGU_CHUNK):
        rs = slice(ck * SGU_CHUNK, (ck + 1) * SGU_CHUNK)
        moe = _load_tile_rows(moe_ref.at[0, pl.ds(ck * SGU_CHUNK * nt, SGU_CHUNK * nt)], SGU_CHUNK, d)
        x = x1p_ref[0, rs, :] + g2p_ref[0] * _rms(moe, nw3p_ref[...])
        h = _rms(x, nw0_ref[...]) * (1.0 + sc1_ref[0]) + sh1_ref[0]
        hb = h.astype(BF16)
        u = gelu(_dot(hb, win_ref[:, 0:w]))
        v = gelu(_dot(hb, win_ref[:, w:2 * w]))
        mu = jnp.mean(v, axis=-1, keepdims=True)
        vc = v - mu
        vn = vc * lax.rsqrt(jnp.mean(vc * vc, axis=-1, keepdims=True) + EPS) * lnw_ref[...] + lnb_ref[...]
        vnb = vn.astype(BF16)
        gated = []
        for g in range(SGU_GROUPS):
            cs = slice(g * gd, (g + 1) * gd)
            mixed = _dot(ws_ref[g], vnb[:, cs]) + bs_ref[:, g:g + 1]
            gated.append((u[:, cs] * mixed).astype(BF16))
        y = _dot(jnp.concatenate(gated, axis=1), wout_ref[...])
        _post_mixer(y, x, rs, nw1_ref, g1_ref, nw2_ref, sh2_ref, sc2_ref, rhi_ref, rlo_ref, x1_ref, h2_ref, lg_ref)


def _sgu(x1p, moe, nw3p, g2p, nw0, sh1, sc1, w_in_bf, ln_w, ln_b, w_s_bf, b_s_t, w_out_bf,
         nw1, g1, nw2, sh2, sc2, r_hi, r_lo, tm):
    bsz, t, d = x1p.shape
    w = ln_w.shape[-1]
    tok = pl.BlockSpec((1, tm, d), lambda b, i: (b, i, 0))
    tiles = _tile_rows_spec(tm, d, lambda b, i: (b, i, 0))
    vec = pl.BlockSpec((1, d), lambda b, i: (0, 0))
    wvec = pl.BlockSpec((1, w), lambda b, i: (0, 0))
    bvec = pl.BlockSpec((1, 1, d), lambda b, i: (b, 0, 0))

    def full(a):
        return pl.BlockSpec(a.shape, lambda b, i: (0,) * a.ndim)

    return pl.pallas_call(
        _sgu_kernel,
        grid=(bsz, t // tm),
        in_specs=[tok, tiles, vec, bvec, vec, bvec, bvec, full(w_in_bf), wvec, wvec, full(w_s_bf), full(b_s_t),
                  full(w_out_bf), vec, bvec, vec, bvec, bvec, full(r_hi), full(r_lo)],
        out_specs=[tok, tiles, pl.BlockSpec((1, tm, LANES), lambda b, i: (b, i, 0))],
        out_shape=[jax.ShapeDtypeStruct((bsz, t, d), F32), jax.ShapeDtypeStruct((bsz, t * (d // LANES), LANES), F32),
                   jax.ShapeDtypeStruct((bsz, t, LANES), F32)],
        compiler_params=_cparams(("arbitrary", "arbitrary")),
        name="sgu",
    )(x1p, moe, nw3p, g2p, nw0, sh1, sc1, w_in_bf, ln_w, ln_b, w_s_bf, b_s_t, w_out_bf,
      nw1, g1, nw2, sh2, sc2, r_hi, r_lo)


def _token_prefix(mask, triu, slow):
    local = _dot(mask.astype(BF16), triu)
    rowtot = jnp.broadcast_to(local[:, LANES - 1:LANES], local.shape)
    prev = _dot(slow, rowtot.astype(BF16))
    return local, prev, rowtot


def _route_kernel(lg_ref, triu_ref, slow_ref, idx_ref, gate_ref, a_scr, thr_scr, *, cap):
    ne, nr = a_scr.shape[0], a_scr.shape[1]
    for r in range(nr):
        a_scr[:, r, :] = lg_ref[0, r * LANES:(r + 1) * LANES, :].T[0:ne, :]
    lg = a_scr[...]
    ex = jnp.exp(lg - jnp.max(lg, axis=0, keepdims=True))
    a = ex / jnp.sum(ex, axis=0, keepdims=True)
    a_scr[...] = a
    capf = jnp.float32(cap)

    def count(m):
        return jnp.sum(jnp.sum(m.astype(F32), axis=1, keepdims=True), axis=2, keepdims=True)

    def unresolved(state):
        it, lo, hi = state
        mid = 0.5 * (lo + hi)
        return jnp.logical_and(it < BISECT_ITERS, jnp.max(((mid != lo) & (mid != hi)).astype(F32)) > 0.0)

    def bis(state):
        it, lo, hi = state
        mid = 0.5 * (lo + hi)
        ge = count(a >= mid) >= capf
        return it + 1, jnp.where(ge, mid, lo), jnp.where(ge, hi, mid)

    _, lo, _ = lax.while_loop(unresolved, bis, (jnp.int32(0), jnp.zeros((ne, 1, 1), F32), jnp.full((ne, 1, 1), 2.0, F32)))
    thr_scr[...] = jnp.broadcast_to(lo, thr_scr.shape)

    idx_ref[0] = jnp.zeros(idx_ref.shape[1:], I32)
    gate_ref[0] = jnp.zeros(gate_ref.shape[1:], F32)
    triu = triu_ref[...]
    slow = slow_ref[...]
    lane = lax.broadcasted_iota(I32, (LANES, LANES), 1)
    lane_f = lane.astype(F32)
    sub_f = lax.broadcasted_iota(I32, (LANES, LANES), 0).astype(F32)
    rowid = lax.broadcasted_iota(I32, (nr, LANES), 0).astype(F32)

    def per_expert(e, carry):
        ae = a_scr[e]
        v = thr_scr[e][0:1, :]
        gt = ae > v
        eq = ae == v
        need = capf - jnp.sum(jnp.sum(gt.astype(F32), axis=0, keepdims=True), axis=1, keepdims=True)
        eql, eqp, _ = _token_prefix(eq, triu, slow)
        sel = gt | (eq & ((eql + eqp - eq.astype(F32)) < need))
        local, prev, rowtot = _token_prefix(sel, triu, slow)
        rowcum = prev + rowtot
        prev_hi = jnp.floor(prev * (1.0 / LANES))
        a0, a1, a2 = _split3(ae)
        rhs = jnp.concatenate([piece.astype(BF16) for piece in
                               (local, prev_hi, prev - LANES * prev_hi, rowid, a0, a1, a2)], axis=1)

        for p in range(cap // LANES):
            base = float(p * LANES + 1)
            slot_row = base + lane_f[0:1, :]
            onehot_t = ((prev < slot_row) & (rowcum >= slot_row)).astype(BF16)
            g = _dot_tn(onehot_t, rhs)
            g_local = g[:, 0:LANES]
            g_prev = LANES * g[:, LANES:2 * LANES] + g[:, 2 * LANES:3 * LANES]
            g_row = g[:, 3 * LANES:4 * LANES]
            g_a = g[:, 4 * LANES:5 * LANES] + g[:, 5 * LANES:6 * LANES] + g[:, 6 * LANES:7 * LANES]
            slot_col = base + sub_f
            lstar = jnp.sum(((g_local + g_prev) < slot_col).astype(F32), axis=1, keepdims=True)
            tok = LANES * g_row[:, 0:1] + lstar
            gat = jnp.sum(jnp.where(lane_f == lstar, g_a, 0.0), axis=1, keepdims=True)
            rs = slice(p * LANES, (p + 1) * LANES)
            idx_ref[0, rs, :] = jnp.where(lane == e, tok.astype(I32), idx_ref[0, rs, :])
            gate_ref[0, rs, :] = jnp.where(lane == e, gat, gate_ref[0, rs, :])
        return carry

    lax.fori_loop(0, ne, per_expert, 0)


def _route(logits, ne, cap):
    bsz, t, _ = logits.shape
    nr = t // LANES
    assert cap % LANES == 0 and nr % SUBLANES == 0
    triu = (jnp.arange(LANES)[:, None] <= jnp.arange(LANES)[None, :]).astype(BF16)
    slow = (jnp.arange(nr)[:, None] > jnp.arange(nr)[None, :]).astype(BF16)
    idx, gate = pl.pallas_call(
        functools.partial(_route_kernel, cap=cap),
        grid=(bsz,),
        in_specs=[pl.BlockSpec((1, t, LANES), lambda b: (b, 0, 0)),
                  pl.BlockSpec((LANES, LANES), lambda b: (0, 0)), pl.BlockSpec((nr, nr), lambda b: (0, 0))],
        out_specs=[pl.BlockSpec((1, cap, LANES), lambda b: (b, 0, 0))] * 2,
        out_shape=[jax.ShapeDtypeStruct((bsz, cap, LANES), I32), jax.ShapeDtypeStruct((bsz, cap, LANES), F32)],
        scratch_shapes=[pltpu.VMEM((ne, nr, LANES), F32), pltpu.VMEM((ne, SUBLANES, LANES), F32)],
        compiler_params=_cparams(("arbitrary",)),
        name="route",
    )(logits, triu, slow)
    return idx, gate


ROW_UNROLL = 8


def _gather_kernel(src_ref, h_hbm, o_ref, h_scr, row_scr, sem, *, cap, ne, nt):
    b, e = pl.program_id(0), pl.program_id(1)

    @pl.when(e == 0)
    def _():
        cp = pltpu.make_async_copy(h_hbm.at[b], h_scr, sem.at[0])
        cp.start()
        cp.wait()

    base = (b * ne + e) * cap
    group = nt * SUBLANES

    def body(jj, carry):
        for u in range(SUBLANES):
            row_scr[pl.ds(jj * group + u, nt, stride=SUBLANES), :] = \
                h_scr[pl.ds(src_ref[base + jj * SUBLANES + u], nt, stride=SUBLANES), :]
        return carry

    lax.fori_loop(0, cap // SUBLANES, body, 0, unroll=2)

    def emit(g, carry):
        rows = [jnp.concatenate([row_scr[pl.ds((g * 2 + i) * group + c * SUBLANES, SUBLANES), :] for c in range(nt)],
                                axis=1) for i in range(2)]
        o_ref[0, 0, pl.ds(g * 2 * SUBLANES, 2 * SUBLANES), :] = jnp.concatenate(rows, axis=0).astype(BF16)
        return carry

    lax.fori_loop(0, cap // (2 * SUBLANES), emit, 0, unroll=4)


def _gather(src_flat, h2_rows, ne, cap, t):
    bsz, rows, _ = h2_rows.shape
    nt = rows // t
    return pl.pallas_call(
        functools.partial(_gather_kernel, cap=cap, ne=ne, nt=nt),
        grid_spec=pltpu.PrefetchScalarGridSpec(
            num_scalar_prefetch=1,
            grid=(bsz, ne),
            in_specs=[pl.BlockSpec(memory_space=pl.ANY)],
            out_specs=pl.BlockSpec((1, 1, cap, nt * LANES), lambda b, e, src: (b, e, 0, 0)),
            scratch_shapes=[pltpu.VMEM((rows, LANES), F32), pltpu.VMEM((cap * nt, LANES), F32),
                            pltpu.SemaphoreType.DMA((1,))],
        ),
        out_shape=jax.ShapeDtypeStruct((bsz, ne, cap, nt * LANES), BF16),
        compiler_params=_cparams(("arbitrary", "arbitrary")),
        name="gather",
    )(src_flat, h2_rows)


def _ffn_kernel(xs_ref, gate_ref, wg_ref, wu_ref, wd_ref, y_ref):
    e, f = pl.program_id(0), pl.program_id(1)
    cap, d = xs_ref.shape[2], xs_ref.shape[3]
    nt = d // LANES

    @pl.when(f == 0)
    def _():
        y_ref[...] = jnp.zeros(y_ref.shape, F32)

    wg = wg_ref[0, 0].astype(BF16)
    wu = wu_ref[0, 0].astype(BF16)
    wd = wd_ref[0, 0].astype(BF16)
    lane = lax.broadcasted_iota(I32, (cap, LANES), 1)
    for b in range(xs_ref.shape[0]):
        xb = xs_ref[b, 0]
        g = _dot(xb, wg)
        u = _dot(xb, wu)
        gate = jnp.sum(jnp.where(lane == e, gate_ref[b], 0.0), axis=1, keepdims=True)
        hid = (g * jax.nn.sigmoid(g) * u * gate).astype(BF16)
        part = _dot(hid, wd)
        for i in range(cap // SUBLANES):
            for c in range(nt):
                r0 = (i * nt + c) * SUBLANES
                y_ref[b, 0, r0:r0 + SUBLANES, :] += part[i * SUBLANES:(i + 1) * SUBLANES, c * LANES:(c + 1) * LANES]


def _ffn(xs, gate_cols, w_gate, w_up, w_down, layer, tf):
    bsz, ne, cap, d = xs.shape
    ff = w_gate.shape[-1]
    tf = min(tf, ff)
    nt = d // LANES
    return pl.pallas_call(
        _ffn_kernel,
        grid=(ne, ff // tf),
        in_specs=[pl.BlockSpec((bsz, 1, cap, d), lambda e, f: (0, e, 0, 0)),
                  pl.BlockSpec((bsz, cap, LANES), lambda e, f: (0, 0, 0)),
                  pl.BlockSpec((1, 1, d, tf), lambda e, f: (layer, e, 0, f)),
                  pl.BlockSpec((1, 1, d, tf), lambda e, f: (layer, e, 0, f)),
                  pl.BlockSpec((1, 1, tf, d), lambda e, f: (layer, e, f, 0))],
        out_specs=pl.BlockSpec((bsz, 1, cap * nt, LANES), lambda e, f: (0, e, 0, 0)),
        out_shape=jax.ShapeDtypeStruct((bsz, ne, cap * nt, LANES), F32),
        compiler_params=_cparams(("arbitrary", "arbitrary")),
        name="ffn",
    )(xs, gate_cols, w_gate, w_up, w_down)


SCATTER_SPLIT = 8


def _scatter_kernel(dst_ref, cut_ref, src_ref, y_ref, zero_hbm, o_hbm, *scr, cap, ne, nt):
    accs, sem = scr[:SCATTER_SPLIT], scr[SCATTER_SPLIT]
    b, e = pl.program_id(0), pl.program_id(1)

    @pl.when(e == 0)
    def _():
        fills = [pltpu.make_async_copy(zero_hbm, acc, sem.at[k]) for k, acc in enumerate(accs)]
        for cp in fills:
            cp.start()
        for cp in fills:
            cp.wait()

    base = (b * ne + e) * cap
    cbase = (b * ne + e) * (SCATTER_SPLIT + 1)
    starts = [cut_ref[cbase + k] for k in range(SCATTER_SPLIT)]
    counts = [cut_ref[cbase + k + 1] - starts[k] for k in range(SCATTER_SPLIT)]
    shortest = functools.reduce(jnp.minimum, counts)
    longest = functools.reduce(jnp.maximum, counts)

    def update(acc, j, dst, scale=None):
        src = src_ref[j]
        row = y_ref[0, 0, pl.ds(src, nt, stride=SUBLANES), :]
        acc[pl.ds(dst, nt, stride=SUBLANES), :] += row if scale is None else scale * row

    def common(i, carry):
        for k, acc in enumerate(accs):
            j = starts[k] + i
            update(acc, j, dst_ref[base + j])
        return carry

    def tail(i, carry):
        for k, acc in enumerate(accs):
            valid = i < counts[k]
            j = jnp.minimum(starts[k] + i, cap - 1)
            update(acc, j, jnp.where(valid, dst_ref[base + j], 0), jnp.where(valid, 1.0, 0.0))
        return carry

    lax.fori_loop(0, shortest, common, 0)
    lax.fori_loop(shortest, longest, tail, 0)

    @pl.when(e == ne - 1)
    def _():
        part = accs[0].shape[0]
        copies = [pltpu.make_async_copy(acc, o_hbm.at[b, pl.ds(k * part, part)], sem.at[k])
                  for k, acc in enumerate(accs)]
        for cp in copies:
            cp.start()
        for cp in copies:
            cp.wait()


def _scatter(dst_flat, cuts_flat, y_rows, t, nt):
    bsz, ne, rows, _ = y_rows.shape
    cap = rows // nt
    part_rows = (t // SCATTER_SPLIT) * nt
    zeros = jnp.zeros((part_rows, LANES), F32)
    return pl.pallas_call(
        functools.partial(_scatter_kernel, cap=cap, ne=ne, nt=nt),
        grid_spec=pltpu.PrefetchScalarGridSpec(
            num_scalar_prefetch=3,
            grid=(bsz, ne),
            in_specs=[pl.BlockSpec((1, 1, rows, LANES), lambda b, e, i, c, s: (b, e, 0, 0)),
                      pl.BlockSpec(memory_space=pl.ANY)],
            out_specs=pl.BlockSpec(memory_space=pl.ANY),
            scratch_shapes=[pltpu.VMEM((part_rows, LANES), F32)] * SCATTER_SPLIT
            + [pltpu.SemaphoreType.DMA((SCATTER_SPLIT,))],
        ),
        out_shape=jax.ShapeDtypeStruct((bsz, t * nt, LANES), F32),
        compiler_params=_cparams(("arbitrary", "arbitrary")),
        name="scatter",
    )(dst_flat, cuts_flat, _tile_row_base(jnp.arange(cap, dtype=I32), nt), y_rows, zeros)


def _moe(h2_rows, logits, ne, w_gate, w_up, w_down, layer, tf):
    bsz, t, _ = logits.shape
    nt = h2_rows.shape[1] // t
    cap = EC_CAPACITY_FACTOR * t // ne
    tpart = t // SCATTER_SPLIT
    idx_cols, gate_cols = _route(logits, ne, cap)
    idx = jnp.swapaxes(idx_cols[:, :, :ne], 1, 2)
    xs = _gather(_tile_row_base(idx, nt).reshape(-1), h2_rows, ne, cap, t)
    y = _ffn(xs, gate_cols, w_gate, w_up, w_down, layer, tf)
    edges = jnp.arange(SCATTER_SPLIT + 1, dtype=I32) * tpart
    cuts = jnp.sum(idx[..., None] < edges, axis=2).astype(I32)
    return _scatter(_tile_row_base(idx % tpart, nt).reshape(-1), cuts.reshape(-1), y, t, nt)


def _final_kernel(x_ref, moe_ref, nw_ref, g_ref, o_ref):
    x = x_ref[0]
    o_ref[0] = x + g_ref[0] * _rms(_load_tile_rows(moe_ref.at[0], x.shape[0], x.shape[1]), nw_ref[...])


def _final(x1, moe, nw3, g2, tm):
    bsz, t, d = x1.shape
    tok = pl.BlockSpec((1, tm, d), lambda b, i: (b, i, 0))
    return pl.pallas_call(
        _final_kernel,
        grid=(bsz, t // tm),
        in_specs=[tok, _tile_rows_spec(tm, d, lambda b, i: (b, i, 0)), pl.BlockSpec((1, d), lambda b, i: (0, 0)),
                  pl.BlockSpec((1, 1, d), lambda b, i: (b, 0, 0))],
        out_specs=tok,
        out_shape=jax.ShapeDtypeStruct((bsz, t, d), F32),
        compiler_params=_cparams(("arbitrary", "arbitrary")),
        name="final",
    )(x1, moe, nw3, g2)


def _sincos_tables(rows, dim):
    quarter = dim // 4
    freqs = jnp.exp(-math.log(10000.0) * jnp.arange(quarter, dtype=F32) / quarter)

    def emb1d(n):
        ang = jnp.arange(n, dtype=F32)[:, None] * freqs[None, :]
        return jnp.concatenate([jnp.sin(ang), jnp.cos(ang)], axis=-1)

    return emb1d(rows), emb1d(GRID_W)


TOKEN_TILE = 256
FFN_TILE = 512


def kernel(x, c, ctx, c_ctx, w_ada, b_ada, norm_w, hg_w_in, hg_lb, hg_gnorm, hg_w_out, sg_w_in, sg_ln_w, sg_ln_b,
           sg_w_s, sg_b_s, sg_w_out, moe_router, moe_w_gate, moe_w_up, moe_w_down):
    bsz, t, d = x.shape
    depth = w_ada.shape[0]
    assert depth == 2 and d == N_HEADS * HEAD_DIM and t % GLA_CHUNK == 0 and ctx.shape[1] % GLA_CHUNK == 0
    tm = min(TOKEN_TILE, t)

    cvec = jnp.zeros((SUBLANES, d), F32).at[:bsz].set(c).at[bsz].set(c_ctx)
    mod = _ada(cvec, w_ada, b_ada)

    def mods(layer, rows):
        m = mod[layer, rows].reshape(-1, N_ADA, 1, d)
        return [m[:, k] for k in range(N_ADA)]

    nw = norm_w.reshape(depth, 4, 1, d)
    ne = moe_router.shape[-1]
    router = jnp.zeros((depth, d, LANES), F32).at[:, :, :ne].set(moe_router)
    r_hi = router.astype(BF16)
    r_lo = (router - r_hi.astype(F32)).astype(BF16)

    sh1, sc1, g1, sh2, sc2, g2 = mods(0, slice(0, bsz))
    sh1c, sc1c = [jnp.broadcast_to(m, (bsz, 1, d)) for m in mods(0, slice(bsz, bsz + 1))[:2]]
    w_in_bf = hg_w_in[0].astype(BF16)
    pos = _sincos_tables(t // GRID_W, d)
    tri, masks = _gla_consts()

    cv, ckf, cgf, ckb, cgb, cq, _ = _hg_in(ctx, None, nw[0, 0], sh1c, sc1c, w_in_bf, hg_lb, 1, tm)
    s_zero = jnp.zeros((bsz, 2, N_HEADS, HEAD_DIM, HEAD_DIM), F32)
    _, _, s_ctx = _gla(ckf, cgf, ckb, cgb, cv, cq, s_zero, tri, masks)

    v, kf, gf, kb, gb, q, sg = _hg_in(x, pos, nw[0, 0], sh1, sc1, w_in_bf, hg_lb, 1, tm)
    o_f, o_b, _ = _gla(kf, gf, kb, gb, v, q, s_ctx, tri, masks)
    x1, h2, lg = _hg_out(o_f, o_b, sg, hg_gnorm[0:1], hg_w_out[0].astype(BF16), x, pos, nw[0, 1], g1, nw[0, 2],
                         sh2, sc2, r_hi[0], r_lo[0], tm)
    moe = _moe(h2, lg, ne, moe_w_gate, moe_w_up, moe_w_down, 0, FFN_TILE)

    sh1b, sc1b, g1b, sh2b, sc2b, g2b = mods(1, slice(0, bsz))
    x1b, h2b, lgb = _sgu(x1, moe, nw[0, 3], g2, nw[1, 0], sh1b, sc1b, sg_w_in[0].astype(BF16),
                         sg_ln_w[0:1], sg_ln_b[0:1], sg_w_s[0].astype(BF16), sg_b_s[0].T, sg_w_out[0].astype(BF16),
                         nw[1, 1], g1b, nw[1, 2], sh2b, sc2b, r_hi[1], r_lo[1], tm)
    moe_b = _moe(h2b, lgb, ne, moe_w_gate, moe_w_up, moe_w_down, 1, FFN_TILE)
    return _final(x1b, moe_b, nw[1, 3], g2b, tm)
```

```python
import functools
import math

import jax
import jax.numpy as jnp
from jax import lax
from jax.experimental import pallas as pl
from jax.experimental.pallas import tpu as pltpu

F32 = jnp.float32
BF16 = jnp.bfloat16
I32 = jnp.int32
HIGHEST = lax.Precision.HIGHEST

EPS = 1e-6
GRID_W = 64
N_ADA = 6
N_HEADS = 8
HEAD_DIM = 128
EC_CAPACITY_FACTOR = 2
SGU_CHUNK = 128
SGU_GROUPS = 8

LANES = 128
SUBLANES = 8
GLA_CHUNK = 128
GLA_LEVELS = (64, 32, 16, 8, 4, 2, 1)
GLA_SHORT_BLOCK = 32
GLA_SHORT_MAX_EXPONENT = 60.0
BISECT_ITERS = 160
VMEM_LIMIT = 52 * 1024 * 1024


def _cparams(sem):
    return pltpu.CompilerParams(dimension_semantics=sem, vmem_limit_bytes=VMEM_LIMIT)


def _rms(x, w):
    ms = jnp.mean(x * x, axis=-1, keepdims=True)
    return x * lax.rsqrt(ms + EPS) * w


def _dot(a, b):
    return jnp.dot(a, b, preferred_element_type=F32)


def _dot_nt(a, b):
    return lax.dot_general(a, b, (((1,), (1,)), ((), ())), preferred_element_type=F32)


def _dot_tn(a, b):
    return lax.dot_general(a, b, (((0,), (0,)), ((), ())), preferred_element_type=F32)


def _ada_kernel(c_ref, w_ref, b_ref, o_ref):
    c = c_ref[...]
    s = c * jax.nn.sigmoid(c)
    o_ref[0] = jnp.dot(s, w_ref[0], precision=HIGHEST, preferred_element_type=F32) + b_ref[0]


def _ada(cvec, w_ada, b_ada):
    depth, d, nd = w_ada.shape
    rows = cvec.shape[0]
    return pl.pallas_call(
        _ada_kernel,
        grid=(depth, nd // d),
        in_specs=[pl.BlockSpec((rows, d), lambda l, n: (0, 0)),
                  pl.BlockSpec((1, d, d), lambda l, n: (l, 0, n)),
                  pl.BlockSpec((1, 1, d), lambda l, n: (l, 0, n))],
        out_specs=pl.BlockSpec((1, rows, d), lambda l, n: (l, 0, n)),
        out_shape=jax.ShapeDtypeStruct((depth, rows, nd), F32),
        compiler_params=_cparams(("arbitrary", "arbitrary")),
        name="ada",
    )(cvec, w_ada, b_ada.reshape(depth, 1, nd))


def _hgin_body(x, nw_ref, sh_ref, sc_ref, w_ref, lb_ref, outs, n_lb):
    v_ref, kf_ref, gf_ref, kb_ref, gb_ref, q_ref, sg_ref = outs
    d = x.shape[-1]
    h = _rms(x, nw_ref[...]) * (1.0 + sc_ref[0]) + sh_ref[0]
    hb = h.astype(BF16)
    lbs = lb_ref[...]
    e = jnp.exp(lbs - jnp.max(lbs, axis=0, keepdims=True))
    lb = jnp.sum(e[:n_lb], axis=0) / jnp.sum(e, axis=0)

    v_ref[0] = _dot(hb, w_ref[:, 0:d]).astype(BF16)
    for j, (k_ref, g_ref) in enumerate(((kf_ref, gf_ref), (kb_ref, gb_ref))):
        raw = _dot(hb, w_ref[:, (1 + j) * d:(2 + j) * d])
        lbj = lb[j:j + 1]
        sig = jax.nn.sigmoid(raw)
        f = lbj + (1.0 - lbj) * sig
        k_ref[0] = ((1.0 - lbj) * (1.0 - sig)).astype(BF16)
        g_ref[0] = jnp.log(f)
    qr = _dot(hb, w_ref[:, 3 * d:4 * d])
    q_ref[0] = (qr * jax.nn.sigmoid(qr)).astype(BF16)
    gr = _dot(hb, w_ref[:, 4 * d:5 * d])
    sg_ref[0] = (gr * jax.nn.sigmoid(gr)).astype(BF16)


def _with_pos(x, er_ref, ec_ref):
    er, ec = er_ref[0], ec_ref[...]
    pos = jnp.concatenate([jnp.concatenate([jnp.broadcast_to(er[r:r + 1], ec.shape), ec], axis=1)
                           for r in range(er.shape[0])], axis=0)
    return x + pos


def _pos_specs(pos, tm, d):
    er, ec = pos
    assert tm % GRID_W == 0
    rpt = tm // GRID_W
    specs = [pl.BlockSpec((1, rpt, d // 2), lambda b, i: (i, 0, 0)), pl.BlockSpec(ec.shape, lambda b, i: (0, 0))]
    return specs, (er.reshape(-1, rpt, d // 2), ec)


def _hgin_pos_kernel(x_ref, er_ref, ec_ref, nw_ref, sh_ref, sc_ref, w_ref, lb_ref, *outs, n_lb):
    _hgin_body(_with_pos(x_ref[0], er_ref, ec_ref), nw_ref, sh_ref, sc_ref, w_ref, lb_ref, outs, n_lb)


def _hgin_kernel(x_ref, nw_ref, sh_ref, sc_ref, w_ref, lb_ref, *outs, n_lb):
    _hgin_body(x_ref[0], nw_ref, sh_ref, sc_ref, w_ref, lb_ref, outs, n_lb)


def _hg_in(x, pos, nw, sh, sc, w_bf, lb_raw, n_lb, tm):
    bsz, t, d = x.shape
    tm = min(tm, t)
    tok = pl.BlockSpec((1, tm, d), lambda b, i: (b, i, 0))
    vec = pl.BlockSpec((1, d), lambda b, i: (0, 0))
    bvec = pl.BlockSpec((1, 1, d), lambda b, i: (b, 0, 0))
    wspec = pl.BlockSpec(w_bf.shape, lambda b, i: (0, 0))
    lbspec = pl.BlockSpec(lb_raw.shape, lambda b, i: (0, 0, 0))
    gate_shapes = [jax.ShapeDtypeStruct((bsz, t, d), dt) for dt in (BF16, BF16, F32, BF16, F32, BF16, BF16)]
    if pos is not None:
        kern = functools.partial(_hgin_pos_kernel, n_lb=n_lb)
        pspecs, pargs = _pos_specs(pos, tm, d)
        in_specs = [tok] + pspecs + [vec, bvec, bvec, wspec, lbspec]
        args = (x,) + pargs + (nw, sh, sc, w_bf, lb_raw)
        out_shape = gate_shapes
    else:
        kern = functools.partial(_hgin_kernel, n_lb=n_lb)
        in_specs = [tok, vec, bvec, bvec, wspec, lbspec]
        args = (x, nw, sh, sc, w_bf, lb_raw)
        out_shape = gate_shapes
    return pl.pallas_call(
        kern,
        grid=(bsz, t // tm),
        in_specs=in_specs,
        out_specs=[tok] * len(out_shape),
        out_shape=out_shape,
        compiler_params=_cparams(("arbitrary", "arbitrary")),
        name="hg_in",
    )(*args)


def _gla_consts():
    c = GLA_CHUNK
    t = jnp.arange(c)[:, None]
    s = jnp.arange(c)[None, :]
    tri = jnp.stack([(s <= t), (s >= t)]).astype(BF16)
    masks = []
    for rev in (False, True):
        lv = []
        for m in GLA_LEVELS:
            same = (t // (2 * m)) == (s // (2 * m))
            tq = ((t // m) % 2) == (0 if rev else 1)
            sk = ((s // m) % 2) == (1 if rev else 0)
            lv.append(same & tq & sk)
        lv.append(t == s)
        lv.append(((t // GLA_SHORT_BLOCK) == (s // GLA_SHORT_BLOCK)) & ((s >= t) if rev else (s <= t)))
        masks.append(jnp.stack(lv))
    return tri, jnp.stack(masks).astype(F32)


def _split3(x):
    p0 = x.astype(BF16)
    r1 = x - p0.astype(F32)
    p1 = r1.astype(BF16)
    p2 = (r1 - p1.astype(F32)).astype(BF16)
    return p0, p1, p2


def _level_ref(b_scr, hs, m, rev):
    c = GLA_CHUNK
    off = m if rev else m - 1

    def row8(i):
        return jnp.broadcast_to(b_scr[pl.ds(i, 1), hs], (SUBLANES, HEAD_DIM))

    pieces = []
    if m >= SUBLANES:
        for blk in range(c // (2 * m)):
            r8 = row8(blk * 2 * m + off)
            pieces.extend([r8] * (2 * m // SUBLANES))
    else:
        sub = lax.broadcasted_iota(I32, (SUBLANES, HEAD_DIM), 0) // (2 * m)
        for grp in range(c // SUBLANES):
            piece = row8(grp * SUBLANES + off)
            for cls in range(1, SUBLANES // (2 * m)):
                piece = jnp.where(sub == cls, row8(grp * SUBLANES + cls * 2 * m + off), piece)
            pieces.append(piece)
    return jnp.concatenate(pieces, axis=0)


def _block_decay_bound(b_scr, rev):
    c, blk = GLA_CHUNK, GLA_SHORT_BLOCK
    worst = None
    for i in range(c // blk):
        inner = b_scr[pl.ds(i * blk if rev else (i + 1) * blk - 1, 1), :]
        r = (i + 1) * blk if rev else i * blk - 1
        span = jnp.abs(inner - b_scr[pl.ds(r, 1), :]) if 0 <= r < c else jnp.abs(inner)
        worst = span if worst is None else jnp.maximum(worst, span)
    return jnp.max(worst)


def _gla_head(q_ref, k_ref, v_ref, o_ref, mask_ref, st_scr, b_scr, d_idx, rev, h, short):
    c = GLA_CHUNK
    n_lv = len(GLA_LEVELS)
    hs = pl.ds(pl.multiple_of(h * HEAD_DIM, HEAD_DIM), HEAD_DIM)
    b = b_scr[:, hs]
    q = q_ref[0, :, hs]
    k = k_ref[0, :, hs]
    vb = v_ref[0, :, hs]
    bl = b_scr[pl.ds(0 if rev else c - 1, 1), hs]
    st = st_scr[d_idx, h]

    if short:
        blk = GLA_SHORT_BLOCK
        nb = c // blk
        own = []
        for i in range(nb):
            r = (i + 1) * blk if rev else i * blk - 1
            own.append(b_scr[pl.ds(r, 1), hs] if 0 <= r < c else jnp.zeros((1, HEAD_DIM), F32))

        def blockwise(rows):
            return jnp.concatenate([jnp.broadcast_to(r, (blk, HEAD_DIM)) for r in rows], axis=0)

        u = blockwise(own) - b
        qh = q * jnp.exp(-u).astype(BF16)
        kh = k * jnp.exp(u).astype(BF16)
        qt = qh * blockwise([jnp.exp(r) for r in own]).astype(BF16)
        kt = kh * blockwise([jnp.exp(bl - r) for r in own]).astype(BF16)
        a = jnp.where(mask_ref[d_idx, n_lv + 1] > 0.0, _dot_nt(qh, kh), 0.0)
        zero = jnp.zeros((1, HEAD_DIM), F32)
        for li, m in enumerate(GLA_LEVELS):
            if m < blk:
                continue
            qf, kf = [], []
            for i in range(nb):
                lvl = b_scr[pl.ds((i * blk) // (2 * m) * 2 * m + (m if rev else m - 1), 1), hs]
                is_query = ((i * blk) // m) % 2 == (0 if rev else 1)
                qf.append(jnp.exp(own[i] - lvl) if is_query else zero)
                kf.append(zero if is_query else jnp.exp(lvl - own[i]))
            a = a + mask_ref[d_idx, li] * _dot_nt(qh * blockwise(qf).astype(BF16), kh * blockwise(kf).astype(BF16))
    else:
        qt = q * jnp.exp(b).astype(BF16)
        kt = k * jnp.exp(bl - b).astype(BF16)

        def level(m):
            e = jnp.exp(-jnp.abs(b - _level_ref(b_scr, hs, m, rev))).astype(BF16)
            return _dot_nt(q * e, k * e)

        a = mask_ref[d_idx, n_lv] * _dot_nt(q, k)
        for li, m in enumerate(GLA_LEVELS):
            a = a + mask_ref[d_idx, li] * level(m)
    o_ref[0, :, hs] = (_dot_nt(qt, st.astype(BF16)) + _dot(a.astype(BF16), vb)).astype(o_ref.dtype)
    st_scr[d_idx, h] = st * jnp.exp(bl) + _dot_tn(vb, kt)


def _gla_kernel(kf_ref, gf_ref, vf_ref, qf_ref, kb_ref, gb_ref, vb_ref, qb_ref, s0_ref, tri_ref, mask_ref,
                of_ref, ob_ref, sout_ref, st_scr, bf_scr, bb_scr):
    n = pl.program_id(1)

    @pl.when(n == 0)
    def _():
        st_scr[...] = s0_ref[0]

    dirs = ((qf_ref, kf_ref, vf_ref, gf_ref, of_ref, bf_scr, 0, False),
            (qb_ref, kb_ref, vb_ref, gb_ref, ob_ref, bb_scr, 1, True))
    bounded = []
    for q_ref, k_ref, v_ref, g_ref, o_ref, b_scr, d_idx, rev in dirs:
        p0, p1, p2 = _split3(g_ref[0])
        tri = tri_ref[d_idx]
        b_scr[...] = _dot(tri, p0) + _dot(tri, p1) + _dot(tri, p2)
        bounded.append(_block_decay_bound(b_scr, rev) <= GLA_SHORT_MAX_EXPONENT)

    def heads(which, short, unroll):
        def body(h, carry):
            for q_ref, k_ref, v_ref, _, o_ref, b_scr, d_idx, rev in which:
                _gla_head(q_ref, k_ref, v_ref, o_ref, mask_ref, st_scr, b_scr, d_idx, rev, h, short)
            return carry
        lax.fori_loop(0, N_HEADS, body, 0, unroll=unroll)

    both = jnp.logical_and(bounded[0], bounded[1])

    @pl.when(both)
    def _():
        heads(dirs, True, 8)

    for d, ok in zip(dirs, bounded):
        @pl.when(jnp.logical_and(jnp.logical_not(both), ok))
        def _():
            heads((d,), True, 2)

        @pl.when(jnp.logical_not(ok))
        def _():
            heads((d,), False, 2)

    @pl.when(n == pl.num_programs(1) - 1)
    def _():
        sout_ref[0] = st_scr[...]


def _gla(kf, gf, kb, gb, v, q, s0, tri, masks):
    bsz, t, d = v.shape
    c = GLA_CHUNK
    n = t // c
    fwd = pl.BlockSpec((1, c, d), lambda b, i: (b, i, 0))
    bwd = pl.BlockSpec((1, c, d), lambda b, i: (b, n - 1 - i, 0))
    sspec = pl.BlockSpec((1,) + s0.shape[1:], lambda b, i: (b, 0, 0, 0, 0))
    return pl.pallas_call(
        _gla_kernel,
        grid=(bsz, n),
        in_specs=[fwd, fwd, fwd, fwd, bwd, bwd, bwd, bwd, sspec,
                  pl.BlockSpec(tri.shape, lambda b, i: (0, 0, 0)),
                  pl.BlockSpec(masks.shape, lambda b, i: (0, 0, 0, 0))],
        out_specs=[fwd, bwd, sspec],
        out_shape=[jax.ShapeDtypeStruct((bsz, t, d), BF16), jax.ShapeDtypeStruct((bsz, t, d), BF16),
                   jax.ShapeDtypeStruct(s0.shape, F32)],
        scratch_shapes=[pltpu.VMEM(s0.shape[1:], F32), pltpu.VMEM((c, d), F32), pltpu.VMEM((c, d), F32)],
        compiler_params=_cparams(("arbitrary", "arbitrary")),
        name="gla",
    )(kf, gf, v, q, kb, gb, v, q, s0, tri, masks)


def _store_tile_rows(ref2, x):
    n, d = x.shape
    nt = d // LANES
    for i in range(n // SUBLANES):
        for c in range(nt):
            r0 = (i * nt + c) * SUBLANES
            ref2[r0:r0 + SUBLANES, :] = x[i * SUBLANES:(i + 1) * SUBLANES, c * LANES:(c + 1) * LANES]


def _load_tile_rows(ref2, n, d):
    nt = d // LANES
    rows = []
    for i in range(n // SUBLANES):
        rows.append(jnp.concatenate(
            [ref2[(i * nt + c) * SUBLANES:(i * nt + c + 1) * SUBLANES, :] for c in range(nt)], axis=1))
    return jnp.concatenate(rows, axis=0)


def _tile_row_base(t, nt):
    return (t >> 3) * (nt * SUBLANES) + (t & (SUBLANES - 1))


def _tile_rows_spec(tm, d, index_map):
    return pl.BlockSpec((1, tm * (d // LANES), LANES), index_map)


def _post_mixer(y, xres, rows, nw1_ref, g1_ref, nw2_ref, sh2_ref, sc2_ref, rhi_ref, rlo_ref, x1_ref, h2_ref, lg_ref):
    d = y.shape[-1]
    nt = d // LANES
    x1 = xres + g1_ref[0] * _rms(y, nw1_ref[...])
    x1_ref[0, rows, :] = x1
    h2 = _rms(x1, nw2_ref[...]) * (1.0 + sc2_ref[0]) + sh2_ref[0]
    _store_tile_rows(h2_ref.at[0, pl.ds(rows.start * nt, (rows.stop - rows.start) * nt)], h2)
    h_hi = h2.astype(BF16)
    h_lo = (h2 - h_hi.astype(F32)).astype(BF16)
    lg_ref[0, rows, :] = _dot(h_hi, rhi_ref[...]) + (_dot(h_lo, rhi_ref[...]) + _dot(h_hi, rlo_ref[...]))


def _hgout_kernel(of_ref, ob_ref, sg_ref, gn_ref, w_ref, x_ref, er_ref, ec_ref, nw1_ref, g1_ref, nw2_ref, sh2_ref,
                  sc2_ref, rhi_ref, rlo_ref, x1_ref, h2_ref, lg_ref, z_scr):
    o = of_ref[0].astype(F32) + ob_ref[0].astype(F32)
    for h in range(N_HEADS):
        hs = slice(h * HEAD_DIM, (h + 1) * HEAD_DIM)
        oh = o[:, hs]
        ms = jnp.mean(oh * oh, axis=-1, keepdims=True)
        z = oh * lax.rsqrt(ms + EPS) * gn_ref[:, hs] * sg_ref[0, :, hs].astype(F32)
        z_scr[:, hs] = z.astype(BF16)
    y = _dot(z_scr[...], w_ref[...])
    _post_mixer(y, _with_pos(x_ref[0], er_ref, ec_ref), slice(0, y.shape[0]), nw1_ref, g1_ref, nw2_ref, sh2_ref,
                sc2_ref, rhi_ref, rlo_ref, x1_ref, h2_ref, lg_ref)


def _hg_out(o_f, o_b, sg, gnorm, w_out_bf, x, pos, nw1, g1, nw2, sh2, sc2, r_hi, r_lo, tm):
    bsz, t, d = x.shape
    rspec = pl.BlockSpec((d, LANES), lambda b, i: (0, 0))
    pspecs, pargs = _pos_specs(pos, tm, d)
    tok = pl.BlockSpec((1, tm, d), lambda b, i: (b, i, 0))
    vec = pl.BlockSpec((1, d), lambda b, i: (0, 0))
    bvec = pl.BlockSpec((1, 1, d), lambda b, i: (b, 0, 0))
    return pl.pallas_call(
        _hgout_kernel,
        grid=(bsz, t // tm),
        in_specs=[tok, tok, tok, vec, pl.BlockSpec((d, d), lambda b, i: (0, 0)), tok] + pspecs
        + [vec, bvec, vec, bvec, bvec, rspec, rspec],
        out_specs=[tok, _tile_rows_spec(tm, d, lambda b, i: (b, i, 0)),
                   pl.BlockSpec((1, tm, LANES), lambda b, i: (b, i, 0))],
        out_shape=[jax.ShapeDtypeStruct((bsz, t, d), F32), jax.ShapeDtypeStruct((bsz, t * (d // LANES), LANES), F32),
                   jax.ShapeDtypeStruct((bsz, t, LANES), F32)],
        scratch_shapes=[pltpu.VMEM((tm, d), BF16)],
        compiler_params=_cparams(("arbitrary", "arbitrary")),
        name="hg_out",
    )(o_f, o_b, sg, gnorm, w_out_bf, x, *pargs, nw1, g1, nw2, sh2, sc2, r_hi, r_lo)


def _sgu_kernel(x1p_ref, moe_ref, nw3p_ref, g2p_ref, nw0_ref, sh1_ref, sc1_ref, win_ref, lnw_ref, lnb_ref,
                ws_ref, bs_ref, wout_ref, nw1_ref, g1_ref, nw2_ref, sh2_ref, sc2_ref, rhi_ref, rlo_ref,
                x1_ref, h2_ref, lg_ref):
    tm, d = x1p_ref.shape[1], x1p_ref.shape[2]
    w = lnw_ref.shape[-1]
    gd = w // SGU_GROUPS
    nt = d // LANES

    def gelu(z):
        return 0.5 * z * (1.0 + jnp.tanh(math.sqrt(2.0 / math.pi) * (z + 0.044715 * (z * z * z))))

    for ck in range(tm // SGU_CHUNK):
        rs = slice(ck * SGU_CHUNK, (ck + 1) * SGU_CHUNK)
        moe = _load_tile_rows(moe_ref.at[0, pl.ds(ck * SGU_CHUNK * nt, SGU_CHUNK * nt)], SGU_CHUNK, d)
        x = x1p_ref[0, rs, :] + g2p_ref[0] * _rms(moe, nw3p_ref[...])
        h = _rms(x, nw0_ref[...]) * (1.0 + sc1_ref[0]) + sh1_ref[0]
        hb = h.astype(BF16)
        u = gelu(_dot(hb, win_ref[:, 0:w]))
        v = gelu(_dot(hb, win_ref[:, w:2 * w]))
        mu = jnp.mean(v, axis=-1, keepdims=True)
        vc = v - mu
        vn = vc * lax.rsqrt(jnp.mean(vc * vc, axis=-1, keepdims=True) + EPS) * lnw_ref[...] + lnb_ref[...]
        vnb = vn.astype(BF16)
        gated = []
        for g in range(SGU_GROUPS):
            cs = slice(g * gd, (g + 1) * gd)
            mixed = _dot(ws_ref[g], vnb[:, cs]) + bs_ref[:, g:g + 1]
            gated.append((u[:, cs] * mixed).astype(BF16))
        y = _dot(jnp.concatenate(gated, axis=1), wout_ref[...])
        _post_mixer(y, x, rs, nw1_ref, g1_ref, nw2_ref, sh2_ref, sc2_ref, rhi_ref, rlo_ref, x1_ref, h2_ref, lg_ref)


def _sgu(x1p, moe, nw3p, g2p, nw0, sh1, sc1, w_in_bf, ln_w, ln_b, w_s_bf, b_s_t, w_out_bf,
         nw1, g1, nw2, sh2, sc2, r_hi, r_lo, tm):
    bsz, t, d = x1p.shape
    w = ln_w.shape[-1]
    tok = pl.BlockSpec((1, tm, d), lambda b, i: (b, i, 0))
    tiles = _tile_rows_spec(tm, d, lambda b, i: (b, i, 0))
    vec = pl.BlockSpec((1, d), lambda b, i: (0, 0))
    wvec = pl.BlockSpec((1, w), lambda b, i: (0, 0))
    bvec = pl.BlockSpec((1, 1, d), lambda b, i: (b, 0, 0))

    def full(a):
        return pl.BlockSpec(a.shape, lambda b, i: (0,) * a.ndim)

    return pl.pallas_call(
        _sgu_kernel,
        grid=(bsz, t // tm),
        in_specs=[tok, tiles, vec, bvec, vec, bvec, bvec, full(w_in_bf), wvec, wvec, full(w_s_bf), full(b_s_t),
                  full(w_out_bf), vec, bvec, vec, bvec, bvec, full(r_hi), full(r_lo)],
        out_specs=[tok, tiles, pl.BlockSpec((1, tm, LANES), lambda b, i: (b, i, 0))],
        out_shape=[jax.ShapeDtypeStruct((bsz, t, d), F32), jax.ShapeDtypeStruct((bsz, t * (d // LANES), LANES), F32),
                   jax.ShapeDtypeStruct((bsz, t, LANES), F32)],
        compiler_params=_cparams(("arbitrary", "arbitrary")),
        name="sgu",
    )(x1p, moe, nw3p, g2p, nw0, sh1, sc1, w_in_bf, ln_w, ln_b, w_s_bf, b_s_t, w_out_bf,
      nw1, g1, nw2, sh2, sc2, r_hi, r_lo)


def _token_prefix(mask, triu, slow):
    local = _dot(mask.astype(BF16), triu)
    rowtot = jnp.broadcast_to(local[:, LANES - 1:LANES], local.shape)
    prev = _dot(slow, rowtot.astype(BF16))
    return local, prev, rowtot


def _route_kernel(lg_ref, triu_ref, slow_ref, idx_ref, gate_ref, a_scr, thr_scr, *, cap):
    ne, nr = a_scr.shape[0], a_scr.shape[1]
    for r in range(nr):
        a_scr[:, r, :] = lg_ref[0, r * LANES:(r + 1) * LANES, :].T[0:ne, :]
    lg = a_scr[...]
    ex = jnp.exp(lg - jnp.max(lg, axis=0, keepdims=True))
    a = ex / jnp.sum(ex, axis=0, keepdims=True)
    a_scr[...] = a
    capf = jnp.float32(cap)

    def count(m):
        return jnp.sum(jnp.sum(m.astype(F32), axis=1, keepdims=True), axis=2, keepdims=True)

    def unresolved(state):
        it, lo, hi = state
        mid = 0.5 * (lo + hi)
        return jnp.logical_and(it < BISECT_ITERS, jnp.max(((mid != lo) & (mid != hi)).astype(F32)) > 0.0)

    def bis(state):
        it, lo, hi = state
        mid = 0.5 * (lo + hi)
        ge = count(a >= mid) >= capf
        return it + 1, jnp.where(ge, mid, lo), jnp.where(ge, hi, mid)

    _, lo, _ = lax.while_loop(unresolved, bis, (jnp.int32(0), jnp.zeros((ne, 1, 1), F32), jnp.full((ne, 1, 1), 2.0, F32)))
    thr_scr[...] = jnp.broadcast_to(lo, thr_scr.shape)

    idx_ref[0] = jnp.zeros(idx_ref.shape[1:], I32)
    gate_ref[0] = jnp.zeros(gate_ref.shape[1:], F32)
    triu = triu_ref[...]
    slow = slow_ref[...]
    lane = lax.broadcasted_iota(I32, (LANES, LANES), 1)
    lane_f = lane.astype(F32)
    sub_f = lax.broadcasted_iota(I32, (LANES, LANES), 0).astype(F32)
    rowid = lax.broadcasted_iota(I32, (nr, LANES), 0).astype(F32)

    def per_expert(e, carry):
        ae = a_scr[e]
        v = thr_scr[e][0:1, :]
        gt = ae > v
        eq = ae == v
        need = capf - jnp.sum(jnp.sum(gt.astype(F32), axis=0, keepdims=True), axis=1, keepdims=True)
        eql, eqp, _ = _token_prefix(eq, triu, slow)
        sel = gt | (eq & ((eql + eqp - eq.astype(F32)) < need))
        local, prev, rowtot = _token_prefix(sel, triu, slow)
        rowcum = prev + rowtot
        prev_hi = jnp.floor(prev * (1.0 / LANES))
        a0, a1, a2 = _split3(ae)
        rhs = jnp.concatenate([piece.astype(BF16) for piece in
                               (local, prev_hi, prev - LANES * prev_hi, rowid, a0, a1, a2)], axis=1)

        for p in range(cap // LANES):
            base = float(p * LANES + 1)
            slot_row = base + lane_f[0:1, :]
            onehot_t = ((prev < slot_row) & (rowcum >= slot_row)).astype(BF16)
            g = _dot_tn(onehot_t, rhs)
            g_local = g[:, 0:LANES]
            g_prev = LANES * g[:, LANES:2 * LANES] + g[:, 2 * LANES:3 * LANES]
            g_row = g[:, 3 * LANES:4 * LANES]
            g_a = g[:, 4 * LANES:5 * LANES] + g[:, 5 * LANES:6 * LANES] + g[:, 6 * LANES:7 * LANES]
            slot_col = base + sub_f
            lstar = jnp.sum(((g_local + g_prev) < slot_col).astype(F32), axis=1, keepdims=True)
            tok = LANES * g_row[:, 0:1] + lstar
            gat = jnp.sum(jnp.where(lane_f == lstar, g_a, 0.0), axis=1, keepdims=True)
            rs = slice(p * LANES, (p + 1) * LANES)
            idx_ref[0, rs, :] = jnp.where(lane == e, tok.astype(I32), idx_ref[0, rs, :])
            gate_ref[0, rs, :] = jnp.where(lane == e, gat, gate_ref[0, rs, :])
        return carry

    lax.fori_loop(0, ne, per_expert, 0)


def _route(logits, ne, cap):
    bsz, t, _ = logits.shape
    nr = t // LANES
    assert cap % LANES == 0 and nr % SUBLANES == 0
    triu = (jnp.arange(LANES)[:, None] <= jnp.arange(LANES)[None, :]).astype(BF16)
    slow = (jnp.arange(nr)[:, None] > jnp.arange(nr)[None, :]).astype(BF16)
    idx, gate = pl.pallas_call(
        functools.partial(_route_kernel, cap=cap),
        grid=(bsz,),
        in_specs=[pl.BlockSpec((1, t, LANES), lambda b: (b, 0, 0)),
                  pl.BlockSpec((LANES, LANES), lambda b: (0, 0)), pl.BlockSpec((nr, nr), lambda b: (0, 0))],
        out_specs=[pl.BlockSpec((1, cap, LANES), lambda b: (b, 0, 0))] * 2,
        out_shape=[jax.ShapeDtypeStruct((bsz, cap, LANES), I32), jax.ShapeDtypeStruct((bsz, cap, LANES), F32)],
        scratch_shapes=[pltpu.VMEM((ne, nr, LANES), F32), pltpu.VMEM((ne, SUBLANES, LANES), F32)],
        compiler_params=_cparams(("arbitrary",)),
        name="route",
    )(logits, triu, slow)
    return idx, gate


def _gather_kernel(src_ref, h_hbm, o_ref, h_scr, row_scr, sem, *, cap, ne, nt):
    b, e = pl.program_id(0), pl.program_id(1)

    @pl.when(e == 0)
    def _():
        cp = pltpu.make_async_copy(h_hbm.at[b], h_scr, sem.at[0])
        cp.start()
        cp.wait()

    base = (b * ne + e) * cap
    group = nt * SUBLANES

    def body(jj, carry):
        for u in range(SUBLANES):
            row_scr[pl.ds(jj * group + u, nt, stride=SUBLANES), :] = \
                h_scr[pl.ds(src_ref[base + jj * SUBLANES + u], nt, stride=SUBLANES), :]
        return carry

    lax.fori_loop(0, cap // SUBLANES, body, 0, unroll=2)

    def emit(g, carry):
        rows = [jnp.concatenate([row_scr[pl.ds((g * 2 + i) * group + c * SUBLANES, SUBLANES), :] for c in range(nt)],
                                axis=1) for i in range(2)]
        o_ref[0, 0, pl.ds(g * 2 * SUBLANES, 2 * SUBLANES), :] = jnp.concatenate(rows, axis=0).astype(BF16)
        return carry

    lax.fori_loop(0, cap // (2 * SUBLANES), emit, 0, unroll=4)


def _gather(src_flat, h2_rows, ne, cap, t):
    bsz, rows, _ = h2_rows.shape
    nt = rows // t
    return pl.pallas_call(
        functools.partial(_gather_kernel, cap=cap, ne=ne, nt=nt),
        grid_spec=pltpu.PrefetchScalarGridSpec(
            num_scalar_prefetch=1,
            grid=(bsz, ne),
            in_specs=[pl.BlockSpec(memory_space=pl.ANY)],
            out_specs=pl.BlockSpec((1, 1, cap, nt * LANES), lambda b, e, src: (b, e, 0, 0)),
            scratch_shapes=[pltpu.VMEM((rows, LANES), F32), pltpu.VMEM((cap * nt, LANES), F32),
                            pltpu.SemaphoreType.DMA((1,))],
        ),
        out_shape=jax.ShapeDtypeStruct((bsz, ne, cap, nt * LANES), BF16),
        compiler_params=_cparams(("arbitrary", "arbitrary")),
        name="gather",
    )(src_flat, h2_rows)


def _ffn_kernel(xs_ref, gate_ref, wg_ref, wu_ref, wd_ref, y_ref):
    e, f = pl.program_id(0), pl.program_id(1)
    cap, d = xs_ref.shape[2], xs_ref.shape[3]
    nt = d // LANES

    @pl.when(f == 0)
    def _():
        y_ref[...] = jnp.zeros(y_ref.shape, F32)

    wg = wg_ref[0, 0].astype(BF16)
    wu = wu_ref[0, 0].astype(BF16)
    wd = wd_ref[0, 0].astype(BF16)
    lane = lax.broadcasted_iota(I32, (cap, LANES), 1)
    for b in range(xs_ref.shape[0]):
        xb = xs_ref[b, 0]
        g = _dot(xb, wg)
        u = _dot(xb, wu)
        gate = jnp.sum(jnp.where(lane == e, gate_ref[b], 0.0), axis=1, keepdims=True)
        hid = (g * jax.nn.sigmoid(g) * u * gate).astype(BF16)
        part = _dot(hid, wd)
        for i in range(cap // SUBLANES):
            for c in range(nt):
                r0 = (i * nt + c) * SUBLANES
                y_ref[b, 0, r0:r0 + SUBLANES, :] += part[i * SUBLANES:(i + 1) * SUBLANES, c * LANES:(c + 1) * LANES]


def _ffn(xs, gate_cols, w_gate, w_up, w_down, layer, tf):
    bsz, ne, cap, d = xs.shape
    ff = w_gate.shape[-1]
    tf = min(tf, ff)
    nt = d // LANES
    return pl.pallas_call(
        _ffn_kernel,
        grid=(ne, ff // tf),
        in_specs=[pl.BlockSpec((bsz, 1, cap, d), lambda e, f: (0, e, 0, 0)),
                  pl.BlockSpec((bsz, cap, LANES), lambda e, f: (0, 0, 0)),
                  pl.BlockSpec((1, 1, d, tf), lambda e, f: (layer, e, 0, f)),
                  pl.BlockSpec((1, 1, d, tf), lambda e, f: (layer, e, 0, f)),
                  pl.BlockSpec((1, 1, tf, d), lambda e, f: (layer, e, f, 0))],
        out_specs=pl.BlockSpec((bsz, 1, cap * nt, LANES), lambda e, f: (0, e, 0, 0)),
        out_shape=jax.ShapeDtypeStruct((bsz, ne, cap * nt, LANES), F32),
        compiler_params=_cparams(("arbitrary", "arbitrary")),
        name="ffn",
    )(xs, gate_cols, w_gate, w_up, w_down)


SCATTER_SPLIT = 8


RESIDUAL_TILE = 256


def _scatter_kernel(dst_ref, cut_ref, src_ref, y_ref, zero_hbm, *rest, cap, ne, nt, fused):
    if fused:
        x_hbm, nw_ref, g_ref, o_hbm = rest[:4]
        accs = rest[4:4 + SCATTER_SPLIT]
        sem, x_buf, o_buf, x_sem, o_sem = rest[4 + SCATTER_SPLIT:]
    else:
        o_hbm = rest[0]
        accs, sem = rest[1:1 + SCATTER_SPLIT], rest[1 + SCATTER_SPLIT]
    b, e = pl.program_id(0), pl.program_id(1)

    @pl.when(e == 0)
    def _():
        fills = [pltpu.make_async_copy(zero_hbm, acc, sem.at[k]) for k, acc in enumerate(accs)]
        for cp in fills:
            cp.start()
        for cp in fills:
            cp.wait()

    base = (b * ne + e) * cap
    cbase = (b * ne + e) * (SCATTER_SPLIT + 1)
    starts = [cut_ref[cbase + k] for k in range(SCATTER_SPLIT)]
    counts = [cut_ref[cbase + k + 1] - starts[k] for k in range(SCATTER_SPLIT)]
    shortest = functools.reduce(jnp.minimum, counts)
    longest = functools.reduce(jnp.maximum, counts)

    def update(acc, j, dst, scale=None):
        src = src_ref[j]
        row = y_ref[0, 0, pl.ds(src, nt, stride=SUBLANES), :]
        acc[pl.ds(dst, nt, stride=SUBLANES), :] += row if scale is None else scale * row

    def common(i, carry):
        for k, acc in enumerate(accs):
            j = starts[k] + i
            update(acc, j, dst_ref[base + j])
        return carry

    def tail(i, carry):
        for k, acc in enumerate(accs):
            valid = i < counts[k]
            j = jnp.minimum(starts[k] + i, cap - 1)
            update(acc, j, jnp.where(valid, dst_ref[base + j], 0), jnp.where(valid, 1.0, 0.0))
        return carry

    lax.fori_loop(0, shortest, common, 0)
    lax.fori_loop(shortest, longest, tail, 0)

    part = accs[0].shape[0]

    @pl.when(e == ne - 1)
    def _():
        if not fused:
            copies = [pltpu.make_async_copy(acc, o_hbm.at[b, pl.ds(k * part, part)], sem.at[k])
                      for k, acc in enumerate(accs)]
            for cp in copies:
                cp.start()
            for cp in copies:
                cp.wait()
            return
        tt = RESIDUAL_TILE
        d = nt * LANES
        per_acc = part // (tt * nt)
        n_tiles = SCATTER_SPLIT * per_acc

        def x_copy(i):
            return pltpu.make_async_copy(x_hbm.at[b, pl.ds(i * tt, tt)], x_buf.at[i % 2], x_sem.at[i % 2])

        def o_copy(i):
            return pltpu.make_async_copy(o_buf.at[i % 2], o_hbm.at[b, pl.ds(i * tt, tt)], o_sem.at[i % 2])

        x_copy(0).start()
        for i in range(n_tiles):
            x_copy(i).wait()
            if i + 1 < n_tiles:
                x_copy(i + 1).start()
            if i >= 2:
                o_copy(i - 2).wait()
            acc = accs[i // per_acc]
            moe = _load_tile_rows(acc.at[pl.ds((i % per_acc) * tt * nt, tt * nt)], tt, d)
            o_buf[i % 2] = x_buf[i % 2] + g_ref[0] * _rms(moe, nw_ref[...])
            o_copy(i).start()
        for i in range(max(n_tiles - 2, 0), n_tiles):
            o_copy(i).wait()


def _scatter(dst_flat, cuts_flat, y_rows, t, nt, residual=None):
    bsz, ne, rows, _ = y_rows.shape
    cap = rows // nt
    d = nt * LANES
    part_rows = (t // SCATTER_SPLIT) * nt
    zeros = jnp.zeros((part_rows, LANES), F32)
    fused = residual is not None
    in_specs = [pl.BlockSpec((1, 1, rows, LANES), lambda b, e, i, c, s: (b, e, 0, 0)),
                pl.BlockSpec(memory_space=pl.ANY)]
    scratch = [pltpu.VMEM((part_rows, LANES), F32)] * SCATTER_SPLIT + [pltpu.SemaphoreType.DMA((SCATTER_SPLIT,))]
    args = (dst_flat, cuts_flat, _tile_row_base(jnp.arange(cap, dtype=I32), nt), y_rows, zeros)
    if fused:
        assert (t // SCATTER_SPLIT) % RESIDUAL_TILE == 0
        in_specs += [pl.BlockSpec(memory_space=pl.ANY), pl.BlockSpec((1, d), lambda b, e, i, c, s: (0, 0)),
                     pl.BlockSpec((1, 1, d), lambda b, e, i, c, s: (b, 0, 0))]
        scratch += [pltpu.VMEM((2, RESIDUAL_TILE, d), F32), pltpu.VMEM((2, RESIDUAL_TILE, d), F32),
                    pltpu.SemaphoreType.DMA((2,)), pltpu.SemaphoreType.DMA((2,))]
        args += residual
        out_shape = jax.ShapeDtypeStruct((bsz, t, d), F32)
    else:
        out_shape = jax.ShapeDtypeStruct((bsz, t * nt, LANES), F32)
    return pl.pallas_call(
        functools.partial(_scatter_kernel, cap=cap, ne=ne, nt=nt, fused=fused),
        grid_spec=pltpu.PrefetchScalarGridSpec(
            num_scalar_prefetch=3,
            grid=(bsz, ne),
            in_specs=in_specs,
            out_specs=pl.BlockSpec(memory_space=pl.ANY),
            scratch_shapes=scratch,
        ),
        out_shape=out_shape,
        compiler_params=_cparams(("arbitrary", "arbitrary")),
        name="scatter",
    )(*args)


def _moe(h2_rows, logits, ne, w_gate, w_up, w_down, layer, tf, residual=None):
    bsz, t, _ = logits.shape
    nt = h2_rows.shape[1] // t
    cap = EC_CAPACITY_FACTOR * t // ne
    tpart = t // SCATTER_SPLIT
    idx_cols, gate_cols = _route(logits, ne, cap)
    idx = jnp.swapaxes(idx_cols[:, :, :ne], 1, 2)
    xs = _gather(_tile_row_base(idx, nt).reshape(-1), h2_rows, ne, cap, t)
    y = _ffn(xs, gate_cols, w_gate, w_up, w_down, layer, tf)
    edges = jnp.arange(SCATTER_SPLIT + 1, dtype=I32) * tpart
    cuts = jnp.sum(idx[..., None] < edges, axis=2).astype(I32)
    return _scatter(_tile_row_base(idx % tpart, nt).reshape(-1), cuts.reshape(-1), y, t, nt, residual)


def _sincos_tables(rows, dim):
    quarter = dim // 4
    freqs = jnp.exp(-math.log(10000.0) * jnp.arange(quarter, dtype=F32) / quarter)

    def emb1d(n):
        ang = jnp.arange(n, dtype=F32)[:, None] * freqs[None, :]
        return jnp.concatenate([jnp.sin(ang), jnp.cos(ang)], axis=-1)

    return emb1d(rows), emb1d(GRID_W)


TOKEN_TILE = 256
FFN_TILE = 512


def kernel(x, c, ctx, c_ctx, w_ada, b_ada, norm_w, hg_w_in, hg_lb, hg_gnorm, hg_w_out, sg_w_in, sg_ln_w, sg_ln_b,
           sg_w_s, sg_b_s, sg_w_out, moe_router, moe_w_gate, moe_w_up, moe_w_down):
    bsz, t, d = x.shape
    depth = w_ada.shape[0]
    assert depth == 2 and d == N_HEADS * HEAD_DIM and t % GLA_CHUNK == 0 and ctx.shape[1] % GLA_CHUNK == 0
    tm = min(TOKEN_TILE, t)

    cvec = jnp.zeros((SUBLANES, d), F32).at[:bsz].set(c).at[bsz].set(c_ctx)
    mod = _ada(cvec, w_ada, b_ada)

    def mods(layer, rows):
        m = mod[layer, rows].reshape(-1, N_ADA, 1, d)
        return [m[:, k] for k in range(N_ADA)]

    nw = norm_w.reshape(depth, 4, 1, d)
    ne = moe_router.shape[-1]
    router = jnp.zeros((depth, d, LANES), F32).at[:, :, :ne].set(moe_router)
    r_hi = router.astype(BF16)
    r_lo = (router - r_hi.astype(F32)).astype(BF16)

    sh1, sc1, g1, sh2, sc2, g2 = mods(0, slice(0, bsz))
    sh1c, sc1c = [jnp.broadcast_to(m, (bsz, 1, d)) for m in mods(0, slice(bsz, bsz + 1))[:2]]
    w_in_bf = hg_w_in[0].astype(BF16)
    pos = _sincos_tables(t // GRID_W, d)
    tri, masks = _gla_consts()

    cv, ckf, cgf, ckb, cgb, cq, _ = _hg_in(ctx, None, nw[0, 0], sh1c, sc1c, w_in_bf, hg_lb, 1, tm)
    s_zero = jnp.zeros((bsz, 2, N_HEADS, HEAD_DIM, HEAD_DIM), F32)
    _, _, s_ctx = _gla(ckf, cgf, ckb, cgb, cv, cq, s_zero, tri, masks)

    v, kf, gf, kb, gb, q, sg = _hg_in(x, pos, nw[0, 0], sh1, sc1, w_in_bf, hg_lb, 1, tm)
    o_f, o_b, _ = _gla(kf, gf, kb, gb, v, q, s_ctx, tri, masks)
    x1, h2, lg = _hg_out(o_f, o_b, sg, hg_gnorm[0:1], hg_w_out[0].astype(BF16), x, pos, nw[0, 1], g1, nw[0, 2],
                         sh2, sc2, r_hi[0], r_lo[0], tm)
    moe = _moe(h2, lg, ne, moe_w_gate, moe_w_up, moe_w_down, 0, FFN_TILE)

    sh1b, sc1b, g1b, sh2b, sc2b, g2b = mods(1, slice(0, bsz))
    x1b, h2b, lgb = _sgu(x1, moe, nw[0, 3], g2, nw[1, 0], sh1b, sc1b, sg_w_in[0].astype(BF16),
                         sg_ln_w[0:1], sg_ln_b[0:1], sg_w_s[0].astype(BF16), sg_b_s[0].T, sg_w_out[0].astype(BF16),
                         nw[1, 1], g1b, nw[1, 2], sh2b, sc2b, r_hi[1], r_lo[1], tm)
    return _moe(h2b, lgb, ne, moe_w_gate, moe_w_up, moe_w_down, 1, FFN_TILE, residual=(x1b, nw[1, 3], g2b))
```

```python
import functools
import math

import jax
import jax.numpy as jnp
from jax import lax
from jax.experimental import pallas as pl
from jax.experimental.pallas import tpu as pltpu

F32 = jnp.float32
BF16 = jnp.bfloat16
I32 = jnp.int32
HIGHEST = lax.Precision.HIGHEST

EPS = 1e-6
GRID_W = 64
N_ADA = 6
N_HEADS = 8
HEAD_DIM = 128
EC_CAPACITY_FACTOR = 2
SGU_CHUNK = 128
SGU_GROUPS = 8

LANES = 128
SUBLANES = 8
GLA_CHUNK = 128
GLA_LEVELS = (64, 32, 16, 8, 4, 2, 1)
GLA_SHORT_BLOCK = 32
GLA_SHORT_MAX_EXPONENT = 60.0
BISECT_ITERS = 160
VMEM_LIMIT = 52 * 1024 * 1024


def _cparams(sem):
    return pltpu.CompilerParams(dimension_semantics=sem, vmem_limit_bytes=VMEM_LIMIT)


def _rms(x, w):
    ms = jnp.mean(x * x, axis=-1, keepdims=True)
    return x * lax.rsqrt(ms + EPS) * w


def _dot(a, b):
    return jnp.dot(a, b, preferred_element_type=F32)


def _dot_nt(a, b):
    return lax.dot_general(a, b, (((1,), (1,)), ((), ())), preferred_element_type=F32)


def _dot_tn(a, b):
    return lax.dot_general(a, b, (((0,), (0,)), ((), ())), preferred_element_type=F32)


def _ada_kernel(c_ref, w_ref, b_ref, o_ref):
    c = c_ref[...]
    s = c * jax.nn.sigmoid(c)
    o_ref[0] = jnp.dot(s, w_ref[0], precision=HIGHEST, preferred_element_type=F32) + b_ref[0]


def _ada(cvec, w_ada, b_ada):
    depth, d, nd = w_ada.shape
    rows = cvec.shape[0]
    return pl.pallas_call(
        _ada_kernel,
        grid=(depth, nd // d),
        in_specs=[pl.BlockSpec((rows, d), lambda l, n: (0, 0)),
                  pl.BlockSpec((1, d, d), lambda l, n: (l, 0, n)),
                  pl.BlockSpec((1, 1, d), lambda l, n: (l, 0, n))],
        out_specs=pl.BlockSpec((1, rows, d), lambda l, n: (l, 0, n)),
        out_shape=jax.ShapeDtypeStruct((depth, rows, nd), F32),
        compiler_params=_cparams(("arbitrary", "arbitrary")),
        name="ada",
    )(cvec, w_ada, b_ada.reshape(depth, 1, nd))


def _hgin_body(x, nw_ref, sh_ref, sc_ref, w_ref, lb_ref, outs, n_lb):
    v_ref, kf_ref, gf_ref, kb_ref, gb_ref, q_ref, sg_ref = outs
    d = x.shape[-1]
    h = _rms(x, nw_ref[...]) * (1.0 + sc_ref[0]) + sh_ref[0]
    hb = h.astype(BF16)
    lbs = lb_ref[...]
    e = jnp.exp(lbs - jnp.max(lbs, axis=0, keepdims=True))
    lb = jnp.sum(e[:n_lb], axis=0) / jnp.sum(e, axis=0)

    v_ref[0] = _dot(hb, w_ref[:, 0:d]).astype(BF16)
    for j, (k_ref, g_ref) in enumerate(((kf_ref, gf_ref), (kb_ref, gb_ref))):
        raw = _dot(hb, w_ref[:, (1 + j) * d:(2 + j) * d])
        lbj = lb[j:j + 1]
        sig = jax.nn.sigmoid(raw)
        f = lbj + (1.0 - lbj) * sig
        k_ref[0] = ((1.0 - lbj) * (1.0 - sig)).astype(BF16)
        g_ref[0] = jnp.log(f)
    qr = _dot(hb, w_ref[:, 3 * d:4 * d])
    q_ref[0] = (qr * jax.nn.sigmoid(qr)).astype(BF16)
    gr = _dot(hb, w_ref[:, 4 * d:5 * d])
    sg_ref[0] = (gr * jax.nn.sigmoid(gr)).astype(BF16)


def _with_pos(x, er_ref, ec_ref):
    er, ec = er_ref[0], ec_ref[...]
    pos = jnp.concatenate([jnp.concatenate([jnp.broadcast_to(er[r:r + 1], ec.shape), ec], axis=1)
                           for r in range(er.shape[0])], axis=0)
    return x + pos


def _pos_specs(pos, tm, d):
    er, ec = pos
    assert tm % GRID_W == 0
    rpt = tm // GRID_W
    specs = [pl.BlockSpec((1, rpt, d // 2), lambda b, i: (i, 0, 0)), pl.BlockSpec(ec.shape, lambda b, i: (0, 0))]
    return specs, (er.reshape(-1, rpt, d // 2), ec)


def _hgin_pos_kernel(x_ref, er_ref, ec_ref, nw_ref, sh_ref, sc_ref, w_ref, lb_ref, *outs, n_lb):
    _hgin_body(_with_pos(x_ref[0], er_ref, ec_ref), nw_ref, sh_ref, sc_ref, w_ref, lb_ref, outs, n_lb)


def _hgin_kernel(x_ref, nw_ref, sh_ref, sc_ref, w_ref, lb_ref, *outs, n_lb):
    _hgin_body(x_ref[0], nw_ref, sh_ref, sc_ref, w_ref, lb_ref, outs, n_lb)


def _hg_in(x, pos, nw, sh, sc, w_bf, lb_raw, n_lb, tm):
    bsz, t, d = x.shape
    tm = min(tm, t)
    tok = pl.BlockSpec((1, tm, d), lambda b, i: (b, i, 0))
    vec = pl.BlockSpec((1, d), lambda b, i: (0, 0))
    bvec = pl.BlockSpec((1, 1, d), lambda b, i: (b, 0, 0))
    wspec = pl.BlockSpec(w_bf.shape, lambda b, i: (0, 0))
    lbspec = pl.BlockSpec(lb_raw.shape, lambda b, i: (0, 0, 0))
    gate_shapes = [jax.ShapeDtypeStruct((bsz, t, d), dt) for dt in (BF16, BF16, F32, BF16, F32, BF16, BF16)]
    if pos is not None:
        kern = functools.partial(_hgin_pos_kernel, n_lb=n_lb)
        pspecs, pargs = _pos_specs(pos, tm, d)
        in_specs = [tok] + pspecs + [vec, bvec, bvec, wspec, lbspec]
        args = (x,) + pargs + (nw, sh, sc, w_bf, lb_raw)
        out_shape = gate_shapes
    else:
        kern = functools.partial(_hgin_kernel, n_lb=n_lb)
        in_specs = [tok, vec, bvec, bvec, wspec, lbspec]
        args = (x, nw, sh, sc, w_bf, lb_raw)
        out_shape = gate_shapes
    return pl.pallas_call(
        kern,
        grid=(bsz, t // tm),
        in_specs=in_specs,
        out_specs=[tok] * len(out_shape),
        out_shape=out_shape,
        compiler_params=_cparams(("arbitrary", "arbitrary")),
        name="hg_in",
    )(*args)


def _gla_consts():
    c = GLA_CHUNK
    t = jnp.arange(c)[:, None]
    s = jnp.arange(c)[None, :]
    tri = jnp.stack([(s <= t), (s >= t)]).astype(BF16)
    masks = []
    for rev in (False, True):
        lv = []
        for m in GLA_LEVELS:
            same = (t // (2 * m)) == (s // (2 * m))
            tq = ((t // m) % 2) == (0 if rev else 1)
            sk = ((s // m) % 2) == (1 if rev else 0)
            lv.append(same & tq & sk)
        lv.append(t == s)
        lv.append(((t // GLA_SHORT_BLOCK) == (s // GLA_SHORT_BLOCK)) & ((s >= t) if rev else (s <= t)))
        masks.append(jnp.stack(lv))
    return tri, jnp.stack(masks).astype(F32)


def _split3(x):
    p0 = x.astype(BF16)
    r1 = x - p0.astype(F32)
    p1 = r1.astype(BF16)
    p2 = (r1 - p1.astype(F32)).astype(BF16)
    return p0, p1, p2


def _level_ref(b_scr, hs, m, rev):
    c = GLA_CHUNK
    off = m if rev else m - 1

    def row8(i):
        return jnp.broadcast_to(b_scr[pl.ds(i, 1), hs], (SUBLANES, HEAD_DIM))

    pieces = []
    if m >= SUBLANES:
        for blk in range(c // (2 * m)):
            r8 = row8(blk * 2 * m + off)
            pieces.extend([r8] * (2 * m // SUBLANES))
    else:
        sub = lax.broadcasted_iota(I32, (SUBLANES, HEAD_DIM), 0) // (2 * m)
        for grp in range(c // SUBLANES):
            piece = row8(grp * SUBLANES + off)
            for cls in range(1, SUBLANES // (2 * m)):
                piece = jnp.where(sub == cls, row8(grp * SUBLANES + cls * 2 * m + off), piece)
            pieces.append(piece)
    return jnp.concatenate(pieces, axis=0)


def _block_decay_bound(b_scr, rev):
    c, blk = GLA_CHUNK, GLA_SHORT_BLOCK
    worst = None
    for i in range(c // blk):
        inner = b_scr[pl.ds(i * blk if rev else (i + 1) * blk - 1, 1), :]
        r = (i + 1) * blk if rev else i * blk - 1
        span = jnp.abs(inner - b_scr[pl.ds(r, 1), :]) if 0 <= r < c else jnp.abs(inner)
        worst = span if worst is None else jnp.maximum(worst, span)
    return jnp.max(worst)


def _gla_head(q_ref, k_ref, v_ref, o_ref, mask_ref, st_scr, b_scr, d_idx, rev, h, short):
    c = GLA_CHUNK
    n_lv = len(GLA_LEVELS)
    hs = pl.ds(pl.multiple_of(h * HEAD_DIM, HEAD_DIM), HEAD_DIM)
    b = b_scr[:, hs]
    q = q_ref[0, :, hs]
    k = k_ref[0, :, hs]
    vb = v_ref[0, :, hs]
    bl = b_scr[pl.ds(0 if rev else c - 1, 1), hs]
    st = st_scr[d_idx, h]

    if short:
        blk = GLA_SHORT_BLOCK
        nb = c // blk
        own = []
        for i in range(nb):
            r = (i + 1) * blk if rev else i * blk - 1
            own.append(b_scr[pl.ds(r, 1), hs] if 0 <= r < c else jnp.zeros((1, HEAD_DIM), F32))

        def blockwise(rows):
            return jnp.concatenate([jnp.broadcast_to(r, (blk, HEAD_DIM)) for r in rows], axis=0)

        u = blockwise(own) - b
        qh = q * jnp.exp(-u).astype(BF16)
        kh = k * jnp.exp(u).astype(BF16)
        qt = qh * blockwise([jnp.exp(r) for r in own]).astype(BF16)
        kt = kh * blockwise([jnp.exp(bl - r) for r in own]).astype(BF16)
        a = jnp.where(mask_ref[d_idx, n_lv + 1] > 0.0, _dot_nt(qh, kh), 0.0)
        zero = jnp.zeros((1, HEAD_DIM), F32)
        for li, m in enumerate(GLA_LEVELS):
            if m < blk:
                continue
            qf, kf = [], []
            for i in range(nb):
                lvl = b_scr[pl.ds((i * blk) // (2 * m) * 2 * m + (m if rev else m - 1), 1), hs]
                is_query = ((i * blk) // m) % 2 == (0 if rev else 1)
                qf.append(jnp.exp(own[i] - lvl) if is_query else zero)
                kf.append(zero if is_query else jnp.exp(lvl - own[i]))
            a = a + mask_ref[d_idx, li] * _dot_nt(qh * blockwise(qf).astype(BF16), kh * blockwise(kf).astype(BF16))
    else:
        qt = q * jnp.exp(b).astype(BF16)
        kt = k * jnp.exp(bl - b).astype(BF16)

        def level(m):
            e = jnp.exp(-jnp.abs(b - _level_ref(b_scr, hs, m, rev))).astype(BF16)
            return _dot_nt(q * e, k * e)

        a = mask_ref[d_idx, n_lv] * _dot_nt(q, k)
        for li, m in enumerate(GLA_LEVELS):
            a = a + mask_ref[d_idx, li] * level(m)
    o_ref[0, :, hs] = (_dot_nt(qt, st.astype(BF16)) + _dot(a.astype(BF16), vb)).astype(o_ref.dtype)
    st_scr[d_idx, h] = st * jnp.exp(bl) + _dot_tn(vb, kt)


def _gla_kernel(kf_ref, gf_ref, vf_ref, qf_ref, kb_ref, gb_ref, vb_ref, qb_ref, s0_ref, tri_ref, mask_ref,
                of_ref, ob_ref, sout_ref, st_scr, bf_scr, bb_scr):
    n = pl.program_id(1)

    @pl.when(n == 0)
    def _():
        st_scr[...] = s0_ref[0]

    dirs = ((qf_ref, kf_ref, vf_ref, gf_ref, of_ref, bf_scr, 0, False),
            (qb_ref, kb_ref, vb_ref, gb_ref, ob_ref, bb_scr, 1, True))
    bounded = []
    for q_ref, k_ref, v_ref, g_ref, o_ref, b_scr, d_idx, rev in dirs:
        p0, p1, p2 = _split3(g_ref[0])
        tri = tri_ref[d_idx]
        b_scr[...] = _dot(tri, p0) + _dot(tri, p1) + _dot(tri, p2)
        bounded.append(_block_decay_bound(b_scr, rev) <= GLA_SHORT_MAX_EXPONENT)

    def heads(which, short, unroll):
        def body(h, carry):
            for q_ref, k_ref, v_ref, _, o_ref, b_scr, d_idx, rev in which:
                _gla_head(q_ref, k_ref, v_ref, o_ref, mask_ref, st_scr, b_scr, d_idx, rev, h, short)
            return carry
        lax.fori_loop(0, N_HEADS, body, 0, unroll=unroll)

    both = jnp.logical_and(bounded[0], bounded[1])

    @pl.when(both)
    def _():
        heads(dirs, True, 8)

    for d, ok in zip(dirs, bounded):
        @pl.when(jnp.logical_and(jnp.logical_not(both), ok))
        def _():
            heads((d,), True, 2)

        @pl.when(jnp.logical_not(ok))
        def _():
            heads((d,), False, 2)

    @pl.when(n == pl.num_programs(1) - 1)
    def _():
        sout_ref[0] = st_scr[...]


def _gla(kf, gf, kb, gb, v, q, s0, tri, masks):
    bsz, t, d = v.shape
    c = GLA_CHUNK
    n = t // c
    fwd = pl.BlockSpec((1, c, d), lambda b, i: (b, i, 0))
    bwd = pl.BlockSpec((1, c, d), lambda b, i: (b, n - 1 - i, 0))
    sspec = pl.BlockSpec((1,) + s0.shape[1:], lambda b, i: (b, 0, 0, 0, 0))
    return pl.pallas_call(
        _gla_kernel,
        grid=(bsz, n),
        in_specs=[fwd, fwd, fwd, fwd, bwd, bwd, bwd, bwd, sspec,
                  pl.BlockSpec(tri.shape, lambda b, i: (0, 0, 0)),
                  pl.BlockSpec(masks.shape, lambda b, i: (0, 0, 0, 0))],
        out_specs=[fwd, bwd, sspec],
        out_shape=[jax.ShapeDtypeStruct((bsz, t, d), BF16), jax.ShapeDtypeStruct((bsz, t, d), BF16),
                   jax.ShapeDtypeStruct(s0.shape, F32)],
        scratch_shapes=[pltpu.VMEM(s0.shape[1:], F32), pltpu.VMEM((c, d), F32), pltpu.VMEM((c, d), F32)],
        compiler_params=_cparams(("arbitrary", "arbitrary")),
        name="gla",
    )(kf, gf, v, q, kb, gb, v, q, s0, tri, masks)


def _store_tile_rows(ref2, x):
    n, d = x.shape
    nt = d // LANES
    for i in range(n // SUBLANES):
        for c in range(nt):
            r0 = (i * nt + c) * SUBLANES
            ref2[r0:r0 + SUBLANES, :] = x[i * SUBLANES:(i + 1) * SUBLANES, c * LANES:(c + 1) * LANES]


def _load_tile_rows(ref2, n, d):
    nt = d // LANES
    rows = []
    for i in range(n // SUBLANES):
        rows.append(jnp.concatenate(
            [ref2[(i * nt + c) * SUBLANES:(i * nt + c + 1) * SUBLANES, :] for c in range(nt)], axis=1))
    return jnp.concatenate(rows, axis=0)


def _tile_row_base(t, nt):
    return (t >> 3) * (nt * SUBLANES) + (t & (SUBLANES - 1))


def _tile_rows_spec(tm, d, index_map):
    return pl.BlockSpec((1, tm * (d // LANES), LANES), index_map)


def _post_mixer(y, xres, rows, nw1_ref, g1_ref, nw2_ref, sh2_ref, sc2_ref, rhi_ref, rlo_ref, x1_ref, h2_ref, lg_ref):
    d = y.shape[-1]
    nt = d // LANES
    x1 = xres + g1_ref[0] * _rms(y, nw1_ref[...])
    x1_ref[0, rows, :] = x1
    h2 = _rms(x1, nw2_ref[...]) * (1.0 + sc2_ref[0]) + sh2_ref[0]
    _store_tile_rows(h2_ref.at[0, pl.ds(rows.start * nt, (rows.stop - rows.start) * nt)], h2)
    h_hi = h2.astype(BF16)
    h_lo = (h2 - h_hi.astype(F32)).astype(BF16)
    lg_ref[0, rows, :] = _dot(h_hi, rhi_ref[...]) + (_dot(h_lo, rhi_ref[...]) + _dot(h_hi, rlo_ref[...]))


def _hgout_kernel(of_ref, ob_ref, sg_ref, gn_ref, w_ref, x_ref, er_ref, ec_ref, nw1_ref, g1_ref, nw2_ref, sh2_ref,
                  sc2_ref, rhi_ref, rlo_ref, x1_ref, h2_ref, lg_ref, z_scr):
    o = of_ref[0].astype(F32) + ob_ref[0].astype(F32)
    for h in range(N_HEADS):
        hs = slice(h * HEAD_DIM, (h + 1) * HEAD_DIM)
        oh = o[:, hs]
        ms = jnp.mean(oh * oh, axis=-1, keepdims=True)
        z = oh * lax.rsqrt(ms + EPS) * gn_ref[:, hs] * sg_ref[0, :, hs].astype(F32)
        z_scr[:, hs] = z.astype(BF16)
    y = _dot(z_scr[...], w_ref[...])
    _post_mixer(y, _with_pos(x_ref[0], er_ref, ec_ref), slice(0, y.shape[0]), nw1_ref, g1_ref, nw2_ref, sh2_ref,
                sc2_ref, rhi_ref, rlo_ref, x1_ref, h2_ref, lg_ref)


def _hg_out(o_f, o_b, sg, gnorm, w_out_bf, x, pos, nw1, g1, nw2, sh2, sc2, r_hi, r_lo, tm):
    bsz, t, d = x.shape
    rspec = pl.BlockSpec((d, LANES), lambda b, i: (0, 0))
    pspecs, pargs = _pos_specs(pos, tm, d)
    tok = pl.BlockSpec((1, tm, d), lambda b, i: (b, i, 0))
    vec = pl.BlockSpec((1, d), lambda b, i: (0, 0))
    bvec = pl.BlockSpec((1, 1, d), lambda b, i: (b, 0, 0))
    return pl.pallas_call(
        _hgout_kernel,
        grid=(bsz, t // tm),
        in_specs=[tok, tok, tok, vec, pl.BlockSpec((d, d), lambda b, i: (0, 0)), tok] + pspecs
        + [vec, bvec, vec, bvec, bvec, rspec, rspec],
        out_specs=[tok, _tile_rows_spec(tm, d, lambda b, i: (b, i, 0)),
                   pl.BlockSpec((1, tm, LANES), lambda b, i: (b, i, 0))],
        out_shape=[jax.ShapeDtypeStruct((bsz, t, d), F32), jax.ShapeDtypeStruct((bsz, t * (d // LANES), LANES), F32),
                   jax.ShapeDtypeStruct((bsz, t, LANES), F32)],
        scratch_shapes=[pltpu.VMEM((tm, d), BF16)],
        compiler_params=_cparams(("arbitrary", "arbitrary")),
        name="hg_out",
    )(o_f, o_b, sg, gnorm, w_out_bf, x, *pargs, nw1, g1, nw2, sh2, sc2, r_hi, r_lo)


def _sgu_kernel(x1p_ref, moe_ref, nw3p_ref, g2p_ref, nw0_ref, sh1_ref, sc1_ref, win_ref, lnw_ref, lnb_ref,
                ws_ref, bs_ref, wout_ref, nw1_ref, g1_ref, nw2_ref, sh2_ref, sc2_ref, rhi_ref, rlo_ref,
                x1_ref, h2_ref, lg_ref):
    tm, d = x1p_ref.shape[1], x1p_ref.shape[2]
    w = lnw_ref.shape[-1]
    gd = w // SGU_GROUPS
    nt = d // LANES

    def gelu(z):
        return 0.5 * z * (1.0 + jnp.tanh(math.sqrt(2.0 / math.pi) * (z + 0.044715 * (z * z * z))))

    for ck in range(tm // SGU_CHUNK):
        rs = slice(ck * SGU_CHUNK, (ck + 1) * SGU_CHUNK)
        moe = _load_tile_rows(moe_ref.at[0, pl.ds(ck * SGU_CHUNK * nt, SGU_CHUNK * nt)], SGU_CHUNK, d)
        x = x1p_ref[0, rs, :] + g2p_ref[0] * _rms(moe, nw3p_ref[...])
        h = _rms(x, nw0_ref[...]) * (1.0 + sc1_ref[0]) + sh1_ref[0]
        hb = h.astype(BF16)
        u = gelu(_dot(hb, win_ref[:, 0:w]))
        v = gelu(_dot(hb, win_ref[:, w:2 * w]))
        mu = jnp.mean(v, axis=-1, keepdims=True)
        vc = v - mu
        vn = vc * lax.rsqrt(jnp.mean(vc * vc, axis=-1, keepdims=True) + EPS) * lnw_ref[...] + lnb_ref[...]
        vnb = vn.astype(BF16)
        gated = []
        for g in range(SGU_GROUPS):
            cs = slice(g * gd, (g + 1) * gd)
            mixed = _dot(ws_ref[g], vnb[:, cs]) + bs_ref[:, g:g + 1]
            gated.append((u[:, cs] * mixed).astype(BF16))
        y = _dot(jnp.concatenate(gated, axis=1), wout_ref[...])
        _post_mixer(y, x, rs, nw1_ref, g1_ref, nw2_ref, sh2_ref, sc2_ref, rhi_ref, rlo_ref, x1_ref, h2_ref, lg_ref)


def _sgu(x1p, moe, nw3p, g2p, nw0, sh1, sc1, w_in_bf, ln_w, ln_b, w_s_bf, b_s_t, w_out_bf,
         nw1, g1, nw2, sh2, sc2, r_hi, r_lo, tm):
    bsz, t, d = x1p.shape
    w = ln_w.shape[-1]
    tok = pl.BlockSpec((1, tm, d), lambda b, i: (b, i, 0))
    tiles = _tile_rows_spec(tm, d, lambda b, i: (b, i, 0))
    vec = pl.BlockSpec((1, d), lambda b, i: (0, 0))
    wvec = pl.BlockSpec((1, w), lambda b, i: (0, 0))
    bvec = pl.BlockSpec((1, 1, d), lambda b, i: (b, 0, 0))

    def full(a):
        return pl.BlockSpec(a.shape, lambda b, i: (0,) * a.ndim)

    return pl.pallas_call(
        _sgu_kernel,
        grid=(bsz, t // tm),
        in_specs=[tok, tiles, vec, bvec, vec, bvec, bvec, full(w_in_bf), wvec, wvec, full(w_s_bf), full(b_s_t),
                  full(w_out_bf), vec, bvec, vec, bvec, bvec, full(r_hi), full(r_lo)],
        out_specs=[tok, tiles, pl.BlockSpec((1, tm, LANES), lambda b, i: (b, i, 0))],
        out_shape=[jax.ShapeDtypeStruct((bsz, t, d), F32), jax.ShapeDtypeStruct((bsz, t * (d // LANES), LANES), F32),
                   jax.ShapeDtypeStruct((bsz, t, LANES), F32)],
        compiler_params=_cparams(("arbitrary", "arbitrary")),
        name="sgu",
    )(x1p, moe, nw3p, g2p, nw0, sh1, sc1, w_in_bf, ln_w, ln_b, w_s_bf, b_s_t, w_out_bf,
      nw1, g1, nw2, sh2, sc2, r_hi, r_lo)


def _token_prefix(mask, triu, slow):
    local = _dot(mask.astype(BF16), triu)
    rowtot = jnp.broadcast_to(local[:, LANES - 1:LANES], local.shape)
    prev = _dot(slow, rowtot.astype(BF16))
    return local, prev, rowtot


def _route_kernel(lg_ref, triu_ref, slow_ref, idx_ref, gate_ref, a_scr, thr_scr, *, cap):
    ne, nr = a_scr.shape[0], a_scr.shape[1]
    for r in range(nr):
        a_scr[:, r, :] = lg_ref[0, r * LANES:(r + 1) * LANES, :].T[0:ne, :]
    lg = a_scr[...]
    ex = jnp.exp(lg - jnp.max(lg, axis=0, keepdims=True))
    a = ex / jnp.sum(ex, axis=0, keepdims=True)
    a_scr[...] = a
    capf = jnp.float32(cap)

    def count(m):
        return jnp.sum(jnp.sum(m.astype(F32), axis=1, keepdims=True), axis=2, keepdims=True)

    def unresolved(state):
        it, lo, hi = state
        mid = 0.5 * (lo + hi)
        return jnp.logical_and(it < BISECT_ITERS, jnp.max(((mid != lo) & (mid != hi)).astype(F32)) > 0.0)

    def bis(state):
        it, lo, hi = state
        mid = 0.5 * (lo + hi)
        ge = count(a >= mid) >= capf
        return it + 1, jnp.where(ge, mid, lo), jnp.where(ge, hi, mid)

    _, lo, _ = lax.while_loop(unresolved, bis, (jnp.int32(0), jnp.zeros((ne, 1, 1), F32), jnp.full((ne, 1, 1), 2.0, F32)))
    thr_scr[...] = jnp.broadcast_to(lo, thr_scr.shape)

    idx_ref[0] = jnp.zeros(idx_ref.shape[1:], I32)
    gate_ref[0] = jnp.zeros(gate_ref.shape[1:], F32)
    triu = triu_ref[...]
    slow = slow_ref[...]
    lane = lax.broadcasted_iota(I32, (LANES, LANES), 1)
    lane_f = lane.astype(F32)
    sub_f = lax.broadcasted_iota(I32, (LANES, LANES), 0).astype(F32)
    rowid = lax.broadcasted_iota(I32, (nr, LANES), 0).astype(F32)

    def per_expert(e, carry):
        ae = a_scr[e]
        v = thr_scr[e][0:1, :]
        gt = ae > v
        eq = ae == v
        need = capf - jnp.sum(jnp.sum(gt.astype(F32), axis=0, keepdims=True), axis=1, keepdims=True)
        eql, eqp, _ = _token_prefix(eq, triu, slow)
        sel = gt | (eq & ((eql + eqp - eq.astype(F32)) < need))
        local, prev, rowtot = _token_prefix(sel, triu, slow)
        rowcum = prev + rowtot
        prev_hi = jnp.floor(prev * (1.0 / LANES))
        a0, a1, a2 = _split3(ae)
        rhs = jnp.concatenate([piece.astype(BF16) for piece in
                               (local, prev_hi, prev - LANES * prev_hi, rowid, a0, a1, a2)], axis=1)

        for p in range(cap // LANES):
            base = float(p * LANES + 1)
            slot_row = base + lane_f[0:1, :]
            onehot_t = ((prev < slot_row) & (rowcum >= slot_row)).astype(BF16)
            g = _dot_tn(onehot_t, rhs)
            g_local = g[:, 0:LANES]
            g_prev = LANES * g[:, LANES:2 * LANES] + g[:, 2 * LANES:3 * LANES]
            g_row = g[:, 3 * LANES:4 * LANES]
            g_a = g[:, 4 * LANES:5 * LANES] + g[:, 5 * LANES:6 * LANES] + g[:, 6 * LANES:7 * LANES]
            slot_col = base + sub_f
            lstar = jnp.sum(((g_local + g_prev) < slot_col).astype(F32), axis=1, keepdims=True)
            tok = LANES * g_row[:, 0:1] + lstar
            gat = jnp.sum(jnp.where(lane_f == lstar, g_a, 0.0), axis=1, keepdims=True)
            rs = slice(p * LANES, (p + 1) * LANES)
            idx_ref[0, rs, :] = jnp.where(lane == e, tok.astype(I32), idx_ref[0, rs, :])
            gate_ref[0, rs, :] = jnp.where(lane == e, gat, gate_ref[0, rs, :])
        return carry

    lax.fori_loop(0, ne, per_expert, 0)


def _route(logits, ne, cap):
    bsz, t, _ = logits.shape
    nr = t // LANES
    assert cap % LANES == 0 and nr % SUBLANES == 0
    triu = (jnp.arange(LANES)[:, None] <= jnp.arange(LANES)[None, :]).astype(BF16)
    slow = (jnp.arange(nr)[:, None] > jnp.arange(nr)[None, :]).astype(BF16)
    idx, gate = pl.pallas_call(
        functools.partial(_route_kernel, cap=cap),
        grid=(bsz,),
        in_specs=[pl.BlockSpec((1, t, LANES), lambda b: (b, 0, 0)),
                  pl.BlockSpec((LANES, LANES), lambda b: (0, 0)), pl.BlockSpec((nr, nr), lambda b: (0, 0))],
        out_specs=[pl.BlockSpec((1, cap, LANES), lambda b: (b, 0, 0))] * 2,
        out_shape=[jax.ShapeDtypeStruct((bsz, cap, LANES), I32), jax.ShapeDtypeStruct((bsz, cap, LANES), F32)],
        scratch_shapes=[pltpu.VMEM((ne, nr, LANES), F32), pltpu.VMEM((ne, SUBLANES, LANES), F32)],
        compiler_params=_cparams(("arbitrary",)),
        name="route",
    )(logits, triu, slow)
    return idx, gate


def _gather_kernel(src_ref, h_hbm, o_ref, h_scr, row_scr, sem, *, cap, ne, nt):
    b, e = pl.program_id(0), pl.program_id(1)

    @pl.when(e == 0)
    def _():
        cp = pltpu.make_async_copy(h_hbm.at[b], h_scr, sem.at[0])
        cp.start()
        cp.wait()

    base = (b * ne + e) * cap
    group = nt * SUBLANES

    def body(jj, carry):
        for u in range(SUBLANES):
            row_scr[pl.ds(jj * group + u, nt, stride=SUBLANES), :] = \
                h_scr[pl.ds(src_ref[base + jj * SUBLANES + u], nt, stride=SUBLANES), :]
        return carry

    lax.fori_loop(0, cap // SUBLANES, body, 0, unroll=2)

    def emit(g, carry):
        rows = [jnp.concatenate([row_scr[pl.ds((g * 2 + i) * group + c * SUBLANES, SUBLANES), :] for c in range(nt)],
                                axis=1) for i in range(2)]
        o_ref[0, 0, pl.ds(g * 2 * SUBLANES, 2 * SUBLANES), :] = jnp.concatenate(rows, axis=0).astype(BF16)
        return carry

    lax.fori_loop(0, cap // (2 * SUBLANES), emit, 0, unroll=4)


def _gather(src_flat, h2_rows, ne, cap, t):
    bsz, rows, _ = h2_rows.shape
    nt = rows // t
    return pl.pallas_call(
        functools.partial(_gather_kernel, cap=cap, ne=ne, nt=nt),
        grid_spec=pltpu.PrefetchScalarGridSpec(
            num_scalar_prefetch=1,
            grid=(bsz, ne),
            in_specs=[pl.BlockSpec(memory_space=pl.ANY)],
            out_specs=pl.BlockSpec((1, 1, cap, nt * LANES), lambda b, e, src: (b, e, 0, 0)),
            scratch_shapes=[pltpu.VMEM((rows, LANES), F32), pltpu.VMEM((cap * nt, LANES), F32),
                            pltpu.SemaphoreType.DMA((1,))],
        ),
        out_shape=jax.ShapeDtypeStruct((bsz, ne, cap, nt * LANES), BF16),
        compiler_params=_cparams(("arbitrary", "arbitrary")),
        name="gather",
    )(src_flat, h2_rows)


def _ffn_kernel(xs_ref, gate_ref, wg_ref, wu_ref, wd_ref, y_ref):
    e, f = pl.program_id(0), pl.program_id(1)
    cap, d = xs_ref.shape[2], xs_ref.shape[3]
    nt = d // LANES

    @pl.when(f == 0)
    def _():
        y_ref[...] = jnp.zeros(y_ref.shape, F32)

    wg = wg_ref[0, 0].astype(BF16)
    wu = wu_ref[0, 0].astype(BF16)
    wd = wd_ref[0, 0].astype(BF16)
    lane = lax.broadcasted_iota(I32, (cap, LANES), 1)
    for b in range(xs_ref.shape[0]):
        xb = xs_ref[b, 0]
        g = _dot(xb, wg)
        u = _dot(xb, wu)
        gate = jnp.sum(jnp.where(lane == e, gate_ref[b], 0.0), axis=1, keepdims=True)
        hid = (g * jax.nn.sigmoid(g) * u * gate).astype(BF16)
        part = _dot(hid, wd)
        for i in range(cap // SUBLANES):
            for c in range(nt):
                r0 = (i * nt + c) * SUBLANES
                y_ref[b, 0, r0:r0 + SUBLANES, :] += part[i * SUBLANES:(i + 1) * SUBLANES, c * LANES:(c + 1) * LANES]


def _ffn(xs, gate_cols, w_gate, w_up, w_down, layer, tf):
    bsz, ne, cap, d = xs.shape
    ff = w_gate.shape[-1]
    tf = min(tf, ff)
    nt = d // LANES
    return pl.pallas_call(
        _ffn_kernel,
        grid=(ne, ff // tf),
        in_specs=[pl.BlockSpec((bsz, 1, cap, d), lambda e, f: (0, e, 0, 0)),
                  pl.BlockSpec((bsz, cap, LANES), lambda e, f: (0, 0, 0)),
                  pl.BlockSpec((1, 1, d, tf), lambda e, f: (layer, e, 0, f)),
                  pl.BlockSpec((1, 1, d, tf), lambda e, f: (layer, e, 0, f)),
                  pl.BlockSpec((1, 1, tf, d), lambda e, f: (layer, e, f, 0))],
        out_specs=pl.BlockSpec((bsz, 1, cap * nt, LANES), lambda e, f: (0, e, 0, 0)),
        out_shape=jax.ShapeDtypeStruct((bsz, ne, cap * nt, LANES), F32),
        compiler_params=_cparams(("arbitrary", "arbitrary")),
        name="ffn",
    )(xs, gate_cols, w_gate, w_up, w_down)


SCATTER_SPLIT = 8


RESIDUAL_TILE = 256
RESIDUAL_LOOKAHEAD = 3


def _scatter_kernel(dst_ref, cut_ref, src_ref, y_ref, zero_hbm, *rest, cap, ne, nt, fused):
    if fused:
        x_hbm, nw_ref, g_ref, o_hbm = rest[:4]
        accs = rest[4:4 + SCATTER_SPLIT]
        sem, x_buf, o_buf, x_sem, o_sem = rest[4 + SCATTER_SPLIT:]
    else:
        o_hbm = rest[0]
        accs, sem = rest[1:1 + SCATTER_SPLIT], rest[1 + SCATTER_SPLIT]
    b, e = pl.program_id(0), pl.program_id(1)

    @pl.when(e == 0)
    def _():
        fills = [pltpu.make_async_copy(zero_hbm, acc, sem.at[k]) for k, acc in enumerate(accs)]
        for cp in fills:
            cp.start()
        for cp in fills:
            cp.wait()

    base = (b * ne + e) * cap
    cbase = (b * ne + e) * (SCATTER_SPLIT + 1)
    starts = [cut_ref[cbase + k] for k in range(SCATTER_SPLIT)]
    counts = [cut_ref[cbase + k + 1] - starts[k] for k in range(SCATTER_SPLIT)]
    shortest = functools.reduce(jnp.minimum, counts)
    longest = functools.reduce(jnp.maximum, counts)

    def update(acc, j, dst, scale=None):
        src = src_ref[j]
        row = y_ref[0, 0, pl.ds(src, nt, stride=SUBLANES), :]
        acc[pl.ds(dst, nt, stride=SUBLANES), :] += row if scale is None else scale * row

    def common(i, carry):
        for k, acc in enumerate(accs):
            j = starts[k] + i
            update(acc, j, dst_ref[base + j])
        return carry

    def tail(i, carry):
        for k, acc in enumerate(accs):
            valid = i < counts[k]
            j = jnp.minimum(starts[k] + i, cap - 1)
            update(acc, j, jnp.where(valid, dst_ref[base + j], 0), jnp.where(valid, 1.0, 0.0))
        return carry

    lax.fori_loop(0, shortest, common, 0)
    lax.fori_loop(shortest, longest, tail, 0)

    part = accs[0].shape[0]

    @pl.when(e == ne - 1)
    def _():
        if not fused:
            copies = [pltpu.make_async_copy(acc, o_hbm.at[b, pl.ds(k * part, part)], sem.at[k])
                      for k, acc in enumerate(accs)]
            for cp in copies:
                cp.start()
            for cp in copies:
                cp.wait()
            return
        tt = RESIDUAL_TILE
        d = nt * LANES
        per_acc = part // (tt * nt)
        n_tiles = SCATTER_SPLIT * per_acc
        nx = RESIDUAL_LOOKAHEAD + 1

        def x_copy(i):
            return pltpu.make_async_copy(x_hbm.at[b, pl.ds(i * tt, tt)], x_buf.at[i % nx], x_sem.at[i % nx])

        def o_copy(i):
            return pltpu.make_async_copy(o_buf.at[i % 2], o_hbm.at[b, pl.ds(i * tt, tt)], o_sem.at[i % 2])

        for i in range(min(RESIDUAL_LOOKAHEAD, n_tiles)):
            x_copy(i).start()
        for i in range(n_tiles):
            x_copy(i).wait()
            if i + RESIDUAL_LOOKAHEAD < n_tiles:
                x_copy(i + RESIDUAL_LOOKAHEAD).start()
            if i >= 2:
                o_copy(i - 2).wait()
            acc = accs[i // per_acc]
            moe = _load_tile_rows(acc.at[pl.ds((i % per_acc) * tt * nt, tt * nt)], tt, d)
            o_buf[i % 2] = x_buf[i % nx] + g_ref[0] * _rms(moe, nw_ref[...])
            o_copy(i).start()
        for i in range(max(n_tiles - 2, 0), n_tiles):
            o_copy(i).wait()


def _scatter(dst_flat, cuts_flat, y_rows, t, nt, residual=None):
    bsz, ne, rows, _ = y_rows.shape
    cap = rows // nt
    d = nt * LANES
    part_rows = (t // SCATTER_SPLIT) * nt
    zeros = jnp.zeros((part_rows, LANES), F32)
    fused = residual is not None
    in_specs = [pl.BlockSpec((1, 1, rows, LANES), lambda b, e, i, c, s: (b, e, 0, 0)),
                pl.BlockSpec(memory_space=pl.ANY)]
    scratch = [pltpu.VMEM((part_rows, LANES), F32)] * SCATTER_SPLIT + [pltpu.SemaphoreType.DMA((SCATTER_SPLIT,))]
    args = (dst_flat, cuts_flat, _tile_row_base(jnp.arange(cap, dtype=I32), nt), y_rows, zeros)
    if fused:
        assert (t // SCATTER_SPLIT) % RESIDUAL_TILE == 0
        in_specs += [pl.BlockSpec(memory_space=pl.ANY), pl.BlockSpec((1, d), lambda b, e, i, c, s: (0, 0)),
                     pl.BlockSpec((1, 1, d), lambda b, e, i, c, s: (b, 0, 0))]
        scratch += [pltpu.VMEM((RESIDUAL_LOOKAHEAD + 1, RESIDUAL_TILE, d), F32), pltpu.VMEM((2, RESIDUAL_TILE, d), F32),
                    pltpu.SemaphoreType.DMA((RESIDUAL_LOOKAHEAD + 1,)), pltpu.SemaphoreType.DMA((2,))]
        args += residual
        out_shape = jax.ShapeDtypeStruct((bsz, t, d), F32)
    else:
        out_shape = jax.ShapeDtypeStruct((bsz, t * nt, LANES), F32)
    return pl.pallas_call(
        functools.partial(_scatter_kernel, cap=cap, ne=ne, nt=nt, fused=fused),
        grid_spec=pltpu.PrefetchScalarGridSpec(
            num_scalar_prefetch=3,
            grid=(bsz, ne),
            in_specs=in_specs,
            out_specs=pl.BlockSpec(memory_space=pl.ANY),
            scratch_shapes=scratch,
        ),
        out_shape=out_shape,
        compiler_params=_cparams(("arbitrary", "arbitrary")),
        name="scatter",
    )(*args)


def _moe(h2_rows, logits, ne, w_gate, w_up, w_down, layer, tf, residual=None):
    bsz, t, _ = logits.shape
    nt = h2_rows.shape[1] // t
    cap = EC_CAPACITY_FACTOR * t // ne
    tpart = t // SCATTER_SPLIT
    idx_cols, gate_cols = _route(logits, ne, cap)
    idx = jnp.swapaxes(idx_cols[:, :, :ne], 1, 2)
    xs = _gather(_tile_row_base(idx, nt).reshape(-1), h2_rows, ne, cap, t)
    y = _ffn(xs, gate_cols, w_gate, w_up, w_down, layer, tf)
    edges = jnp.arange(SCATTER_SPLIT + 1, dtype=I32) * tpart
    cuts = jnp.sum(idx[..., None] < edges, axis=2).astype(I32)
    return _scatter(_tile_row_base(idx % tpart, nt).reshape(-1), cuts.reshape(-1), y, t, nt, residual)


def _sincos_tables(rows, dim):
    quarter = dim // 4
    freqs = jnp.exp(-math.log(10000.0) * jnp.arange(quarter, dtype=F32) / quarter)

    def emb1d(n):
        ang = jnp.arange(n, dtype=F32)[:, None] * freqs[None, :]
        return jnp.concatenate([jnp.sin(ang), jnp.cos(ang)], axis=-1)

    return emb1d(rows), emb1d(GRID_W)


TOKEN_TILE = 256
FFN_TILE = 512


def kernel(x, c, ctx, c_ctx, w_ada, b_ada, norm_w, hg_w_in, hg_lb, hg_gnorm, hg_w_out, sg_w_in, sg_ln_w, sg_ln_b,
           sg_w_s, sg_b_s, sg_w_out, moe_router, moe_w_gate, moe_w_up, moe_w_down):
    bsz, t, d = x.shape
    depth = w_ada.shape[0]
    assert depth == 2 and d == N_HEADS * HEAD_DIM and t % GLA_CHUNK == 0 and ctx.shape[1] % GLA_CHUNK == 0
    tm = min(TOKEN_TILE, t)

    cvec = jnp.zeros((SUBLANES, d), F32).at[:bsz].set(c).at[bsz].set(c_ctx)
    mod = _ada(cvec, w_ada, b_ada)

    def mods(layer, rows):
        m = mod[layer, rows].reshape(-1, N_ADA, 1, d)
        return [m[:, k] for k in range(N_ADA)]

    nw = norm_w.reshape(depth, 4, 1, d)
    ne = moe_router.shape[-1]
    router = jnp.zeros((depth, d, LANES), F32).at[:, :, :ne].set(moe_router)
    r_hi = router.astype(BF16)
    r_lo = (router - r_hi.astype(F32)).astype(BF16)

    sh1, sc1, g1, sh2, sc2, g2 = mods(0, slice(0, bsz))
    sh1c, sc1c = [jnp.broadcast_to(m, (bsz, 1, d)) for m in mods(0, slice(bsz, bsz + 1))[:2]]
    w_in_bf = hg_w_in[0].astype(BF16)
    pos = _sincos_tables(t // GRID_W, d)
    tri, masks = _gla_consts()

    cv, ckf, cgf, ckb, cgb, cq, _ = _hg_in(ctx, None, nw[0, 0], sh1c, sc1c, w_in_bf, hg_lb, 1, tm)
    s_zero = jnp.zeros((bsz, 2, N_HEADS, HEAD_DIM, HEAD_DIM), F32)
    _, _, s_ctx = _gla(ckf, cgf, ckb, cgb, cv, cq, s_zero, tri, masks)

    v, kf, gf, kb, gb, q, sg = _hg_in(x, pos, nw[0, 0], sh1, sc1, w_in_bf, hg_lb, 1, tm)
    o_f, o_b, _ = _gla(kf, gf, kb, gb, v, q, s_ctx, tri, masks)
    x1, h2, lg = _hg_out(o_f, o_b, sg, hg_gnorm[0:1], hg_w_out[0].astype(BF16), x, pos, nw[0, 1], g1, nw[0, 2],
                         sh2, sc2, r_hi[0], r_lo[0], tm)
    moe = _moe(h2, lg, ne, moe_w_gate, moe_w_up, moe_w_down, 0, FFN_TILE)

    sh1b, sc1b, g1b, sh2b, sc2b, g2b = mods(1, slice(0, bsz))
    x1b, h2b, lgb = _sgu(x1, moe, nw[0, 3], g2, nw[1, 0], sh1b, sc1b, sg_w_in[0].astype(BF16),
                         sg_ln_w[0:1], sg_ln_b[0:1], sg_w_s[0].astype(BF16), sg_b_s[0].T, sg_w_out[0].astype(BF16),
                         nw[1, 1], g1b, nw[1, 2], sh2b, sc2b, r_hi[1], r_lo[1], tm)
    return _moe(h2b, lgb, ne, moe_w_gate, moe_w_up, moe_w_down, 1, FFN_TILE, residual=(x1b, nw[1, 3], g2b))
```

```python
import functools
import math

import jax
import jax.numpy as jnp
from jax import lax
from jax.experimental import pallas as pl
from jax.experimental.pallas import tpu as pltpu

F32 = jnp.float32
BF16 = jnp.bfloat16
I32 = jnp.int32
HIGHEST = lax.Precision.HIGHEST

EPS = 1e-6
GRID_W = 64
N_ADA = 6
N_HEADS = 8
HEAD_DIM = 128
EC_CAPACITY_FACTOR = 2
SGU_CHUNK = 128
SGU_GROUPS = 8

LANES = 128
SUBLANES = 8
GLA_CHUNK = 128
GLA_LEVELS = (64, 32, 16, 8, 4, 2, 1)
GLA_SHORT_BLOCK = 32
GLA_SHORT_MAX_EXPONENT = 60.0
BISECT_ITERS = 160
VMEM_LIMIT = 52 * 1024 * 1024


def _cparams(sem):
    return pltpu.CompilerParams(dimension_semantics=sem, vmem_limit_bytes=VMEM_LIMIT)


def _rms(x, w):
    ms = jnp.mean(x * x, axis=-1, keepdims=True)
    return x * lax.rsqrt(ms + EPS) * w


def _dot(a, b):
    return jnp.dot(a, b, preferred_element_type=F32)


def _dot_nt(a, b):
    return lax.dot_general(a, b, (((1,), (1,)), ((), ())), preferred_element_type=F32)


def _dot_tn(a, b):
    return lax.dot_general(a, b, (((0,), (0,)), ((), ())), preferred_element_type=F32)


def _ada_kernel(c_ref, w_ref, b_ref, o_ref):
    c = c_ref[...]
    s = c * jax.nn.sigmoid(c)
    o_ref[0] = jnp.dot(s, w_ref[0], precision=HIGHEST, preferred_element_type=F32) + b_ref[0]


def _ada(cvec, w_ada, b_ada):
    depth, d, nd = w_ada.shape
    rows = cvec.shape[0]
    return pl.pallas_call(
        _ada_kernel,
        grid=(depth, nd // d),
        in_specs=[pl.BlockSpec((rows, d), lambda l, n: (0, 0)),
                  pl.BlockSpec((1, d, d), lambda l, n: (l, 0, n)),
                  pl.BlockSpec((1, 1, d), lambda l, n: (l, 0, n))],
        out_specs=pl.BlockSpec((1, rows, d), lambda l, n: (l, 0, n)),
        out_shape=jax.ShapeDtypeStruct((depth, rows, nd), F32),
        compiler_params=_cparams(("arbitrary", "arbitrary")),
        name="ada",
    )(cvec, w_ada, b_ada.reshape(depth, 1, nd))


def _hgin_body(x, nw_ref, sh_ref, sc_ref, w_ref, lb_ref, outs, n_lb):
    v_ref, kf_ref, gf_ref, kb_ref, gb_ref, q_ref, sg_ref = outs
    d = x.shape[-1]
    h = _rms(x, nw_ref[...]) * (1.0 + sc_ref[0]) + sh_ref[0]
    hb = h.astype(BF16)
    lbs = lb_ref[...]
    e = jnp.exp(lbs - jnp.max(lbs, axis=0, keepdims=True))
    lb = jnp.sum(e[:n_lb], axis=0) / jnp.sum(e, axis=0)

    v_ref[0] = _dot(hb, w_ref[:, 0:d]).astype(BF16)
    for j, (k_ref, g_ref) in enumerate(((kf_ref, gf_ref), (kb_ref, gb_ref))):
        raw = _dot(hb, w_ref[:, (1 + j) * d:(2 + j) * d])
        lbj = lb[j:j + 1]
        sig = jax.nn.sigmoid(raw)
        f = lbj + (1.0 - lbj) * sig
        k_ref[0] = ((1.0 - lbj) * (1.0 - sig)).astype(BF16)
        g_ref[0] = jnp.log(f)
    qr = _dot(hb, w_ref[:, 3 * d:4 * d])
    q_ref[0] = (qr * jax.nn.sigmoid(qr)).astype(BF16)
    gr = _dot(hb, w_ref[:, 4 * d:5 * d])
    sg_ref[0] = (gr * jax.nn.sigmoid(gr)).astype(BF16)


def _with_pos(x, er_ref, ec_ref):
    er, ec = er_ref[0], ec_ref[...]
    pos = jnp.concatenate([jnp.concatenate([jnp.broadcast_to(er[r:r + 1], ec.shape), ec], axis=1)
                           for r in range(er.shape[0])], axis=0)
    return x + pos


def _pos_specs(pos, tm, d):
    er, ec = pos
    assert tm % GRID_W == 0
    rpt = tm // GRID_W
    specs = [pl.BlockSpec((1, rpt, d // 2), lambda b, i: (i, 0, 0)), pl.BlockSpec(ec.shape, lambda b, i: (0, 0))]
    return specs, (er.reshape(-1, rpt, d // 2), ec)


def _hgin_pos_kernel(x_ref, er_ref, ec_ref, nw_ref, sh_ref, sc_ref, w_ref, lb_ref, *outs, n_lb):
    _hgin_body(_with_pos(x_ref[0], er_ref, ec_ref), nw_ref, sh_ref, sc_ref, w_ref, lb_ref, outs, n_lb)


def _hgin_kernel(x_ref, nw_ref, sh_ref, sc_ref, w_ref, lb_ref, *outs, n_lb):
    _hgin_body(x_ref[0], nw_ref, sh_ref, sc_ref, w_ref, lb_ref, outs, n_lb)


def _hg_in(x, pos, nw, sh, sc, w_bf, lb_raw, n_lb, tm):
    bsz, t, d = x.shape
    tm = min(tm, t)
    tok = pl.BlockSpec((1, tm, d), lambda b, i: (b, i, 0))
    vec = pl.BlockSpec((1, d), lambda b, i: (0, 0))
    bvec = pl.BlockSpec((1, 1, d), lambda b, i: (b, 0, 0))
    wspec = pl.BlockSpec(w_bf.shape, lambda b, i: (0, 0))
    lbspec = pl.BlockSpec(lb_raw.shape, lambda b, i: (0, 0, 0))
    gate_shapes = [jax.ShapeDtypeStruct((bsz, t, d), dt) for dt in (BF16, BF16, F32, BF16, F32, BF16, BF16)]
    if pos is not None:
        kern = functools.partial(_hgin_pos_kernel, n_lb=n_lb)
        pspecs, pargs = _pos_specs(pos, tm, d)
        in_specs = [tok] + pspecs + [vec, bvec, bvec, wspec, lbspec]
        args = (x,) + pargs + (nw, sh, sc, w_bf, lb_raw)
        out_shape = gate_shapes
    else:
        kern = functools.partial(_hgin_kernel, n_lb=n_lb)
        in_specs = [tok, vec, bvec, bvec, wspec, lbspec]
        args = (x, nw, sh, sc, w_bf, lb_raw)
        out_shape = gate_shapes
    return pl.pallas_call(
        kern,
        grid=(bsz, t // tm),
        in_specs=in_specs,
        out_specs=[tok] * len(out_shape),
        out_shape=out_shape,
        compiler_params=_cparams(("arbitrary", "arbitrary")),
        name="hg_in",
    )(*args)


def _gla_consts():
    c = GLA_CHUNK
    t = jnp.arange(c)[:, None]
    s = jnp.arange(c)[None, :]
    tri = jnp.stack([(s <= t), (s >= t)]).astype(BF16)
    masks = []
    for rev in (False, True):
        lv = []
        for m in GLA_LEVELS:
            same = (t // (2 * m)) == (s // (2 * m))
            tq = ((t // m) % 2) == (0 if rev else 1)
            sk = ((s // m) % 2) == (1 if rev else 0)
            lv.append(same & tq & sk)
        lv.append(t == s)
        pair = 2 * GLA_SHORT_BLOCK
        lv.append(((t // pair) == (s // pair)) & ((s >= t) if rev else (s <= t)))
        masks.append(jnp.stack(lv))
    return tri, jnp.stack(masks).astype(F32)


def _split3(x):
    p0 = x.astype(BF16)
    r1 = x - p0.astype(F32)
    p1 = r1.astype(BF16)
    p2 = (r1 - p1.astype(F32)).astype(BF16)
    return p0, p1, p2


def _level_ref(b_scr, hs, m, rev):
    c = GLA_CHUNK
    off = m if rev else m - 1

    def row8(i):
        return jnp.broadcast_to(b_scr[pl.ds(i, 1), hs], (SUBLANES, HEAD_DIM))

    pieces = []
    if m >= SUBLANES:
        for blk in range(c // (2 * m)):
            r8 = row8(blk * 2 * m + off)
            pieces.extend([r8] * (2 * m // SUBLANES))
    else:
        sub = lax.broadcasted_iota(I32, (SUBLANES, HEAD_DIM), 0) // (2 * m)
        for grp in range(c // SUBLANES):
            piece = row8(grp * SUBLANES + off)
            for cls in range(1, SUBLANES // (2 * m)):
                piece = jnp.where(sub == cls, row8(grp * SUBLANES + cls * 2 * m + off), piece)
            pieces.append(piece)
    return jnp.concatenate(pieces, axis=0)


def _block_decay_bound(b_scr, rev):
    c, blk = GLA_CHUNK, GLA_SHORT_BLOCK
    worst = None
    for i in range(c // blk):
        inner = b_scr[pl.ds(i * blk if rev else (i + 1) * blk - 1, 1), :]
        r = (i + 1) * blk if rev else i * blk - 1
        span = jnp.abs(inner - b_scr[pl.ds(r, 1), :]) if 0 <= r < c else jnp.abs(inner)
        worst = span if worst is None else jnp.maximum(worst, span)
    return jnp.max(worst)


def _gla_head(q_ref, k_ref, v_ref, o_ref, mask_ref, st_scr, b_scr, d_idx, rev, h, short):
    c = GLA_CHUNK
    n_lv = len(GLA_LEVELS)
    hs = pl.ds(pl.multiple_of(h * HEAD_DIM, HEAD_DIM), HEAD_DIM)
    b = b_scr[:, hs]
    q = q_ref[0, :, hs]
    k = k_ref[0, :, hs]
    vb = v_ref[0, :, hs]
    bl = b_scr[pl.ds(0 if rev else c - 1, 1), hs]
    st = st_scr[d_idx, h]

    if short:
        blk = 2 * GLA_SHORT_BLOCK
        nb = c // blk
        own = [b_scr[pl.ds(i * blk + (blk // 2 if rev else blk // 2 - 1), 1), hs] for i in range(nb)]

        def blockwise(rows):
            return jnp.concatenate([jnp.broadcast_to(r, (blk, HEAD_DIM)) for r in rows], axis=0)

        u = blockwise(own) - b
        qh = q * jnp.exp(-u).astype(BF16)
        kh = k * jnp.exp(u).astype(BF16)
        qt = qh * blockwise([jnp.exp(r) for r in own]).astype(BF16)
        kt = kh * blockwise([jnp.exp(bl - r) for r in own]).astype(BF16)
        a = jnp.where(mask_ref[d_idx, n_lv + 1] > 0.0, _dot_nt(qh, kh), 0.0)
        zero = jnp.zeros((1, HEAD_DIM), F32)
        for li, m in enumerate(GLA_LEVELS):
            if m < blk:
                continue
            qf, kf = [], []
            for i in range(nb):
                lvl = b_scr[pl.ds((i * blk) // (2 * m) * 2 * m + (m if rev else m - 1), 1), hs]
                is_query = ((i * blk) // m) % 2 == (0 if rev else 1)
                qf.append(jnp.exp(own[i] - lvl) if is_query else zero)
                kf.append(zero if is_query else jnp.exp(lvl - own[i]))
            a = a + mask_ref[d_idx, li] * _dot_nt(qh * blockwise(qf).astype(BF16), kh * blockwise(kf).astype(BF16))
    else:
        qt = q * jnp.exp(b).astype(BF16)
        kt = k * jnp.exp(bl - b).astype(BF16)

        def level(m):
            e = jnp.exp(-jnp.abs(b - _level_ref(b_scr, hs, m, rev))).astype(BF16)
            return _dot_nt(q * e, k * e)

        a = mask_ref[d_idx, n_lv] * _dot_nt(q, k)
        for li, m in enumerate(GLA_LEVELS):
            a = a + mask_ref[d_idx, li] * level(m)
    o_ref[0, :, hs] = (_dot_nt(qt, st.astype(BF16)) + _dot(a.astype(BF16), vb)).astype(o_ref.dtype)
    st_scr[d_idx, h] = st * jnp.exp(bl) + _dot_tn(vb, kt)


def _gla_kernel(kf_ref, gf_ref, vf_ref, qf_ref, kb_ref, gb_ref, vb_ref, qb_ref, s0_ref, tri_ref, mask_ref,
                of_ref, ob_ref, sout_ref, st_scr, bf_scr, bb_scr):
    n = pl.program_id(1)

    @pl.when(n == 0)
    def _():
        st_scr[...] = s0_ref[0]

    dirs = ((qf_ref, kf_ref, vf_ref, gf_ref, of_ref, bf_scr, 0, False),
            (qb_ref, kb_ref, vb_ref, gb_ref, ob_ref, bb_scr, 1, True))
    bounded = []
    for q_ref, k_ref, v_ref, g_ref, o_ref, b_scr, d_idx, rev in dirs:
        p0, p1, p2 = _split3(g_ref[0])
        tri = tri_ref[d_idx]
        b_scr[...] = _dot(tri, p0) + _dot(tri, p1) + _dot(tri, p2)
        bounded.append(_block_decay_bound(b_scr, rev) <= GLA_SHORT_MAX_EXPONENT)

    def heads(which, short, unroll):
        def body(h, carry):
            for q_ref, k_ref, v_ref, _, o_ref, b_scr, d_idx, rev in which:
                _gla_head(q_ref, k_ref, v_ref, o_ref, mask_ref, st_scr, b_scr, d_idx, rev, h, short)
            return carry
        lax.fori_loop(0, N_HEADS, body, 0, unroll=unroll)

    both = jnp.logical_and(bounded[0], bounded[1])

    @pl.when(both)
    def _():
        heads(dirs, True, 8)

    for d, ok in zip(dirs, bounded):
        @pl.when(jnp.logical_and(jnp.logical_not(both), ok))
        def _():
            heads((d,), True, 2)

        @pl.when(jnp.logical_not(ok))
        def _():
            heads((d,), False, 2)

    @pl.when(n == pl.num_programs(1) - 1)
    def _():
        sout_ref[0] = st_scr[...]


def _gla(kf, gf, kb, gb, v, q, s0, tri, masks):
    bsz, t, d = v.shape
    c = GLA_CHUNK
    n = t // c
    fwd = pl.BlockSpec((1, c, d), lambda b, i: (b, i, 0))
    bwd = pl.BlockSpec((1, c, d), lambda b, i: (b, n - 1 - i, 0))
    sspec = pl.BlockSpec((1,) + s0.shape[1:], lambda b, i: (b, 0, 0, 0, 0))
    return pl.pallas_call(
        _gla_kernel,
        grid=(bsz, n),
        in_specs=[fwd, fwd, fwd, fwd, bwd, bwd, bwd, bwd, sspec,
                  pl.BlockSpec(tri.shape, lambda b, i: (0, 0, 0)),
                  pl.BlockSpec(masks.shape, lambda b, i: (0, 0, 0, 0))],
        out_specs=[fwd, bwd, sspec],
        out_shape=[jax.ShapeDtypeStruct((bsz, t, d), BF16), jax.ShapeDtypeStruct((bsz, t, d), BF16),
                   jax.ShapeDtypeStruct(s0.shape, F32)],
        scratch_shapes=[pltpu.VMEM(s0.shape[1:], F32), pltpu.VMEM((c, d), F32), pltpu.VMEM((c, d), F32)],
        compiler_params=_cparams(("arbitrary", "arbitrary")),
        name="gla",
    )(kf, gf, v, q, kb, gb, v, q, s0, tri, masks)


def _store_tile_rows(ref2, x):
    n, d = x.shape
    nt = d // LANES
    for i in range(n // SUBLANES):
        for c in range(nt):
            r0 = (i * nt + c) * SUBLANES
            ref2[r0:r0 + SUBLANES, :] = x[i * SUBLANES:(i + 1) * SUBLANES, c * LANES:(c + 1) * LANES]


def _load_tile_rows(ref2, n, d):
    nt = d // LANES
    rows = []
    for i in range(n // SUBLANES):
        rows.append(jnp.concatenate(
            [ref2[(i * nt + c) * SUBLANES:(i * nt + c + 1) * SUBLANES, :] for c in range(nt)], axis=1))
    return jnp.concatenate(rows, axis=0)


def _tile_row_base(t, nt):
    return (t >> 3) * (nt * SUBLANES) + (t & (SUBLANES - 1))


def _tile_rows_spec(tm, d, index_map):
    return pl.BlockSpec((1, tm * (d // LANES), LANES), index_map)


def _post_mixer(y, xres, rows, nw1_ref, g1_ref, nw2_ref, sh2_ref, sc2_ref, rhi_ref, rlo_ref, x1_ref, h2_ref, lg_ref):
    d = y.shape[-1]
    nt = d // LANES
    x1 = xres + g1_ref[0] * _rms(y, nw1_ref[...])
    x1_ref[0, rows, :] = x1
    h2 = _rms(x1, nw2_ref[...]) * (1.0 + sc2_ref[0]) + sh2_ref[0]
    _store_tile_rows(h2_ref.at[0, pl.ds(rows.start * nt, (rows.stop - rows.start) * nt)], h2)
    h_hi = h2.astype(BF16)
    h_lo = (h2 - h_hi.astype(F32)).astype(BF16)
    lg_ref[0, rows, :] = _dot(h_hi, rhi_ref[...]) + (_dot(h_lo, rhi_ref[...]) + _dot(h_hi, rlo_ref[...]))


def _hgout_kernel(of_ref, ob_ref, sg_ref, gn_ref, w_ref, x_ref, er_ref, ec_ref, nw1_ref, g1_ref, nw2_ref, sh2_ref,
                  sc2_ref, rhi_ref, rlo_ref, x1_ref, h2_ref, lg_ref, z_scr):
    o = of_ref[0].astype(F32) + ob_ref[0].astype(F32)
    for h in range(N_HEADS):
        hs = slice(h * HEAD_DIM, (h + 1) * HEAD_DIM)
        oh = o[:, hs]
        ms = jnp.mean(oh * oh, axis=-1, keepdims=True)
        z = oh * lax.rsqrt(ms + EPS) * gn_ref[:, hs] * sg_ref[0, :, hs].astype(F32)
        z_scr[:, hs] = z.astype(BF16)
    y = _dot(z_scr[...], w_ref[...])
    _post_mixer(y, _with_pos(x_ref[0], er_ref, ec_ref), slice(0, y.shape[0]), nw1_ref, g1_ref, nw2_ref, sh2_ref,
                sc2_ref, rhi_ref, rlo_ref, x1_ref, h2_ref, lg_ref)


def _hg_out(o_f, o_b, sg, gnorm, w_out_bf, x, pos, nw1, g1, nw2, sh2, sc2, r_hi, r_lo, tm):
    bsz, t, d = x.shape
    rspec = pl.BlockSpec((d, LANES), lambda b, i: (0, 0))
    pspecs, pargs = _pos_specs(pos, tm, d)
    tok = pl.BlockSpec((1, tm, d), lambda b, i: (b, i, 0))
    vec = pl.BlockSpec((1, d), lambda b, i: (0, 0))
    bvec = pl.BlockSpec((1, 1, d), lambda b, i: (b, 0, 0))
    return pl.pallas_call(
        _hgout_kernel,
        grid=(bsz, t // tm),
        in_specs=[tok, tok, tok, vec, pl.BlockSpec((d, d), lambda b, i: (0, 0)), tok] + pspecs
        + [vec, bvec, vec, bvec, bvec, rspec, rspec],
        out_specs=[tok, _tile_rows_spec(tm, d, lambda b, i: (b, i, 0)),
                   pl.BlockSpec((1, tm, LANES), lambda b, i: (b, i, 0))],
        out_shape=[jax.ShapeDtypeStruct((bsz, t, d), F32), jax.ShapeDtypeStruct((bsz, t * (d // LANES), LANES), F32),
                   jax.ShapeDtypeStruct((bsz, t, LANES), F32)],
        scratch_shapes=[pltpu.VMEM((tm, d), BF16)],
        compiler_params=_cparams(("arbitrary", "arbitrary")),
        name="hg_out",
    )(o_f, o_b, sg, gnorm, w_out_bf, x, *pargs, nw1, g1, nw2, sh2, sc2, r_hi, r_lo)


def _sgu_kernel(x1p_ref, moe_ref, nw3p_ref, g2p_ref, nw0_ref, sh1_ref, sc1_ref, win_ref, lnw_ref, lnb_ref,
                ws_ref, bs_ref, wout_ref, nw1_ref, g1_ref, nw2_ref, sh2_ref, sc2_ref, rhi_ref, rlo_ref,
                x1_ref, h2_ref, lg_ref):
    tm, d = x1p_ref.shape[1], x1p_ref.shape[2]
    w = lnw_ref.shape[-1]
    gd = w // SGU_GROUPS
    nt = d // LANES

    def gelu(z):
        return 0.5 * z * (1.0 + jnp.tanh(math.sqrt(2.0 / math.pi) * (z + 0.044715 * (z * z * z))))

    for ck in range(tm // SGU_CHUNK):
        rs = slice(ck * SGU_CHUNK, (ck + 1) * SGU_CHUNK)
        moe = _load_tile_rows(moe_ref.at[0, pl.ds(ck * SGU_CHUNK * nt, SGU_CHUNK * nt)], SGU_CHUNK, d)
        x = x1p_ref[0, rs, :] + g2p_ref[0] * _rms(moe, nw3p_ref[...])
        h = _rms(x, nw0_ref[...]) * (1.0 + sc1_ref[0]) + sh1_ref[0]
        hb = h.astype(BF16)
        u = gelu(_dot(hb, win_ref[:, 0:w]))
        v = gelu(_dot(hb, win_ref[:, w:2 * w]))
        mu = jnp.mean(v, axis=-1, keepdims=True)
        vc = v - mu
        vn = vc * lax.rsqrt(jnp.mean(vc * vc, axis=-1, keepdims=True) + EPS) * lnw_ref[...] + lnb_ref[...]
        vnb = vn.astype(BF16)
        gated = []
        for g in range(SGU_GROUPS):
            cs = slice(g * gd, (g + 1) * gd)
            mixed = _dot(ws_ref[g], vnb[:, cs]) + bs_ref[:, g:g + 1]
            gated.append((u[:, cs] * mixed).astype(BF16))
        y = _dot(jnp.concatenate(gated, axis=1), wout_ref[...])
        _post_mixer(y, x, rs, nw1_ref, g1_ref, nw2_ref, sh2_ref, sc2_ref, rhi_ref, rlo_ref, x1_ref, h2_ref, lg_ref)


def _sgu(x1p, moe, nw3p, g2p, nw0, sh1, sc1, w_in_bf, ln_w, ln_b, w_s_bf, b_s_t, w_out_bf,
         nw1, g1, nw2, sh2, sc2, r_hi, r_lo, tm):
    bsz, t, d = x1p.shape
    w = ln_w.shape[-1]
    tok = pl.BlockSpec((1, tm, d), lambda b, i: (b, i, 0))
    tiles = _tile_rows_spec(tm, d, lambda b, i: (b, i, 0))
    vec = pl.BlockSpec((1, d), lambda b, i: (0, 0))
    wvec = pl.BlockSpec((1, w), lambda b, i: (0, 0))
    bvec = pl.BlockSpec((1, 1, d), lambda b, i: (b, 0, 0))

    def full(a):
        return pl.BlockSpec(a.shape, lambda b, i: (0,) * a.ndim)

    return pl.pallas_call(
        _sgu_kernel,
        grid=(bsz, t // tm),
        in_specs=[tok, tiles, vec, bvec, vec, bvec, bvec, full(w_in_bf), wvec, wvec, full(w_s_bf), full(b_s_t),
                  full(w_out_bf), vec, bvec, vec, bvec, bvec, full(r_hi), full(r_lo)],
        out_specs=[tok, tiles, pl.BlockSpec((1, tm, LANES), lambda b, i: (b, i, 0))],
        out_shape=[jax.ShapeDtypeStruct((bsz, t, d), F32), jax.ShapeDtypeStruct((bsz, t * (d // LANES), LANES), F32),
                   jax.ShapeDtypeStruct((bsz, t, LANES), F32)],
        compiler_params=_cparams(("arbitrary", "arbitrary")),
        name="sgu",
    )(x1p, moe, nw3p, g2p, nw0, sh1, sc1, w_in_bf, ln_w, ln_b, w_s_bf, b_s_t, w_out_bf,
      nw1, g1, nw2, sh2, sc2, r_hi, r_lo)


def _token_prefix(mask, triu, slow):
    local = _dot(mask.astype(BF16), triu)
    rowtot = jnp.broadcast_to(local[:, LANES - 1:LANES], local.shape)
    prev = _dot(slow, rowtot.astype(BF16))
    return local, prev, rowtot


def _route_kernel(lg_ref, triu_ref, slow_ref, idx_ref, gate_ref, a_scr, thr_scr, *, cap):
    ne, nr = a_scr.shape[0], a_scr.shape[1]
    for r in range(nr):
        a_scr[:, r, :] = lg_ref[0, r * LANES:(r + 1) * LANES, :].T[0:ne, :]
    lg = a_scr[...]
    ex = jnp.exp(lg - jnp.max(lg, axis=0, keepdims=True))
    a = ex / jnp.sum(ex, axis=0, keepdims=True)
    a_scr[...] = a
    capf = jnp.float32(cap)

    def count(m):
        return jnp.sum(jnp.sum(m.astype(F32), axis=1, keepdims=True), axis=2, keepdims=True)

    def unresolved(state):
        it, lo, hi = state
        mid = 0.5 * (lo + hi)
        return jnp.logical_and(it < BISECT_ITERS, jnp.max(((mid != lo) & (mid != hi)).astype(F32)) > 0.0)

    def bis(state):
        it, lo, hi = state
        mid = 0.5 * (lo + hi)
        ge = count(a >= mid) >= capf
        return it + 1, jnp.where(ge, mid, lo), jnp.where(ge, hi, mid)

    _, lo, _ = lax.while_loop(unresolved, bis, (jnp.int32(0), jnp.zeros((ne, 1, 1), F32), jnp.full((ne, 1, 1), 2.0, F32)))
    thr_scr[...] = jnp.broadcast_to(lo, thr_scr.shape)

    idx_ref[0] = jnp.zeros(idx_ref.shape[1:], I32)
    gate_ref[0] = jnp.zeros(gate_ref.shape[1:], F32)
    triu = triu_ref[...]
    slow = slow_ref[...]
    lane = lax.broadcasted_iota(I32, (LANES, LANES), 1)
    lane_f = lane.astype(F32)
    sub_f = lax.broadcasted_iota(I32, (LANES, LANES), 0).astype(F32)
    rowid = lax.broadcasted_iota(I32, (nr, LANES), 0).astype(F32)

    def per_expert(e, carry):
        ae = a_scr[e]
        v = thr_scr[e][0:1, :]
        gt = ae > v
        eq = ae == v
        need = capf - jnp.sum(jnp.sum(gt.astype(F32), axis=0, keepdims=True), axis=1, keepdims=True)
        eql, eqp, _ = _token_prefix(eq, triu, slow)
        sel = gt | (eq & ((eql + eqp - eq.astype(F32)) < need))
        local, prev, rowtot = _token_prefix(sel, triu, slow)
        rowcum = prev + rowtot
        prev_hi = jnp.floor(prev * (1.0 / LANES))
        a0, a1, a2 = _split3(ae)
        rhs = jnp.concatenate([piece.astype(BF16) for piece in
                               (local, prev_hi, prev - LANES * prev_hi, rowid, a0, a1, a2)], axis=1)

        for p in range(cap // LANES):
            base = float(p * LANES + 1)
            slot_row = base + lane_f[0:1, :]
            onehot_t = ((prev < slot_row) & (rowcum >= slot_row)).astype(BF16)
            g = _dot_tn(onehot_t, rhs)
            g_local = g[:, 0:LANES]
            g_prev = LANES * g[:, LANES:2 * LANES] + g[:, 2 * LANES:3 * LANES]
            g_row = g[:, 3 * LANES:4 * LANES]
            g_a = g[:, 4 * LANES:5 * LANES] + g[:, 5 * LANES:6 * LANES] + g[:, 6 * LANES:7 * LANES]
            slot_col = base + sub_f
            lstar = jnp.sum(((g_local + g_prev) < slot_col).astype(F32), axis=1, keepdims=True)
            tok = LANES * g_row[:, 0:1] + lstar
            gat = jnp.sum(jnp.where(lane_f == lstar, g_a, 0.0), axis=1, keepdims=True)
            rs = slice(p * LANES, (p + 1) * LANES)
            idx_ref[0, rs, :] = jnp.where(lane == e, tok.astype(I32), idx_ref[0, rs, :])
            gate_ref[0, rs, :] = jnp.where(lane == e, gat, gate_ref[0, rs, :])
        return carry

    lax.fori_loop(0, ne, per_expert, 0)


def _route(logits, ne, cap):
    bsz, t, _ = logits.shape
    nr = t // LANES
    assert cap % LANES == 0 and nr % SUBLANES == 0
    triu = (jnp.arange(LANES)[:, None] <= jnp.arange(LANES)[None, :]).astype(BF16)
    slow = (jnp.arange(nr)[:, None] > jnp.arange(nr)[None, :]).astype(BF16)
    idx, gate = pl.pallas_call(
        functools.partial(_route_kernel, cap=cap),
        grid=(bsz,),
        in_specs=[pl.BlockSpec((1, t, LANES), lambda b: (b, 0, 0)),
                  pl.BlockSpec((LANES, LANES), lambda b: (0, 0)), pl.BlockSpec((nr, nr), lambda b: (0, 0))],
        out_specs=[pl.BlockSpec((1, cap, LANES), lambda b: (b, 0, 0))] * 2,
        out_shape=[jax.ShapeDtypeStruct((bsz, cap, LANES), I32), jax.ShapeDtypeStruct((bsz, cap, LANES), F32)],
        scratch_shapes=[pltpu.VMEM((ne, nr, LANES), F32), pltpu.VMEM((ne, SUBLANES, LANES), F32)],
        compiler_params=_cparams(("arbitrary",)),
        name="route",
    )(logits, triu, slow)
    return idx, gate


def _gather_kernel(src_ref, h_hbm, o_ref, h_scr, row_scr, sem, *, cap, ne, nt):
    b, e = pl.program_id(0), pl.program_id(1)

    @pl.when(e == 0)
    def _():
        cp = pltpu.make_async_copy(h_hbm.at[b], h_scr, sem.at[0])
        cp.start()
        cp.wait()

    base = (b * ne + e) * cap
    group = nt * SUBLANES

    def body(jj, carry):
        for u in range(SUBLANES):
            row_scr[pl.ds(jj * group + u, nt, stride=SUBLANES), :] = \
                h_scr[pl.ds(src_ref[base + jj * SUBLANES + u], nt, stride=SUBLANES), :]
        return carry

    lax.fori_loop(0, cap // SUBLANES, body, 0, unroll=2)

    def emit(g, carry):
        rows = [jnp.concatenate([row_scr[pl.ds((g * 2 + i) * group + c * SUBLANES, SUBLANES), :] for c in range(nt)],
                                axis=1) for i in range(2)]
        o_ref[0, 0, pl.ds(g * 2 * SUBLANES, 2 * SUBLANES), :] = jnp.concatenate(rows, axis=0).astype(BF16)
        return carry

    lax.fori_loop(0, cap // (2 * SUBLANES), emit, 0, unroll=4)


def _gather(src_flat, h2_rows, ne, cap, t):
    bsz, rows, _ = h2_rows.shape
    nt = rows // t
    return pl.pallas_call(
        functools.partial(_gather_kernel, cap=cap, ne=ne, nt=nt),
        grid_spec=pltpu.PrefetchScalarGridSpec(
            num_scalar_prefetch=1,
            grid=(bsz, ne),
            in_specs=[pl.BlockSpec(memory_space=pl.ANY)],
            out_specs=pl.BlockSpec((1, 1, cap, nt * LANES), lambda b, e, src: (b, e, 0, 0)),
            scratch_shapes=[pltpu.VMEM((rows, LANES), F32), pltpu.VMEM((cap * nt, LANES), F32),
                            pltpu.SemaphoreType.DMA((1,))],
        ),
        out_shape=jax.ShapeDtypeStruct((bsz, ne, cap, nt * LANES), BF16),
        compiler_params=_cparams(("arbitrary", "arbitrary")),
        name="gather",
    )(src_flat, h2_rows)


def _ffn_kernel(xs_ref, gate_ref, wg_ref, wu_ref, wd_ref, y_ref):
    e, f = pl.program_id(0), pl.program_id(1)
    cap, d = xs_ref.shape[2], xs_ref.shape[3]
    nt = d // LANES

    @pl.when(f == 0)
    def _():
        y_ref[...] = jnp.zeros(y_ref.shape, F32)

    wg = wg_ref[0, 0].astype(BF16)
    wu = wu_ref[0, 0].astype(BF16)
    wd = wd_ref[0, 0].astype(BF16)
    lane = lax.broadcasted_iota(I32, (cap, LANES), 1)
    for b in range(xs_ref.shape[0]):
        xb = xs_ref[b, 0]
        g = _dot(xb, wg)
        u = _dot(xb, wu)
        gate = jnp.sum(jnp.where(lane == e, gate_ref[b], 0.0), axis=1, keepdims=True)
        hid = (g * jax.nn.sigmoid(g) * u * gate).astype(BF16)
        part = _dot(hid, wd)
        for i in range(cap // SUBLANES):
            for c in range(nt):
                r0 = (i * nt + c) * SUBLANES
                y_ref[b, 0, r0:r0 + SUBLANES, :] += part[i * SUBLANES:(i + 1) * SUBLANES, c * LANES:(c + 1) * LANES]


def _ffn(xs, gate_cols, w_gate, w_up, w_down, layer, tf):
    bsz, ne, cap, d = xs.shape
    ff = w_gate.shape[-1]
    tf = min(tf, ff)
    nt = d // LANES
    return pl.pallas_call(
        _ffn_kernel,
        grid=(ne, ff // tf),
        in_specs=[pl.BlockSpec((bsz, 1, cap, d), lambda e, f: (0, e, 0, 0)),
                  pl.BlockSpec((bsz, cap, LANES), lambda e, f: (0, 0, 0)),
                  pl.BlockSpec((1, 1, d, tf), lambda e, f: (layer, e, 0, f)),
                  pl.BlockSpec((1, 1, d, tf), lambda e, f: (layer, e, 0, f)),
                  pl.BlockSpec((1, 1, tf, d), lambda e, f: (layer, e, f, 0))],
        out_specs=pl.BlockSpec((bsz, 1, cap * nt, LANES), lambda e, f: (0, e, 0, 0)),
        out_shape=jax.ShapeDtypeStruct((bsz, ne, cap * nt, LANES), F32),
        compiler_params=_cparams(("arbitrary", "arbitrary")),
        name="ffn",
    )(xs, gate_cols, w_gate, w_up, w_down)


SCATTER_SPLIT = 8


RESIDUAL_TILE = 256
RESIDUAL_LOOKAHEAD = 3


def _scatter_kernel(dst_ref, cut_ref, src_ref, y_ref, zero_hbm, *rest, cap, ne, nt, fused):
    if fused:
        x_hbm, nw_ref, g_ref, o_hbm = rest[:4]
        accs = rest[4:4 + SCATTER_SPLIT]
        sem, x_buf, o_buf, x_sem, o_sem = rest[4 + SCATTER_SPLIT:]
    else:
        o_hbm = rest[0]
        accs, sem = rest[1:1 + SCATTER_SPLIT], rest[1 + SCATTER_SPLIT]
    b, e = pl.program_id(0), pl.program_id(1)

    @pl.when(e == 0)
    def _():
        fills = [pltpu.make_async_copy(zero_hbm, acc, sem.at[k]) for k, acc in enumerate(accs)]
        for cp in fills:
            cp.start()
        for cp in fills:
            cp.wait()

    base = (b * ne + e) * cap
    cbase = (b * ne + e) * (SCATTER_SPLIT + 1)
    starts = [cut_ref[cbase + k] for k in range(SCATTER_SPLIT)]
    counts = [cut_ref[cbase + k + 1] - starts[k] for k in range(SCATTER_SPLIT)]
    shortest = functools.reduce(jnp.minimum, counts)
    longest = functools.reduce(jnp.maximum, counts)

    def update(acc, j, dst, scale=None):
        src = src_ref[j]
        row = y_ref[0, 0, pl.ds(src, nt, stride=SUBLANES), :]
        acc[pl.ds(dst, nt, stride=SUBLANES), :] += row if scale is None else scale * row

    def common(i, carry):
        for k, acc in enumerate(accs):
            j = starts[k] + i
            update(acc, j, dst_ref[base + j])
        return carry

    def tail(i, carry):
        for k, acc in enumerate(accs):
            valid = i < counts[k]
            j = jnp.minimum(starts[k] + i, cap - 1)
            update(acc, j, jnp.where(valid, dst_ref[base + j], 0), jnp.where(valid, 1.0, 0.0))
        return carry

    lax.fori_loop(0, shortest, common, 0)
    lax.fori_loop(shortest, longest, tail, 0)

    part = accs[0].shape[0]

    @pl.when(e == ne - 1)
    def _():
        if not fused:
            copies = [pltpu.make_async_copy(acc, o_hbm.at[b, pl.ds(k * part, part)], sem.at[k])
                      for k, acc in enumerate(accs)]
            for cp in copies:
                cp.start()
            for cp in copies:
                cp.wait()
            return
        tt = RESIDUAL_TILE
        d = nt * LANES
        per_acc = part // (tt * nt)
        n_tiles = SCATTER_SPLIT * per_acc
        nx = RESIDUAL_LOOKAHEAD + 1

        def x_copy(i):
            return pltpu.make_async_copy(x_hbm.at[b, pl.ds(i * tt, tt)], x_buf.at[i % nx], x_sem.at[i % nx])

        def o_copy(i):
            return pltpu.make_async_copy(o_buf.at[i % 2], o_hbm.at[b, pl.ds(i * tt, tt)], o_sem.at[i % 2])

        for i in range(min(RESIDUAL_LOOKAHEAD, n_tiles)):
            x_copy(i).start()
        for i in range(n_tiles):
            x_copy(i).wait()
            if i + RESIDUAL_LOOKAHEAD < n_tiles:
                x_copy(i + RESIDUAL_LOOKAHEAD).start()
            if i >= 2:
                o_copy(i - 2).wait()
            acc = accs[i // per_acc]
            moe = _load_tile_rows(acc.at[pl.ds((i % per_acc) * tt * nt, tt * nt)], tt, d)
            o_buf[i % 2] = x_buf[i % nx] + g_ref[0] * _rms(moe, nw_ref[...])
            o_copy(i).start()
        for i in range(max(n_tiles - 2, 0), n_tiles):
            o_copy(i).wait()


def _scatter(dst_flat, cuts_flat, y_rows, t, nt, residual=None):
    bsz, ne, rows, _ = y_rows.shape
    cap = rows // nt
    d = nt * LANES
    part_rows = (t // SCATTER_SPLIT) * nt
    zeros = jnp.zeros((part_rows, LANES), F32)
    fused = residual is not None
    in_specs = [pl.BlockSpec((1, 1, rows, LANES), lambda b, e, i, c, s: (b, e, 0, 0)),
                pl.BlockSpec(memory_space=pl.ANY)]
    scratch = [pltpu.VMEM((part_rows, LANES), F32)] * SCATTER_SPLIT + [pltpu.SemaphoreType.DMA((SCATTER_SPLIT,))]
    args = (dst_flat, cuts_flat, _tile_row_base(jnp.arange(cap, dtype=I32), nt), y_rows, zeros)
    if fused:
        assert (t // SCATTER_SPLIT) % RESIDUAL_TILE == 0
        in_specs += [pl.BlockSpec(memory_space=pl.ANY), pl.BlockSpec((1, d), lambda b, e, i, c, s: (0, 0)),
                     pl.BlockSpec((1, 1, d), lambda b, e, i, c, s: (b, 0, 0))]
        scratch += [pltpu.VMEM((RESIDUAL_LOOKAHEAD + 1, RESIDUAL_TILE, d), F32), pltpu.VMEM((2, RESIDUAL_TILE, d), F32),
                    pltpu.SemaphoreType.DMA((RESIDUAL_LOOKAHEAD + 1,)), pltpu.SemaphoreType.DMA((2,))]
        args += residual
        out_shape = jax.ShapeDtypeStruct((bsz, t, d), F32)
    else:
        out_shape = jax.ShapeDtypeStruct((bsz, t * nt, LANES), F32)
    return pl.pallas_call(
        functools.partial(_scatter_kernel, cap=cap, ne=ne, nt=nt, fused=fused),
        grid_spec=pltpu.PrefetchScalarGridSpec(
            num_scalar_prefetch=3,
            grid=(bsz, ne),
            in_specs=in_specs,
            out_specs=pl.BlockSpec(memory_space=pl.ANY),
            scratch_shapes=scratch,
        ),
        out_shape=out_shape,
        compiler_params=_cparams(("arbitrary", "arbitrary")),
        name="scatter",
    )(*args)


def _moe(h2_rows, logits, ne, w_gate, w_up, w_down, layer, tf, residual=None):
    bsz, t, _ = logits.shape
    nt = h2_rows.shape[1] // t
    cap = EC_CAPACITY_FACTOR * t // ne
    tpart = t // SCATTER_SPLIT
    idx_cols, gate_cols = _route(logits, ne, cap)
    idx = jnp.swapaxes(idx_cols[:, :, :ne], 1, 2)
    xs = _gather(_tile_row_base(idx, nt).reshape(-1), h2_rows, ne, cap, t)
    y = _ffn(xs, gate_cols, w_gate, w_up, w_down, layer, tf)
    edges = jnp.arange(SCATTER_SPLIT + 1, dtype=I32) * tpart
    cuts = jnp.sum(idx[..., None] < edges, axis=2).astype(I32)
    return _scatter(_tile_row_base(idx % tpart, nt).reshape(-1), cuts.reshape(-1), y, t, nt, residual)


def _sincos_tables(rows, dim):
    quarter = dim // 4
    freqs = jnp.exp(-math.log(10000.0) * jnp.arange(quarter, dtype=F32) / quarter)

    def emb1d(n):
        ang = jnp.arange(n, dtype=F32)[:, None] * freqs[None, :]
        return jnp.concatenate([jnp.sin(ang), jnp.cos(ang)], axis=-1)

    return emb1d(rows), emb1d(GRID_W)


TOKEN_TILE = 256
FFN_TILE = 512


def kernel(x, c, ctx, c_ctx, w_ada, b_ada, norm_w, hg_w_in, hg_lb, hg_gnorm, hg_w_out, sg_w_in, sg_ln_w, sg_ln_b,
           sg_w_s, sg_b_s, sg_w_out, moe_router, moe_w_gate, moe_w_up, moe_w_down):
    bsz, t, d = x.shape
    depth = w_ada.shape[0]
    assert depth == 2 and d == N_HEADS * HEAD_DIM and t % GLA_CHUNK == 0 and ctx.shape[1] % GLA_CHUNK == 0
    tm = min(TOKEN_TILE, t)

    cvec = jnp.zeros((SUBLANES, d), F32).at[:bsz].set(c).at[bsz].set(c_ctx)
    mod = _ada(cvec, w_ada, b_ada)

    def mods(layer, rows):
        m = mod[layer, rows].reshape(-1, N_ADA, 1, d)
        return [m[:, k] for k in range(N_ADA)]

    nw = norm_w.reshape(depth, 4, 1, d)
    ne = moe_router.shape[-1]
    router = jnp.zeros((depth, d, LANES), F32).at[:, :, :ne].set(moe_router)
    r_hi = router.astype(BF16)
    r_lo = (router - r_hi.astype(F32)).astype(BF16)

    sh1, sc1, g1, sh2, sc2, g2 = mods(0, slice(0, bsz))
    sh1c, sc1c = [jnp.broadcast_to(m, (bsz, 1, d)) for m in mods(0, slice(bsz, bsz + 1))[:2]]
    w_in_bf = hg_w_in[0].astype(BF16)
    pos = _sincos_tables(t // GRID_W, d)
    tri, masks = _gla_consts()

    cv, ckf, cgf, ckb, cgb, cq, _ = _hg_in(ctx, None, nw[0, 0], sh1c, sc1c, w_in_bf, hg_lb, 1, tm)
    s_zero = jnp.zeros((bsz, 2, N_HEADS, HEAD_DIM, HEAD_DIM), F32)
    _, _, s_ctx = _gla(ckf, cgf, ckb, cgb, cv, cq, s_zero, tri, masks)

    v, kf, gf, kb, gb, q, sg = _hg_in(x, pos, nw[0, 0], sh1, sc1, w_in_bf, hg_lb, 1, tm)
    o_f, o_b, _ = _gla(kf, gf, kb, gb, v, q, s_ctx, tri, masks)
    x1, h2, lg = _hg_out(o_f, o_b, sg, hg_gnorm[0:1], hg_w_out[0].astype(BF16), x, pos, nw[0, 1], g1, nw[0, 2],
                         sh2, sc2, r_hi[0], r_lo[0], tm)
    moe = _moe(h2, lg, ne, moe_w_gate, moe_w_up, moe_w_down, 0, FFN_TILE)

    sh1b, sc1b, g1b, sh2b, sc2b, g2b = mods(1, slice(0, bsz))
    x1b, h2b, lgb = _sgu(x1, moe, nw[0, 3], g2, nw[1, 0], sh1b, sc1b, sg_w_in[0].astype(BF16),
                         sg_ln_w[0:1], sg_ln_b[0:1], sg_w_s[0].astype(BF16), sg_b_s[0].T, sg_w_out[0].astype(BF16),
                         nw[1, 1], g1b, nw[1, 2], sh2b, sc2b, r_hi[1], r_lo[1], tm)
    return _moe(h2b, lgb, ne, moe_w_gate, moe_w_up, moe_w_down, 1, FFN_TILE, residual=(x1b, nw[1, 3], g2b))
```

```python
import functools
import math

import jax
import jax.numpy as jnp
from jax import lax
from jax.experimental import pallas as pl
from jax.experimental.pallas import tpu as pltpu

F32 = jnp.float32
BF16 = jnp.bfloat16
I32 = jnp.int32
HIGHEST = lax.Precision.HIGHEST

EPS = 1e-6
GRID_W = 64
N_ADA = 6
N_HEADS = 8
HEAD_DIM = 128
EC_CAPACITY_FACTOR = 2
SGU_CHUNK = 128
SGU_GROUPS = 8

LANES = 128
SUBLANES = 8
GLA_CHUNK = 128
GLA_LEVELS = (64, 32, 16, 8, 4, 2, 1)
GLA_SHORT_BLOCK = 32
GLA_SHORT_MAX_EXPONENT = 60.0
BISECT_ITERS = 160
VMEM_LIMIT = 52 * 1024 * 1024


def _cparams(sem):
    return pltpu.CompilerParams(dimension_semantics=sem, vmem_limit_bytes=VMEM_LIMIT)


def _rms(x, w):
    ms = jnp.mean(x * x, axis=-1, keepdims=True)
    return x * lax.rsqrt(ms + EPS) * w


def _dot(a, b):
    return jnp.dot(a, b, preferred_element_type=F32)


def _dot_nt(a, b):
    return lax.dot_general(a, b, (((1,), (1,)), ((), ())), preferred_element_type=F32)


def _dot_tn(a, b):
    return lax.dot_general(a, b, (((0,), (0,)), ((), ())), preferred_element_type=F32)


def _ada_kernel(c_ref, w_ref, b_ref, o_ref):
    c = c_ref[...]
    s = c * jax.nn.sigmoid(c)
    o_ref[0] = jnp.dot(s, w_ref[0], precision=HIGHEST, preferred_element_type=F32) + b_ref[0]


def _ada(cvec, w_ada, b_ada):
    depth, d, nd = w_ada.shape
    rows = cvec.shape[0]
    return pl.pallas_call(
        _ada_kernel,
        grid=(depth, nd // d),
        in_specs=[pl.BlockSpec((rows, d), lambda l, n: (0, 0)),
                  pl.BlockSpec((1, d, d), lambda l, n: (l, 0, n)),
                  pl.BlockSpec((1, 1, d), lambda l, n: (l, 0, n))],
        out_specs=pl.BlockSpec((1, rows, d), lambda l, n: (l, 0, n)),
        out_shape=jax.ShapeDtypeStruct((depth, rows, nd), F32),
        compiler_params=_cparams(("arbitrary", "arbitrary")),
        name="ada",
    )(cvec, w_ada, b_ada.reshape(depth, 1, nd))


def _hgin_body(x, nw_ref, sh_ref, sc_ref, w_ref, lb_ref, outs, n_lb):
    v_ref, kf_ref, gf_ref, kb_ref, gb_ref, q_ref, sg_ref = outs
    d = x.shape[-1]
    h = _rms(x, nw_ref[...]) * (1.0 + sc_ref[0]) + sh_ref[0]
    hb = h.astype(BF16)
    lbs = lb_ref[...]
    e = jnp.exp(lbs - jnp.max(lbs, axis=0, keepdims=True))
    lb = jnp.sum(e[:n_lb], axis=0) / jnp.sum(e, axis=0)

    v_ref[0] = _dot(hb, w_ref[:, 0:d]).astype(BF16)
    for j, (k_ref, g_ref) in enumerate(((kf_ref, gf_ref), (kb_ref, gb_ref))):
        raw = _dot(hb, w_ref[:, (1 + j) * d:(2 + j) * d])
        lbj = lb[j:j + 1]
        sig = jax.nn.sigmoid(raw)
        f = lbj + (1.0 - lbj) * sig
        k_ref[0] = ((1.0 - lbj) * (1.0 - sig)).astype(BF16)
        g_ref[0] = jnp.log(f)
    qr = _dot(hb, w_ref[:, 3 * d:4 * d])
    q_ref[0] = (qr * jax.nn.sigmoid(qr)).astype(BF16)
    gr = _dot(hb, w_ref[:, 4 * d:5 * d])
    sg_ref[0] = (gr * jax.nn.sigmoid(gr)).astype(BF16)


def _with_pos(x, er_ref, ec_ref):
    er, ec = er_ref[0], ec_ref[...]
    pos = jnp.concatenate([jnp.concatenate([jnp.broadcast_to(er[r:r + 1], ec.shape), ec], axis=1)
                           for r in range(er.shape[0])], axis=0)
    return x + pos


def _pos_specs(pos, tm, d):
    er, ec = pos
    assert tm % GRID_W == 0
    rpt = tm // GRID_W
    specs = [pl.BlockSpec((1, rpt, d // 2), lambda b, i: (i, 0, 0)), pl.BlockSpec(ec.shape, lambda b, i: (0, 0))]
    return specs, (er.reshape(-1, rpt, d // 2), ec)


def _hgin_pos_kernel(x_ref, er_ref, ec_ref, nw_ref, sh_ref, sc_ref, w_ref, lb_ref, *outs, n_lb):
    _hgin_body(_with_pos(x_ref[0], er_ref, ec_ref), nw_ref, sh_ref, sc_ref, w_ref, lb_ref, outs, n_lb)


def _hgin_kernel(x_ref, nw_ref, sh_ref, sc_ref, w_ref, lb_ref, *outs, n_lb):
    _hgin_body(x_ref[0], nw_ref, sh_ref, sc_ref, w_ref, lb_ref, outs, n_lb)


def _hg_in(x, pos, nw, sh, sc, w_bf, lb_raw, n_lb, tm):
    bsz, t, d = x.shape
    tm = min(tm, t)
    tok = pl.BlockSpec((1, tm, d), lambda b, i: (b, i, 0))
    vec = pl.BlockSpec((1, d), lambda b, i: (0, 0))
    bvec = pl.BlockSpec((1, 1, d), lambda b, i: (b, 0, 0))
    wspec = pl.BlockSpec(w_bf.shape, lambda b, i: (0, 0))
    lbspec = pl.BlockSpec(lb_raw.shape, lambda b, i: (0, 0, 0))
    gate_shapes = [jax.ShapeDtypeStruct((bsz, t, d), dt) for dt in (BF16, BF16, F32, BF16, F32, BF16, BF16)]
    if pos is not None:
        kern = functools.partial(_hgin_pos_kernel, n_lb=n_lb)
        pspecs, pargs = _pos_specs(pos, tm, d)
        in_specs = [tok] + pspecs + [vec, bvec, bvec, wspec, lbspec]
        args = (x,) + pargs + (nw, sh, sc, w_bf, lb_raw)
        out_shape = gate_shapes
    else:
        kern = functools.partial(_hgin_kernel, n_lb=n_lb)
        in_specs = [tok, vec, bvec, bvec, wspec, lbspec]
        args = (x, nw, sh, sc, w_bf, lb_raw)
        out_shape = gate_shapes
    return pl.pallas_call(
        kern,
        grid=(bsz, t // tm),
        in_specs=in_specs,
        out_specs=[tok] * len(out_shape),
        out_shape=out_shape,
        compiler_params=_cparams(("arbitrary", "arbitrary")),
        name="hg_in",
    )(*args)


def _gla_consts():
    c = GLA_CHUNK
    t = jnp.arange(c)[:, None]
    s = jnp.arange(c)[None, :]
    tri = jnp.stack([(s <= t), (s >= t)]).astype(BF16)
    masks = []
    for rev in (False, True):
        lv = []
        for m in GLA_LEVELS:
            same = (t // (2 * m)) == (s // (2 * m))
            tq = ((t // m) % 2) == (0 if rev else 1)
            sk = ((s // m) % 2) == (1 if rev else 0)
            lv.append(same & tq & sk)
        lv.append(t == s)
        pair = 2 * GLA_SHORT_BLOCK
        lv.append(((t // pair) == (s // pair)) & ((s >= t) if rev else (s <= t)))
        masks.append(jnp.stack(lv))
    return tri, jnp.stack(masks).astype(F32)


def _split3(x):
    p0 = x.astype(BF16)
    r1 = x - p0.astype(F32)
    p1 = r1.astype(BF16)
    p2 = (r1 - p1.astype(F32)).astype(BF16)
    return p0, p1, p2


def _level_ref(b_scr, hs, m, rev):
    c = GLA_CHUNK
    off = m if rev else m - 1

    def row8(i):
        return jnp.broadcast_to(b_scr[pl.ds(i, 1), hs], (SUBLANES, HEAD_DIM))

    pieces = []
    if m >= SUBLANES:
        for blk in range(c // (2 * m)):
            r8 = row8(blk * 2 * m + off)
            pieces.extend([r8] * (2 * m // SUBLANES))
    else:
        sub = lax.broadcasted_iota(I32, (SUBLANES, HEAD_DIM), 0) // (2 * m)
        for grp in range(c // SUBLANES):
            piece = row8(grp * SUBLANES + off)
            for cls in range(1, SUBLANES // (2 * m)):
                piece = jnp.where(sub == cls, row8(grp * SUBLANES + cls * 2 * m + off), piece)
            pieces.append(piece)
    return jnp.concatenate(pieces, axis=0)


def _block_decay_bound(b_scr, rev):
    c, blk = GLA_CHUNK, GLA_SHORT_BLOCK
    worst = None
    for i in range(c // blk):
        inner = b_scr[pl.ds(i * blk if rev else (i + 1) * blk - 1, 1), :]
        r = (i + 1) * blk if rev else i * blk - 1
        span = jnp.abs(inner - b_scr[pl.ds(r, 1), :]) if 0 <= r < c else jnp.abs(inner)
        worst = span if worst is None else jnp.maximum(worst, span)
    return jnp.max(worst)


def _gla_head(q_ref, k_ref, v_ref, o_ref, mask_ref, st_scr, b_scr, d_idx, rev, h, short):
    c = GLA_CHUNK
    n_lv = len(GLA_LEVELS)
    hs = pl.ds(pl.multiple_of(h * HEAD_DIM, HEAD_DIM), HEAD_DIM)
    b = b_scr[:, hs]
    q = q_ref[0, :, hs]
    k = k_ref[0, :, hs]
    vb = v_ref[0, :, hs]
    bl = b_scr[pl.ds(0 if rev else c - 1, 1), hs]
    st = st_scr[d_idx, h]

    if short:
        blk = 2 * GLA_SHORT_BLOCK
        nb = c // blk
        own = [b_scr[pl.ds(i * blk + (blk // 2 if rev else blk // 2 - 1), 1), hs] for i in range(nb)]

        def blockwise(rows):
            return jnp.concatenate([jnp.broadcast_to(r, (blk, HEAD_DIM)) for r in rows], axis=0)

        u = blockwise(own) - b
        qh = q * jnp.exp(-u).astype(BF16)
        kh = k * jnp.exp(u).astype(BF16)
        qt = qh * blockwise([jnp.exp(r) for r in own]).astype(BF16)
        kt = kh * blockwise([jnp.exp(bl - r) for r in own]).astype(BF16)
        a = jnp.where(mask_ref[d_idx, n_lv + 1] > 0.0, _dot_nt(qh, kh), 0.0)
        zero = jnp.zeros((1, HEAD_DIM), F32)
        for li, m in enumerate(GLA_LEVELS):
            if m < blk:
                continue
            qf, kf = [], []
            for i in range(nb):
                lvl = b_scr[pl.ds((i * blk) // (2 * m) * 2 * m + (m if rev else m - 1), 1), hs]
                is_query = ((i * blk) // m) % 2 == (0 if rev else 1)
                qf.append(jnp.exp(own[i] - lvl) if is_query else zero)
                kf.append(zero if is_query else jnp.exp(lvl - own[i]))
            a = a + mask_ref[d_idx, li] * _dot_nt(qh * blockwise(qf).astype(BF16), kh * blockwise(kf).astype(BF16))
    else:
        qt = q * jnp.exp(b).astype(BF16)
        kt = k * jnp.exp(bl - b).astype(BF16)

        def level(m):
            e = jnp.exp(-jnp.abs(b - _level_ref(b_scr, hs, m, rev))).astype(BF16)
            return _dot_nt(q * e, k * e)

        a = mask_ref[d_idx, n_lv] * _dot_nt(q, k)
        for li, m in enumerate(GLA_LEVELS):
            a = a + mask_ref[d_idx, li] * level(m)
    o_ref[0, :, hs] = (_dot_nt(qt, st.astype(BF16)) + _dot(a.astype(BF16), vb)).astype(o_ref.dtype)
    st_scr[d_idx, h] = st * jnp.exp(bl) + _dot_tn(vb, kt)


def _gla_kernel(kf_ref, gf_ref, vf_ref, qf_ref, kb_ref, gb_ref, vb_ref, qb_ref, s0_ref, tri_ref, mask_ref,
                of_ref, ob_ref, sout_ref, st_scr, bf_scr, bb_scr):
    n = pl.program_id(1)

    @pl.when(n == 0)
    def _():
        st_scr[...] = s0_ref[0]

    dirs = ((qf_ref, kf_ref, vf_ref, gf_ref, of_ref, bf_scr, 0, False),
            (qb_ref, kb_ref, vb_ref, gb_ref, ob_ref, bb_scr, 1, True))
    bounded = []
    for q_ref, k_ref, v_ref, g_ref, o_ref, b_scr, d_idx, rev in dirs:
        p0, p1, p2 = _split3(g_ref[0])
        tri = tri_ref[d_idx]
        b_scr[...] = _dot(tri, p0) + _dot(tri, p1) + _dot(tri, p2)
        bounded.append(_block_decay_bound(b_scr, rev) <= GLA_SHORT_MAX_EXPONENT)

    def heads(which, short, unroll):
        def body(h, carry):
            for q_ref, k_ref, v_ref, _, o_ref, b_scr, d_idx, rev in which:
                _gla_head(q_ref, k_ref, v_ref, o_ref, mask_ref, st_scr, b_scr, d_idx, rev, h, short)
            return carry
        lax.fori_loop(0, N_HEADS, body, 0, unroll=unroll)

    both = jnp.logical_and(bounded[0], bounded[1])

    @pl.when(both)
    def _():
        heads(dirs, True, 8)

    for d, ok in zip(dirs, bounded):
        @pl.when(jnp.logical_and(jnp.logical_not(both), ok))
        def _():
            heads((d,), True, 2)

        @pl.when(jnp.logical_not(ok))
        def _():
            heads((d,), False, 2)

    @pl.when(n == pl.num_programs(1) - 1)
    def _():
        sout_ref[0] = st_scr[...]


def _gla(kf, gf, kb, gb, v, q, s0, tri, masks):
    bsz, t, d = v.shape
    c = GLA_CHUNK
    n = t // c
    fwd = pl.BlockSpec((1, c, d), lambda b, i: (b, i, 0))
    bwd = pl.BlockSpec((1, c, d), lambda b, i: (b, n - 1 - i, 0))
    sspec = pl.BlockSpec((1,) + s0.shape[1:], lambda b, i: (b, 0, 0, 0, 0))
    return pl.pallas_call(
        _gla_kernel,
        grid=(bsz, n),
        in_specs=[fwd, fwd, fwd, fwd, bwd, bwd, bwd, bwd, sspec,
                  pl.BlockSpec(tri.shape, lambda b, i: (0, 0, 0)),
                  pl.BlockSpec(masks.shape, lambda b, i: (0, 0, 0, 0))],
        out_specs=[fwd, bwd, sspec],
        out_shape=[jax.ShapeDtypeStruct((bsz, t, d), BF16), jax.ShapeDtypeStruct((bsz, t, d), BF16),
                   jax.ShapeDtypeStruct(s0.shape, F32)],
        scratch_shapes=[pltpu.VMEM(s0.shape[1:], F32), pltpu.VMEM((c, d), F32), pltpu.VMEM((c, d), F32)],
        compiler_params=_cparams(("arbitrary", "arbitrary")),
        name="gla",
    )(kf, gf, v, q, kb, gb, v, q, s0, tri, masks)


def _store_tile_rows(ref2, x):
    n, d = x.shape
    nt = d // LANES
    for i in range(n // SUBLANES):
        for c in range(nt):
            r0 = (i * nt + c) * SUBLANES
            ref2[r0:r0 + SUBLANES, :] = x[i * SUBLANES:(i + 1) * SUBLANES, c * LANES:(c + 1) * LANES]


def _load_tile_rows(ref2, n, d):
    nt = d // LANES
    rows = []
    for i in range(n // SUBLANES):
        rows.append(jnp.concatenate(
            [ref2[(i * nt + c) * SUBLANES:(i * nt + c + 1) * SUBLANES, :] for c in range(nt)], axis=1))
    return jnp.concatenate(rows, axis=0)


def _tile_row_base(t, nt):
    return (t >> 3) * (nt * SUBLANES) + (t & (SUBLANES - 1))


def _tile_rows_spec(tm, d, index_map):
    return pl.BlockSpec((1, tm * (d // LANES), LANES), index_map)


def _post_mixer(y, xres, rows, nw1_ref, g1_ref, nw2_ref, sh2_ref, sc2_ref, rhi_ref, rlo_ref, x1_ref, h2_ref, lg_ref):
    d = y.shape[-1]
    nt = d // LANES
    x1 = xres + g1_ref[0] * _rms(y, nw1_ref[...])
    x1_ref[0, rows, :] = x1
    h2 = _rms(x1, nw2_ref[...]) * (1.0 + sc2_ref[0]) + sh2_ref[0]
    _store_tile_rows(h2_ref.at[0, pl.ds(rows.start * nt, (rows.stop - rows.start) * nt)], h2)
    h_hi = h2.astype(BF16)
    h_lo = (h2 - h_hi.astype(F32)).astype(BF16)
    lg_ref[0, rows, :] = _dot(h_hi, rhi_ref[...]) + (_dot(h_lo, rhi_ref[...]) + _dot(h_hi, rlo_ref[...]))


def _hgout_kernel(of_ref, ob_ref, sg_ref, gn_ref, w_ref, x_ref, er_ref, ec_ref, nw1_ref, g1_ref, nw2_ref, sh2_ref,
                  sc2_ref, rhi_ref, rlo_ref, x1_ref, h2_ref, lg_ref, z_scr):
    o = of_ref[0].astype(F32) + ob_ref[0].astype(F32)
    for h in range(N_HEADS):
        hs = slice(h * HEAD_DIM, (h + 1) * HEAD_DIM)
        oh = o[:, hs]
        ms = jnp.mean(oh * oh, axis=-1, keepdims=True)
        z = oh * lax.rsqrt(ms + EPS) * gn_ref[:, hs] * sg_ref[0, :, hs].astype(F32)
        z_scr[:, hs] = z.astype(BF16)
    y = _dot(z_scr[...], w_ref[...])
    _post_mixer(y, _with_pos(x_ref[0], er_ref, ec_ref), slice(0, y.shape[0]), nw1_ref, g1_ref, nw2_ref, sh2_ref,
                sc2_ref, rhi_ref, rlo_ref, x1_ref, h2_ref, lg_ref)


def _hg_out(o_f, o_b, sg, gnorm, w_out_bf, x, pos, nw1, g1, nw2, sh2, sc2, r_hi, r_lo, tm):
    bsz, t, d = x.shape
    rspec = pl.BlockSpec((d, LANES), lambda b, i: (0, 0))
    pspecs, pargs = _pos_specs(pos, tm, d)
    tok = pl.BlockSpec((1, tm, d), lambda b, i: (b, i, 0))
    vec = pl.BlockSpec((1, d), lambda b, i: (0, 0))
    bvec = pl.BlockSpec((1, 1, d), lambda b, i: (b, 0, 0))
    return pl.pallas_call(
        _hgout_kernel,
        grid=(bsz, t // tm),
        in_specs=[tok, tok, tok, vec, pl.BlockSpec((d, d), lambda b, i: (0, 0)), tok] + pspecs
        + [vec, bvec, vec, bvec, bvec, rspec, rspec],
        out_specs=[tok, _tile_rows_spec(tm, d, lambda b, i: (b, i, 0)),
                   pl.BlockSpec((1, tm, LANES), lambda b, i: (b, i, 0))],
        out_shape=[jax.ShapeDtypeStruct((bsz, t, d), F32), jax.ShapeDtypeStruct((bsz, t * (d // LANES), LANES), F32),
                   jax.ShapeDtypeStruct((bsz, t, LANES), F32)],
        scratch_shapes=[pltpu.VMEM((tm, d), BF16)],
        compiler_params=_cparams(("arbitrary", "arbitrary")),
        name="hg_out",
    )(o_f, o_b, sg, gnorm, w_out_bf, x, *pargs, nw1, g1, nw2, sh2, sc2, r_hi, r_lo)


def _sgu_kernel(x1p_ref, moe_ref, nw3p_ref, g2p_ref, nw0_ref, sh1_ref, sc1_ref, win_ref, lnw_ref, lnb_ref,
                ws_ref, bs_ref, wout_ref, nw1_ref, g1_ref, nw2_ref, sh2_ref, sc2_ref, rhi_ref, rlo_ref,
                x1_ref, h2_ref, lg_ref):
    tm, d = x1p_ref.shape[1], x1p_ref.shape[2]
    w = lnw_ref.shape[-1]
    gd = w // SGU_GROUPS
    nt = d // LANES

    def gelu(z):
        return 0.5 * z * (1.0 + jnp.tanh(math.sqrt(2.0 / math.pi) * (z + 0.044715 * (z * z * z))))

    for ck in range(tm // SGU_CHUNK):
        rs = slice(ck * SGU_CHUNK, (ck + 1) * SGU_CHUNK)
        moe = _load_tile_rows(moe_ref.at[0, pl.ds(ck * SGU_CHUNK * nt, SGU_CHUNK * nt)], SGU_CHUNK, d)
        x = x1p_ref[0, rs, :] + g2p_ref[0] * _rms(moe, nw3p_ref[...])
        h = _rms(x, nw0_ref[...]) * (1.0 + sc1_ref[0]) + sh1_ref[0]
        hb = h.astype(BF16)
        u = gelu(_dot(hb, win_ref[:, 0:w]))
        v = gelu(_dot(hb, win_ref[:, w:2 * w]))
        mu = jnp.mean(v, axis=-1, keepdims=True)
        vc = v - mu
        vn = vc * lax.rsqrt(jnp.mean(vc * vc, axis=-1, keepdims=True) + EPS) * lnw_ref[...] + lnb_ref[...]
        vnb = vn.astype(BF16)
        gated = []
        for g in range(SGU_GROUPS):
            cs = slice(g * gd, (g + 1) * gd)
            mixed = _dot(ws_ref[g], vnb[:, cs]) + bs_ref[:, g:g + 1]
            gated.append((u[:, cs] * mixed).astype(BF16))
        y = _dot(jnp.concatenate(gated, axis=1), wout_ref[...])
        _post_mixer(y, x, rs, nw1_ref, g1_ref, nw2_ref, sh2_ref, sc2_ref, rhi_ref, rlo_ref, x1_ref, h2_ref, lg_ref)


def _sgu(x1p, moe, nw3p, g2p, nw0, sh1, sc1, w_in_bf, ln_w, ln_b, w_s_bf, b_s_t, w_out_bf,
         nw1, g1, nw2, sh2, sc2, r_hi, r_lo, tm):
    bsz, t, d = x1p.shape
    w = ln_w.shape[-1]
    tok = pl.BlockSpec((1, tm, d), lambda b, i: (b, i, 0))
    tiles = _tile_rows_spec(tm, d, lambda b, i: (b, i, 0))
    vec = pl.BlockSpec((1, d), lambda b, i: (0, 0))
    wvec = pl.BlockSpec((1, w), lambda b, i: (0, 0))
    bvec = pl.BlockSpec((1, 1, d), lambda b, i: (b, 0, 0))

    def full(a):
        return pl.BlockSpec(a.shape, lambda b, i: (0,) * a.ndim)

    return pl.pallas_call(
        _sgu_kernel,
        grid=(bsz, t // tm),
        in_specs=[tok, tiles, vec, bvec, vec, bvec, bvec, full(w_in_bf), wvec, wvec, full(w_s_bf), full(b_s_t),
                  full(w_out_bf), vec, bvec, vec, bvec, bvec, full(r_hi), full(r_lo)],
        out_specs=[tok, tiles, pl.BlockSpec((1, tm, LANES), lambda b, i: (b, i, 0))],
        out_shape=[jax.ShapeDtypeStruct((bsz, t, d), F32), jax.ShapeDtypeStruct((bsz, t * (d // LANES), LANES), F32),
                   jax.ShapeDtypeStruct((bsz, t, LANES), F32)],
        compiler_params=_cparams(("arbitrary", "arbitrary")),
        name="sgu",
    )(x1p, moe, nw3p, g2p, nw0, sh1, sc1, w_in_bf, ln_w, ln_b, w_s_bf, b_s_t, w_out_bf,
      nw1, g1, nw2, sh2, sc2, r_hi, r_lo)


def _token_prefix(mask, triu, slow):
    local = _dot(mask.astype(BF16), triu)
    rowtot = jnp.broadcast_to(local[:, LANES - 1:LANES], local.shape)
    prev = _dot(slow, rowtot.astype(BF16))
    return local, prev, rowtot


def _route_kernel(lg_ref, triu_ref, slow_ref, idx_ref, gate_ref, a_scr, thr_scr, *, cap):
    ne, nr = a_scr.shape[0], a_scr.shape[1]
    for r in range(nr):
        a_scr[:, r, :] = lg_ref[0, r * LANES:(r + 1) * LANES, :].T[0:ne, :]
    lg = a_scr[...]
    ex = jnp.exp(lg - jnp.max(lg, axis=0, keepdims=True))
    a = ex / jnp.sum(ex, axis=0, keepdims=True)
    a_scr[...] = a
    capf = jnp.float32(cap)

    def count(m):
        return jnp.sum(jnp.sum(m.astype(F32), axis=1, keepdims=True), axis=2, keepdims=True)

    def unresolved(state):
        it, lo, hi = state
        mid = 0.5 * (lo + hi)
        return jnp.logical_and(it < BISECT_ITERS, jnp.max(((mid != lo) & (mid != hi)).astype(F32)) > 0.0)

    def bis(state):
        it, lo, hi = state
        mid = 0.5 * (lo + hi)
        ge = count(a >= mid) >= capf
        return it + 1, jnp.where(ge, mid, lo), jnp.where(ge, hi, mid)

    _, lo, _ = lax.while_loop(unresolved, bis, (jnp.int32(0), jnp.zeros((ne, 1, 1), F32), jnp.full((ne, 1, 1), 2.0, F32)))
    thr_scr[...] = jnp.broadcast_to(lo, thr_scr.shape)

    idx_ref[0] = jnp.zeros(idx_ref.shape[1:], I32)
    gate_ref[0] = jnp.zeros(gate_ref.shape[1:], F32)
    triu = triu_ref[...]
    slow = slow_ref[...]
    lane = lax.broadcasted_iota(I32, (LANES, LANES), 1)
    lane_f = lane.astype(F32)
    sub_f = lax.broadcasted_iota(I32, (LANES, LANES), 0).astype(F32)
    rowid = lax.broadcasted_iota(I32, (nr, LANES), 0).astype(F32)

    def per_expert(e, carry):
        ae = a_scr[e]
        v = thr_scr[e][0:1, :]
        gt = ae > v
        eq = ae == v
        need = capf - jnp.sum(jnp.sum(gt.astype(F32), axis=0, keepdims=True), axis=1, keepdims=True)
        eql, eqp, _ = _token_prefix(eq, triu, slow)
        sel = gt | (eq & ((eql + eqp - eq.astype(F32)) < need))
        local, prev, rowtot = _token_prefix(sel, triu, slow)
        rowcum = prev + rowtot
        prev_hi = jnp.floor(prev * (1.0 / LANES))
        a0, a1, a2 = _split3(ae)
        rhs = jnp.concatenate([piece.astype(BF16) for piece in
                               (local, prev_hi, prev - LANES * prev_hi, rowid, a0, a1, a2)], axis=1)

        for p in range(cap // LANES):
            base = float(p * LANES + 1)
            slot_row = base + lane_f[0:1, :]
            onehot_t = ((prev < slot_row) & (rowcum >= slot_row)).astype(BF16)
            g = _dot_tn(onehot_t, rhs)
            g_local = g[:, 0:LANES]
            g_prev = LANES * g[:, LANES:2 * LANES] + g[:, 2 * LANES:3 * LANES]
            g_row = g[:, 3 * LANES:4 * LANES]
            g_a = g[:, 4 * LANES:5 * LANES] + g[:, 5 * LANES:6 * LANES] + g[:, 6 * LANES:7 * LANES]
            slot_col = base + sub_f
            lstar = jnp.sum(((g_local + g_prev) < slot_col).astype(F32), axis=1, keepdims=True)
            tok = LANES * g_row[:, 0:1] + lstar
            gat = jnp.sum(jnp.where(lane_f == lstar, g_a, 0.0), axis=1, keepdims=True)
            rs = slice(p * LANES, (p + 1) * LANES)
            idx_ref[0, rs, :] = jnp.where(lane == e, tok.astype(I32), idx_ref[0, rs, :])
            gate_ref[0, rs, :] = jnp.where(lane == e, gat, gate_ref[0, rs, :])
        return carry

    lax.fori_loop(0, ne, per_expert, 0)


def _route(logits, ne, cap):
    bsz, t, _ = logits.shape
    nr = t // LANES
    assert cap % LANES == 0 and nr % SUBLANES == 0
    triu = (jnp.arange(LANES)[:, None] <= jnp.arange(LANES)[None, :]).astype(BF16)
    slow = (jnp.arange(nr)[:, None] > jnp.arange(nr)[None, :]).astype(BF16)
    idx, gate = pl.pallas_call(
        functools.partial(_route_kernel, cap=cap),
        grid=(bsz,),
        in_specs=[pl.BlockSpec((1, t, LANES), lambda b: (b, 0, 0)),
                  pl.BlockSpec((LANES, LANES), lambda b: (0, 0)), pl.BlockSpec((nr, nr), lambda b: (0, 0))],
        out_specs=[pl.BlockSpec((1, cap, LANES), lambda b: (b, 0, 0))] * 2,
        out_shape=[jax.ShapeDtypeStruct((bsz, cap, LANES), I32), jax.ShapeDtypeStruct((bsz, cap, LANES), F32)],
        scratch_shapes=[pltpu.VMEM((ne, nr, LANES), F32), pltpu.VMEM((ne, SUBLANES, LANES), F32)],
        compiler_params=_cparams(("arbitrary",)),
        name="route",
    )(logits, triu, slow)
    return idx, gate


def _gather_kernel(src_ref, h_hbm, o_ref, h_scr, row_scr, sem, *, cap, ne, nt):
    b, e = pl.program_id(0), pl.program_id(1)

    @pl.when(e == 0)
    def _():
        cp = pltpu.make_async_copy(h_hbm.at[b], h_scr, sem.at[0])
        cp.start()
        cp.wait()

    base = (b * ne + e) * cap
    group = nt * SUBLANES

    def body(jj, carry):
        for u in range(SUBLANES):
            row_scr[pl.ds(jj * group + u, nt, stride=SUBLANES), :] = \
                h_scr[pl.ds(src_ref[base + jj * SUBLANES + u], nt, stride=SUBLANES), :]
        return carry

    lax.fori_loop(0, cap // SUBLANES, body, 0, unroll=2)

    def emit(g, carry):
        rows = [jnp.concatenate([row_scr[pl.ds((g * 2 + i) * group + c * SUBLANES, SUBLANES), :] for c in range(nt)],
                                axis=1) for i in range(2)]
        o_ref[0, 0, pl.ds(g * 2 * SUBLANES, 2 * SUBLANES), :] = jnp.concatenate(rows, axis=0).astype(BF16)
        return carry

    lax.fori_loop(0, cap // (2 * SUBLANES), emit, 0, unroll=4)


def _gather(src_flat, h2_rows, ne, cap, t):
    bsz, rows, _ = h2_rows.shape
    nt = rows // t
    return pl.pallas_call(
        functools.partial(_gather_kernel, cap=cap, ne=ne, nt=nt),
        grid_spec=pltpu.PrefetchScalarGridSpec(
            num_scalar_prefetch=1,
            grid=(bsz, ne),
            in_specs=[pl.BlockSpec(memory_space=pl.ANY)],
            out_specs=pl.BlockSpec((1, 1, cap, nt * LANES), lambda b, e, src: (b, e, 0, 0)),
            scratch_shapes=[pltpu.VMEM((rows, LANES), F32), pltpu.VMEM((cap * nt, LANES), F32),
                            pltpu.SemaphoreType.DMA((1,))],
        ),
        out_shape=jax.ShapeDtypeStruct((bsz, ne, cap, nt * LANES), BF16),
        compiler_params=_cparams(("arbitrary", "arbitrary")),
        name="gather",
    )(src_flat, h2_rows)


def _ffn_kernel(xs_ref, gate_ref, wg_ref, wu_ref, wd_ref, y_ref):
    e, f = pl.program_id(0), pl.program_id(1)
    cap, d = xs_ref.shape[2], xs_ref.shape[3]
    nt = d // LANES

    @pl.when(f == 0)
    def _():
        y_ref[...] = jnp.zeros(y_ref.shape, F32)

    wg = wg_ref[0, 0].astype(BF16)
    wu = wu_ref[0, 0].astype(BF16)
    wd = wd_ref[0, 0].astype(BF16)
    lane = lax.broadcasted_iota(I32, (cap, LANES), 1)
    for b in range(xs_ref.shape[0]):
        xb = xs_ref[b, 0]
        g = _dot(xb, wg)
        u = _dot(xb, wu)
        gate = jnp.sum(jnp.where(lane == e, gate_ref[b], 0.0), axis=1, keepdims=True)
        hid = (g * jax.nn.sigmoid(g) * u * gate).astype(BF16)
        part = _dot(hid, wd)
        for i in range(cap // SUBLANES):
            for c in range(nt):
                r0 = (i * nt + c) * SUBLANES
                y_ref[b, 0, r0:r0 + SUBLANES, :] += part[i * SUBLANES:(i + 1) * SUBLANES, c * LANES:(c + 1) * LANES]


def _ffn(xs, gate_cols, w_gate, w_up, w_down, layer, tf):
    bsz, ne, cap, d = xs.shape
    ff = w_gate.shape[-1]
    tf = min(tf, ff)
    nt = d // LANES
    return pl.pallas_call(
        _ffn_kernel,
        grid=(ne, ff // tf),
        in_specs=[pl.BlockSpec((bsz, 1, cap, d), lambda e, f: (0, e, 0, 0)),
                  pl.BlockSpec((bsz, cap, LANES), lambda e, f: (0, 0, 0)),
                  pl.BlockSpec((1, 1, d, tf), lambda e, f: (layer, e, 0, f)),
                  pl.BlockSpec((1, 1, d, tf), lambda e, f: (layer, e, 0, f)),
                  pl.BlockSpec((1, 1, tf, d), lambda e, f: (layer, e, f, 0))],
        out_specs=pl.BlockSpec((bsz, 1, cap * nt, LANES), lambda e, f: (0, e, 0, 0)),
        out_shape=jax.ShapeDtypeStruct((bsz, ne, cap * nt, LANES), F32),
        compiler_params=_cparams(("arbitrary", "arbitrary")),
        name="ffn",
    )(xs, gate_cols, w_gate, w_up, w_down)


SCATTER_SPLIT = 8


RESIDUAL_TILE = 256
RESIDUAL_LOOKAHEAD = 3


def _scatter_kernel(dst_ref, cut_ref, src_ref, y_ref, zero_hbm, *rest, cap, ne, nt, fused):
    if fused:
        x_hbm, nw_ref, g_ref, o_hbm = rest[:4]
        accs = rest[4:4 + SCATTER_SPLIT]
        sem, x_buf, o_buf, x_sem, o_sem = rest[4 + SCATTER_SPLIT:]
    else:
        o_hbm = rest[0]
        accs, sem = rest[1:1 + SCATTER_SPLIT], rest[1 + SCATTER_SPLIT]
    b, e = pl.program_id(0), pl.program_id(1)

    @pl.when(e == 0)
    def _():
        fills = [pltpu.make_async_copy(zero_hbm, acc, sem.at[k]) for k, acc in enumerate(accs)]
        for cp in fills:
            cp.start()
        for cp in fills:
            cp.wait()

    base = (b * ne + e) * cap
    cbase = (b * ne + e) * (SCATTER_SPLIT + 1)
    starts = [cut_ref[cbase + k] for k in range(SCATTER_SPLIT)]
    counts = [cut_ref[cbase + k + 1] - starts[k] for k in range(SCATTER_SPLIT)]
    shortest = functools.reduce(jnp.minimum, counts)
    longest = functools.reduce(jnp.maximum, counts)

    def update(acc, j, dst, scale=None):
        src = src_ref[j]
        row = y_ref[0, 0, pl.ds(src, nt, stride=SUBLANES), :]
        acc[pl.ds(dst, nt, stride=SUBLANES), :] += row if scale is None else scale * row

    def common(i, carry):
        for k, acc in enumerate(accs):
            j = starts[k] + i
            update(acc, j, dst_ref[base + j])
        return carry

    def tail(i, carry):
        for k, acc in enumerate(accs):
            valid = i < counts[k]
            j = jnp.minimum(starts[k] + i, cap - 1)
            update(acc, j, jnp.where(valid, dst_ref[base + j], 0), jnp.where(valid, 1.0, 0.0))
        return carry

    lax.fori_loop(0, shortest, common, 0)
    lax.fori_loop(shortest, longest, tail, 0)

    part = accs[0].shape[0]

    @pl.when(e == ne - 1)
    def _():
        if not fused:
            copies = [pltpu.make_async_copy(acc, o_hbm.at[b, pl.ds(k * part, part)], sem.at[k])
                      for k, acc in enumerate(accs)]
            for cp in copies:
                cp.start()
            for cp in copies:
                cp.wait()
            return
        tt = RESIDUAL_TILE
        d = nt * LANES
        per_acc = part // (tt * nt)
        n_tiles = SCATTER_SPLIT * per_acc
        nx = RESIDUAL_LOOKAHEAD + 1

        def x_copy(i):
            return pltpu.make_async_copy(x_hbm.at[b, pl.ds(i * tt, tt)], x_buf.at[i % nx], x_sem.at[i % nx])

        def o_copy(i):
            return pltpu.make_async_copy(o_buf.at[i % 2], o_hbm.at[b, pl.ds(i * tt, tt)], o_sem.at[i % 2])

        for i in range(min(RESIDUAL_LOOKAHEAD, n_tiles)):
            x_copy(i).start()
        for i in range(n_tiles):
            x_copy(i).wait()
            if i + RESIDUAL_LOOKAHEAD < n_tiles:
                x_copy(i + RESIDUAL_LOOKAHEAD).start()
            if i >= 2:
                o_copy(i - 2).wait()
            acc = accs[i // per_acc]
            moe = _load_tile_rows(acc.at[pl.ds((i % per_acc) * tt * nt, tt * nt)], tt, d)
            o_buf[i % 2] = x_buf[i % nx] + g_ref[0] * _rms(moe, nw_ref[...])
            o_copy(i).start()
        for i in range(max(n_tiles - 2, 0), n_tiles):
            o_copy(i).wait()


def _scatter(dst_flat, cuts_flat, y_rows, t, nt, residual=None):
    bsz, ne, rows, _ = y_rows.shape
    cap = rows // nt
    d = nt * LANES
    part_rows = (t // SCATTER_SPLIT) * nt
    zeros = jnp.zeros((part_rows, LANES), F32)
    fused = residual is not None
    in_specs = [pl.BlockSpec((1, 1, rows, LANES), lambda b, e, i, c, s: (b, e, 0, 0)),
                pl.BlockSpec(memory_space=pl.ANY)]
    scratch = [pltpu.VMEM((part_rows, LANES), F32)] * SCATTER_SPLIT + [pltpu.SemaphoreType.DMA((SCATTER_SPLIT,))]
    args = (dst_flat, cuts_flat, _tile_row_base(jnp.arange(cap, dtype=I32), nt), y_rows, zeros)
    if fused:
        assert (t // SCATTER_SPLIT) % RESIDUAL_TILE == 0
        in_specs += [pl.BlockSpec(memory_space=pl.ANY), pl.BlockSpec((1, d), lambda b, e, i, c, s: (0, 0)),
                     pl.BlockSpec((1, 1, d), lambda b, e, i, c, s: (b, 0, 0))]
        scratch += [pltpu.VMEM((RESIDUAL_LOOKAHEAD + 1, RESIDUAL_TILE, d), F32), pltpu.VMEM((2, RESIDUAL_TILE, d), F32),
                    pltpu.SemaphoreType.DMA((RESIDUAL_LOOKAHEAD + 1,)), pltpu.SemaphoreType.DMA((2,))]
        args += residual
        out_shape = jax.ShapeDtypeStruct((bsz, t, d), F32)
    else:
        out_shape = jax.ShapeDtypeStruct((bsz, t * nt, LANES), F32)
    return pl.pallas_call(
        functools.partial(_scatter_kernel, cap=cap, ne=ne, nt=nt, fused=fused),
        grid_spec=pltpu.PrefetchScalarGridSpec(
            num_scalar_prefetch=3,
            grid=(bsz, ne),
            in_specs=in_specs,
            out_specs=pl.BlockSpec(memory_space=pl.ANY),
            scratch_shapes=scratch,
        ),
        out_shape=out_shape,
        compiler_params=_cparams(("arbitrary", "arbitrary")),
        name="scatter",
    )(*args)


def _moe(h2_rows, logits, ne, w_gate, w_up, w_down, layer, tf, residual=None):
    bsz, t, _ = logits.shape
    nt = h2_rows.shape[1] // t
    cap = EC_CAPACITY_FACTOR * t // ne
    tpart = t // SCATTER_SPLIT
    idx_cols, gate_cols = _route(logits, ne, cap)
    idx = jnp.swapaxes(idx_cols[:, :, :ne], 1, 2)
    xs = _gather(_tile_row_base(idx, nt).reshape(-1), h2_rows, ne, cap, t)
    y = _ffn(xs, gate_cols, w_gate, w_up, w_down, layer, tf)
    edges = jnp.arange(SCATTER_SPLIT + 1, dtype=I32) * tpart
    cuts = jnp.sum(idx[..., None] < edges, axis=2).astype(I32)
    return _scatter(_tile_row_base(idx % tpart, nt).reshape(-1), cuts.reshape(-1), y, t, nt, residual)


def _sincos_tables(rows, dim):
    quarter = dim // 4
    freqs = jnp.exp(-math.log(10000.0) * jnp.arange(quarter, dtype=F32) / quarter)

    def emb1d(n):
        ang = jnp.arange(n, dtype=F32)[:, None] * freqs[None, :]
        return jnp.concatenate([jnp.sin(ang), jnp.cos(ang)], axis=-1)

    return emb1d(rows), emb1d(GRID_W)


TOKEN_TILE = 256
SGU_TILE = 512
FFN_TILE = 512


def kernel(x, c, ctx, c_ctx, w_ada, b_ada, norm_w, hg_w_in, hg_lb, hg_gnorm, hg_w_out, sg_w_in, sg_ln_w, sg_ln_b,
           sg_w_s, sg_b_s, sg_w_out, moe_router, moe_w_gate, moe_w_up, moe_w_down):
    bsz, t, d = x.shape
    depth = w_ada.shape[0]
    assert depth == 2 and d == N_HEADS * HEAD_DIM and t % GLA_CHUNK == 0 and ctx.shape[1] % GLA_CHUNK == 0
    tm = min(TOKEN_TILE, t)

    cvec = jnp.zeros((SUBLANES, d), F32).at[:bsz].set(c).at[bsz].set(c_ctx)
    mod = _ada(cvec, w_ada, b_ada)

    def mods(layer, rows):
        m = mod[layer, rows].reshape(-1, N_ADA, 1, d)
        return [m[:, k] for k in range(N_ADA)]

    nw = norm_w.reshape(depth, 4, 1, d)
    ne = moe_router.shape[-1]
    router = jnp.zeros((depth, d, LANES), F32).at[:, :, :ne].set(moe_router)
    r_hi = router.astype(BF16)
    r_lo = (router - r_hi.astype(F32)).astype(BF16)

    sh1, sc1, g1, sh2, sc2, g2 = mods(0, slice(0, bsz))
    sh1c, sc1c = [jnp.broadcast_to(m, (bsz, 1, d)) for m in mods(0, slice(bsz, bsz + 1))[:2]]
    w_in_bf = hg_w_in[0].astype(BF16)
    pos = _sincos_tables(t // GRID_W, d)
    tri, masks = _gla_consts()

    cv, ckf, cgf, ckb, cgb, cq, _ = _hg_in(ctx, None, nw[0, 0], sh1c, sc1c, w_in_bf, hg_lb, 1, tm)
    s_zero = jnp.zeros((bsz, 2, N_HEADS, HEAD_DIM, HEAD_DIM), F32)
    _, _, s_ctx = _gla(ckf, cgf, ckb, cgb, cv, cq, s_zero, tri, masks)

    v, kf, gf, kb, gb, q, sg = _hg_in(x, pos, nw[0, 0], sh1, sc1, w_in_bf, hg_lb, 1, tm)
    o_f, o_b, _ = _gla(kf, gf, kb, gb, v, q, s_ctx, tri, masks)
    x1, h2, lg = _hg_out(o_f, o_b, sg, hg_gnorm[0:1], hg_w_out[0].astype(BF16), x, pos, nw[0, 1], g1, nw[0, 2],
                         sh2, sc2, r_hi[0], r_lo[0], tm)
    moe = _moe(h2, lg, ne, moe_w_gate, moe_w_up, moe_w_down, 0, FFN_TILE)

    sh1b, sc1b, g1b, sh2b, sc2b, g2b = mods(1, slice(0, bsz))
    x1b, h2b, lgb = _sgu(x1, moe, nw[0, 3], g2, nw[1, 0], sh1b, sc1b, sg_w_in[0].astype(BF16),
                         sg_ln_w[0:1], sg_ln_b[0:1], sg_w_s[0].astype(BF16), sg_b_s[0].T, sg_w_out[0].astype(BF16),
                         nw[1, 1], g1b, nw[1, 2], sh2b, sc2b, r_hi[1], r_lo[1], SGU_TILE)
    return _moe(h2b, lgb, ne, moe_w_gate, moe_w_up, moe_w_down, 1, FFN_TILE, residual=(x1b, nw[1, 3], g2b))
```

```python
import functools
import math

import jax
import jax.numpy as jnp
from jax import lax
from jax.experimental import pallas as pl
from jax.experimental.pallas import tpu as pltpu

F32 = jnp.float32
BF16 = jnp.bfloat16
I32 = jnp.int32
HIGHEST = lax.Precision.HIGHEST

EPS = 1e-6
GRID_W = 64
N_ADA = 6
N_HEADS = 8
HEAD_DIM = 128
EC_CAPACITY_FACTOR = 2
SGU_CHUNK = 128
SGU_GROUPS = 8

LANES = 128
SUBLANES = 8
GLA_CHUNK = 128
GLA_LEVELS = (64, 32, 16, 8, 4, 2, 1)
GLA_SHORT_BLOCK = 32
GLA_SHORT_MAX_EXPONENT = 60.0
BISECT_ITERS = 160
VMEM_LIMIT = 52 * 1024 * 1024


def _cparams(sem):
    return pltpu.CompilerParams(dimension_semantics=sem, vmem_limit_bytes=VMEM_LIMIT)


def _rms(x, w):
    ms = jnp.mean(x * x, axis=-1, keepdims=True)
    return x * lax.rsqrt(ms + EPS) * w


def _dot(a, b):
    return jnp.dot(a, b, preferred_element_type=F32)


def _dot_nt(a, b):
    return lax.dot_general(a, b, (((1,), (1,)), ((), ())), preferred_element_type=F32)


def _dot_tn(a, b):
    return lax.dot_general(a, b, (((0,), (0,)), ((), ())), preferred_element_type=F32)


def _ada_kernel(c_ref, w_ref, b_ref, o_ref):
    c = c_ref[...]
    s = c * jax.nn.sigmoid(c)
    o_ref[0] = jnp.dot(s, w_ref[0], precision=HIGHEST, preferred_element_type=F32) + b_ref[0]


def _ada(cvec, w_ada, b_ada):
    depth, d, nd = w_ada.shape
    rows = cvec.shape[0]
    return pl.pallas_call(
        _ada_kernel,
        grid=(depth, nd // d),
        in_specs=[pl.BlockSpec((rows, d), lambda l, n: (0, 0)),
                  pl.BlockSpec((1, d, d), lambda l, n: (l, 0, n)),
                  pl.BlockSpec((1, 1, d), lambda l, n: (l, 0, n))],
        out_specs=pl.BlockSpec((1, rows, d), lambda l, n: (l, 0, n)),
        out_shape=jax.ShapeDtypeStruct((depth, rows, nd), F32),
        compiler_params=_cparams(("arbitrary", "arbitrary")),
        name="ada",
    )(cvec, w_ada, b_ada.reshape(depth, 1, nd))


def _hgin_body(x, nw_ref, sh_ref, sc_ref, w_ref, lb_ref, outs, n_lb):
    v_ref, kf_ref, gf_ref, kb_ref, gb_ref, q_ref, sg_ref = outs
    d = x.shape[-1]
    h = _rms(x, nw_ref[...]) * (1.0 + sc_ref[0]) + sh_ref[0]
    hb = h.astype(BF16)
    lbs = lb_ref[...]
    e = jnp.exp(lbs - jnp.max(lbs, axis=0, keepdims=True))
    lb = jnp.sum(e[:n_lb], axis=0) / jnp.sum(e, axis=0)

    v_ref[0] = _dot(hb, w_ref[:, 0:d]).astype(BF16)
    for j, (k_ref, g_ref) in enumerate(((kf_ref, gf_ref), (kb_ref, gb_ref))):
        raw = _dot(hb, w_ref[:, (1 + j) * d:(2 + j) * d])
        lbj = lb[j:j + 1]
        sig = jax.nn.sigmoid(raw)
        f = lbj + (1.0 - lbj) * sig
        k_ref[0] = ((1.0 - lbj) * (1.0 - sig)).astype(BF16)
        g_ref[0] = jnp.log(f)
    qr = _dot(hb, w_ref[:, 3 * d:4 * d])
    q_ref[0] = (qr * jax.nn.sigmoid(qr)).astype(BF16)
    gr = _dot(hb, w_ref[:, 4 * d:5 * d])
    sg_ref[0] = (gr * jax.nn.sigmoid(gr)).astype(BF16)


def _with_pos(x, er_ref, ec_ref):
    er, ec = er_ref[0], ec_ref[...]
    pos = jnp.concatenate([jnp.concatenate([jnp.broadcast_to(er[r:r + 1], ec.shape), ec], axis=1)
                           for r in range(er.shape[0])], axis=0)
    return x + pos


def _pos_specs(pos, tm, d):
    er, ec = pos
    assert tm % GRID_W == 0
    rpt = tm // GRID_W
    specs = [pl.BlockSpec((1, rpt, d // 2), lambda b, i: (i, 0, 0)), pl.BlockSpec(ec.shape, lambda b, i: (0, 0))]
    return specs, (er.reshape(-1, rpt, d // 2), ec)


def _hgin_pos_kernel(x_ref, er_ref, ec_ref, nw_ref, sh_ref, sc_ref, w_ref, lb_ref, *outs, n_lb):
    _hgin_body(_with_pos(x_ref[0], er_ref, ec_ref), nw_ref, sh_ref, sc_ref, w_ref, lb_ref, outs, n_lb)


def _hgin_kernel(x_ref, nw_ref, sh_ref, sc_ref, w_ref, lb_ref, *outs, n_lb):
    _hgin_body(x_ref[0], nw_ref, sh_ref, sc_ref, w_ref, lb_ref, outs, n_lb)


def _hg_in(x, pos, nw, sh, sc, w_bf, lb_raw, n_lb, tm):
    bsz, t, d = x.shape
    tm = min(tm, t)
    tok = pl.BlockSpec((1, tm, d), lambda b, i: (b, i, 0))
    vec = pl.BlockSpec((1, d), lambda b, i: (0, 0))
    bvec = pl.BlockSpec((1, 1, d), lambda b, i: (b, 0, 0))
    wspec = pl.BlockSpec(w_bf.shape, lambda b, i: (0, 0))
    lbspec = pl.BlockSpec(lb_raw.shape, lambda b, i: (0, 0, 0))
    gate_shapes = [jax.ShapeDtypeStruct((bsz, t, d), dt) for dt in (BF16, BF16, F32, BF16, F32, BF16, BF16)]
    if pos is not None:
        kern = functools.partial(_hgin_pos_kernel, n_lb=n_lb)
        pspecs, pargs = _pos_specs(pos, tm, d)
        in_specs = [tok] + pspecs + [vec, bvec, bvec, wspec, lbspec]
        args = (x,) + pargs + (nw, sh, sc, w_bf, lb_raw)
        out_shape = gate_shapes
    else:
        kern = functools.partial(_hgin_kernel, n_lb=n_lb)
        in_specs = [tok, vec, bvec, bvec, wspec, lbspec]
        args = (x, nw, sh, sc, w_bf, lb_raw)
        out_shape = gate_shapes
    return pl.pallas_call(
        kern,
        grid=(bsz, t // tm),
        in_specs=in_specs,
        out_specs=[tok] * len(out_shape),
        out_shape=out_shape,
        compiler_params=_cparams(("arbitrary", "arbitrary")),
        name="hg_in",
    )(*args)


def _gla_consts():
    c = GLA_CHUNK
    t = jnp.arange(c)[:, None]
    s = jnp.arange(c)[None, :]
    tri = jnp.stack([(s <= t), (s >= t)]).astype(BF16)
    masks = []
    for rev in (False, True):
        lv = []
        for m in GLA_LEVELS:
            same = (t // (2 * m)) == (s // (2 * m))
            tq = ((t // m) % 2) == (0 if rev else 1)
            sk = ((s // m) % 2) == (1 if rev else 0)
            lv.append(same & tq & sk)
        lv.append(t == s)
        pair = 2 * GLA_SHORT_BLOCK
        lv.append(((t // pair) == (s // pair)) & ((s >= t) if rev else (s <= t)))
        masks.append(jnp.stack(lv))
    return tri, jnp.stack(masks).astype(F32)


def _split3(x):
    p0 = x.astype(BF16)
    r1 = x - p0.astype(F32)
    p1 = r1.astype(BF16)
    p2 = (r1 - p1.astype(F32)).astype(BF16)
    return p0, p1, p2


def _level_ref(b_scr, hs, m, rev):
    c = GLA_CHUNK
    off = m if rev else m - 1

    def row8(i):
        return jnp.broadcast_to(b_scr[pl.ds(i, 1), hs], (SUBLANES, HEAD_DIM))

    pieces = []
    if m >= SUBLANES:
        for blk in range(c // (2 * m)):
            r8 = row8(blk * 2 * m + off)
            pieces.extend([r8] * (2 * m // SUBLANES))
    else:
        sub = lax.broadcasted_iota(I32, (SUBLANES, HEAD_DIM), 0) // (2 * m)
        for grp in range(c // SUBLANES):
            piece = row8(grp * SUBLANES + off)
            for cls in range(1, SUBLANES // (2 * m)):
                piece = jnp.where(sub == cls, row8(grp * SUBLANES + cls * 2 * m + off), piece)
            pieces.append(piece)
    return jnp.concatenate(pieces, axis=0)


def _block_decay_bound(b_scr, rev):
    c, blk = GLA_CHUNK, GLA_SHORT_BLOCK
    worst = None
    for i in range(c // blk):
        inner = b_scr[pl.ds(i * blk if rev else (i + 1) * blk - 1, 1), :]
        r = (i + 1) * blk if rev else i * blk - 1
        span = jnp.abs(inner - b_scr[pl.ds(r, 1), :]) if 0 <= r < c else jnp.abs(inner)
        worst = span if worst is None else jnp.maximum(worst, span)
    return jnp.max(worst)


def _gla_head(q_ref, k_ref, v_ref, o_ref, mask_ref, st_scr, b_scr, d_idx, rev, h, short):
    c = GLA_CHUNK
    n_lv = len(GLA_LEVELS)
    hs = pl.ds(pl.multiple_of(h * HEAD_DIM, HEAD_DIM), HEAD_DIM)
    b = b_scr[:, hs]
    q = q_ref[0, :, hs]
    k = k_ref[0, :, hs]
    vb = v_ref[0, :, hs]
    bl = b_scr[pl.ds(0 if rev else c - 1, 1), hs]
    st = st_scr[d_idx, h]

    if short:
        blk = 2 * GLA_SHORT_BLOCK
        nb = c // blk
        own = [b_scr[pl.ds(i * blk + (blk // 2 if rev else blk // 2 - 1), 1), hs] for i in range(nb)]

        def blockwise(rows):
            return jnp.concatenate([jnp.broadcast_to(r, (blk, HEAD_DIM)) for r in rows], axis=0)

        u = blockwise(own) - b
        qh = q * jnp.exp(-u).astype(BF16)
        kh = k * jnp.exp(u).astype(BF16)
        qt = qh * blockwise([jnp.exp(r) for r in own]).astype(BF16)
        kt = kh * blockwise([jnp.exp(bl - r) for r in own]).astype(BF16)
        a = jnp.where(mask_ref[d_idx, n_lv + 1] > 0.0, _dot_nt(qh, kh), 0.0)
        zero = jnp.zeros((1, HEAD_DIM), F32)
        for li, m in enumerate(GLA_LEVELS):
            if m < blk:
                continue
            qf, kf = [], []
            for i in range(nb):
                lvl = b_scr[pl.ds((i * blk) // (2 * m) * 2 * m + (m if rev else m - 1), 1), hs]
                is_query = ((i * blk) // m) % 2 == (0 if rev else 1)
                qf.append(jnp.exp(own[i] - lvl) if is_query else zero)
                kf.append(zero if is_query else jnp.exp(lvl - own[i]))
            a = a + mask_ref[d_idx, li] * _dot_nt(qh * blockwise(qf).astype(BF16), kh * blockwise(kf).astype(BF16))
    else:
        qt = q * jnp.exp(b).astype(BF16)
        kt = k * jnp.exp(bl - b).astype(BF16)

        def level(m):
            e = jnp.exp(-jnp.abs(b - _level_ref(b_scr, hs, m, rev))).astype(BF16)
            return _dot_nt(q * e, k * e)

        a = mask_ref[d_idx, n_lv] * _dot_nt(q, k)
        for li, m in enumerate(GLA_LEVELS):
            a = a + mask_ref[d_idx, li] * level(m)
    o_ref[0, :, hs] = (_dot_nt(qt, st.astype(BF16)) + _dot(a.astype(BF16), vb)).astype(o_ref.dtype)
    st_scr[d_idx, h] = st * jnp.exp(bl) + _dot_tn(vb, kt)


def _gla_kernel(kf_ref, gf_ref, vf_ref, qf_ref, kb_ref, gb_ref, vb_ref, qb_ref, s0_ref, tri_ref, mask_ref,
                of_ref, ob_ref, sout_ref, st_scr, bf_scr, bb_scr):
    n = pl.program_id(1)

    @pl.when(n == 0)
    def _():
        st_scr[...] = s0_ref[0]

    dirs = ((qf_ref, kf_ref, vf_ref, gf_ref, of_ref, bf_scr, 0, False),
            (qb_ref, kb_ref, vb_ref, gb_ref, ob_ref, bb_scr, 1, True))
    bounded = []
    for q_ref, k_ref, v_ref, g_ref, o_ref, b_scr, d_idx, rev in dirs:
        p0, p1, _ = _split3(g_ref[0])
        tri = tri_ref[d_idx]
        b_scr[...] = _dot(tri, p0) + _dot(tri, p1)
        bounded.append(_block_decay_bound(b_scr, rev) <= GLA_SHORT_MAX_EXPONENT)

    def heads(which, short, unroll):
        def body(h, carry):
            for q_ref, k_ref, v_ref, _, o_ref, b_scr, d_idx, rev in which:
                _gla_head(q_ref, k_ref, v_ref, o_ref, mask_ref, st_scr, b_scr, d_idx, rev, h, short)
            return carry
        lax.fori_loop(0, N_HEADS, body, 0, unroll=unroll)

    both = jnp.logical_and(bounded[0], bounded[1])

    @pl.when(both)
    def _():
        heads(dirs, True, 8)

    for d, ok in zip(dirs, bounded):
        @pl.when(jnp.logical_and(jnp.logical_not(both), ok))
        def _():
            heads((d,), True, 2)

        @pl.when(jnp.logical_not(ok))
        def _():
            heads((d,), False, 2)

    @pl.when(n == pl.num_programs(1) - 1)
    def _():
        sout_ref[0] = st_scr[...]


def _gla(kf, gf, kb, gb, v, q, s0, tri, masks):
    bsz, t, d = v.shape
    c = GLA_CHUNK
    n = t // c
    fwd = pl.BlockSpec((1, c, d), lambda b, i: (b, i, 0))
    bwd = pl.BlockSpec((1, c, d), lambda b, i: (b, n - 1 - i, 0))
    sspec = pl.BlockSpec((1,) + s0.shape[1:], lambda b, i: (b, 0, 0, 0, 0))
    return pl.pallas_call(
        _gla_kernel,
        grid=(bsz, n),
        in_specs=[fwd, fwd, fwd, fwd, bwd, bwd, bwd, bwd, sspec,
                  pl.BlockSpec(tri.shape, lambda b, i: (0, 0, 0)),
                  pl.BlockSpec(masks.shape, lambda b, i: (0, 0, 0, 0))],
        out_specs=[fwd, bwd, sspec],
        out_shape=[jax.ShapeDtypeStruct((bsz, t, d), BF16), jax.ShapeDtypeStruct((bsz, t, d), BF16),
                   jax.ShapeDtypeStruct(s0.shape, F32)],
        scratch_shapes=[pltpu.VMEM(s0.shape[1:], F32), pltpu.VMEM((c, d), F32), pltpu.VMEM((c, d), F32)],
        compiler_params=_cparams(("arbitrary", "arbitrary")),
        name="gla",
    )(kf, gf, v, q, kb, gb, v, q, s0, tri, masks)


def _store_tile_rows(ref2, x):
    n, d = x.shape
    nt = d // LANES
    for i in range(n // SUBLANES):
        for c in range(nt):
            r0 = (i * nt + c) * SUBLANES
            ref2[r0:r0 + SUBLANES, :] = x[i * SUBLANES:(i + 1) * SUBLANES, c * LANES:(c + 1) * LANES]


def _load_tile_rows(ref2, n, d):
    nt = d // LANES
    rows = []
    for i in range(n // SUBLANES):
        rows.append(jnp.concatenate(
            [ref2[(i * nt + c) * SUBLANES:(i * nt + c + 1) * SUBLANES, :] for c in range(nt)], axis=1))
    return jnp.concatenate(rows, axis=0)


def _tile_row_base(t, nt):
    return (t >> 3) * (nt * SUBLANES) + (t & (SUBLANES - 1))


def _tile_rows_spec(tm, d, index_map):
    return pl.BlockSpec((1, tm * (d // LANES), LANES), index_map)


def _post_mixer(y, xres, rows, nw1_ref, g1_ref, nw2_ref, sh2_ref, sc2_ref, rhi_ref, rlo_ref, x1_ref, h2_ref, lg_ref):
    d = y.shape[-1]
    nt = d // LANES
    x1 = xres + g1_ref[0] * _rms(y, nw1_ref[...])
    x1_ref[0, rows, :] = x1
    h2 = _rms(x1, nw2_ref[...]) * (1.0 + sc2_ref[0]) + sh2_ref[0]
    _store_tile_rows(h2_ref.at[0, pl.ds(rows.start * nt, (rows.stop - rows.start) * nt)], h2)
    h_hi = h2.astype(BF16)
    h_lo = (h2 - h_hi.astype(F32)).astype(BF16)
    lg_ref[0, rows, :] = _dot(h_hi, rhi_ref[...]) + (_dot(h_lo, rhi_ref[...]) + _dot(h_hi, rlo_ref[...]))


def _hgout_kernel(of_ref, ob_ref, sg_ref, gn_ref, w_ref, x_ref, er_ref, ec_ref, nw1_ref, g1_ref, nw2_ref, sh2_ref,
                  sc2_ref, rhi_ref, rlo_ref, x1_ref, h2_ref, lg_ref, z_scr):
    o = of_ref[0].astype(F32) + ob_ref[0].astype(F32)
    for h in range(N_HEADS):
        hs = slice(h * HEAD_DIM, (h + 1) * HEAD_DIM)
        oh = o[:, hs]
        ms = jnp.mean(oh * oh, axis=-1, keepdims=True)
        z = oh * lax.rsqrt(ms + EPS) * gn_ref[:, hs] * sg_ref[0, :, hs].astype(F32)
        z_scr[:, hs] = z.astype(BF16)
    y = _dot(z_scr[...], w_ref[...])
    _post_mixer(y, _with_pos(x_ref[0], er_ref, ec_ref), slice(0, y.shape[0]), nw1_ref, g1_ref, nw2_ref, sh2_ref,
                sc2_ref, rhi_ref, rlo_ref, x1_ref, h2_ref, lg_ref)


def _hg_out(o_f, o_b, sg, gnorm, w_out_bf, x, pos, nw1, g1, nw2, sh2, sc2, r_hi, r_lo, tm):
    bsz, t, d = x.shape
    rspec = pl.BlockSpec((d, LANES), lambda b, i: (0, 0))
    pspecs, pargs = _pos_specs(pos, tm, d)
    tok = pl.BlockSpec((1, tm, d), lambda b, i: (b, i, 0))
    vec = pl.BlockSpec((1, d), lambda b, i: (0, 0))
    bvec = pl.BlockSpec((1, 1, d), lambda b, i: (b, 0, 0))
    return pl.pallas_call(
        _hgout_kernel,
        grid=(bsz, t // tm),
        in_specs=[tok, tok, tok, vec, pl.BlockSpec((d, d), lambda b, i: (0, 0)), tok] + pspecs
        + [vec, bvec, vec, bvec, bvec, rspec, rspec],
        out_specs=[tok, _tile_rows_spec(tm, d, lambda b, i: (b, i, 0)),
                   pl.BlockSpec((1, tm, LANES), lambda b, i: (b, i, 0))],
        out_shape=[jax.ShapeDtypeStruct((bsz, t, d), F32), jax.ShapeDtypeStruct((bsz, t * (d // LANES), LANES), F32),
                   jax.ShapeDtypeStruct((bsz, t, LANES), F32)],
        scratch_shapes=[pltpu.VMEM((tm, d), BF16)],
        compiler_params=_cparams(("arbitrary", "arbitrary")),
        name="hg_out",
    )(o_f, o_b, sg, gnorm, w_out_bf, x, *pargs, nw1, g1, nw2, sh2, sc2, r_hi, r_lo)


def _sgu_kernel(x1p_ref, moe_ref, nw3p_ref, g2p_ref, nw0_ref, sh1_ref, sc1_ref, win_ref, lnw_ref, lnb_ref,
                ws_ref, bs_ref, wout_ref, nw1_ref, g1_ref, nw2_ref, sh2_ref, sc2_ref, rhi_ref, rlo_ref,
                x1_ref, h2_ref, lg_ref):
    tm, d = x1p_ref.shape[1], x1p_ref.shape[2]
    w = lnw_ref.shape[-1]
    gd = w // SGU_GROUPS
    nt = d // LANES

    def gelu(z):
        return 0.5 * z * (1.0 + jnp.tanh(math.sqrt(2.0 / math.pi) * (z + 0.044715 * (z * z * z))))

    for ck in range(tm // SGU_CHUNK):
        rs = slice(ck * SGU_CHUNK, (ck + 1) * SGU_CHUNK)
        moe = _load_tile_rows(moe_ref.at[0, pl.ds(ck * SGU_CHUNK * nt, SGU_CHUNK * nt)], SGU_CHUNK, d)
        x = x1p_ref[0, rs, :] + g2p_ref[0] * _rms(moe, nw3p_ref[...])
        h = _rms(x, nw0_ref[...]) * (1.0 + sc1_ref[0]) + sh1_ref[0]
        hb = h.astype(BF16)
        u = gelu(_dot(hb, win_ref[:, 0:w]))
        v = gelu(_dot(hb, win_ref[:, w:2 * w]))
        mu = jnp.mean(v, axis=-1, keepdims=True)
        vc = v - mu
        vn = vc * lax.rsqrt(jnp.mean(vc * vc, axis=-1, keepdims=True) + EPS) * lnw_ref[...] + lnb_ref[...]
        vnb = vn.astype(BF16)
        gated = []
        for g in range(SGU_GROUPS):
            cs = slice(g * gd, (g + 1) * gd)
            mixed = _dot(ws_ref[g], vnb[:, cs]) + bs_ref[:, g:g + 1]
            gated.append((u[:, cs] * mixed).astype(BF16))
        y = _dot(jnp.concatenate(gated, axis=1), wout_ref[...])
        _post_mixer(y, x, rs, nw1_ref, g1_ref, nw2_ref, sh2_ref, sc2_ref, rhi_ref, rlo_ref, x1_ref, h2_ref, lg_ref)


def _sgu(x1p, moe, nw3p, g2p, nw0, sh1, sc1, w_in_bf, ln_w, ln_b, w_s_bf, b_s_t, w_out_bf,
         nw1, g1, nw2, sh2, sc2, r_hi, r_lo, tm):
    bsz, t, d = x1p.shape
    w = ln_w.shape[-1]
    tok = pl.BlockSpec((1, tm, d), lambda b, i: (b, i, 0))
    tiles = _tile_rows_spec(tm, d, lambda b, i: (b, i, 0))
    vec = pl.BlockSpec((1, d), lambda b, i: (0, 0))
    wvec = pl.BlockSpec((1, w), lambda b, i: (0, 0))
    bvec = pl.BlockSpec((1, 1, d), lambda b, i: (b, 0, 0))

    def full(a):
        return pl.BlockSpec(a.shape, lambda b, i: (0,) * a.ndim)

    return pl.pallas_call(
        _sgu_kernel,
        grid=(bsz, t // tm),
        in_specs=[tok, tiles, vec, bvec, vec, bvec, bvec, full(w_in_bf), wvec, wvec, full(w_s_bf), full(b_s_t),
                  full(w_out_bf), vec, bvec, vec, bvec, bvec, full(r_hi), full(r_lo)],
        out_specs=[tok, tiles, pl.BlockSpec((1, tm, LANES), lambda b, i: (b, i, 0))],
        out_shape=[jax.ShapeDtypeStruct((bsz, t, d), F32), jax.ShapeDtypeStruct((bsz, t * (d // LANES), LANES), F32),
                   jax.ShapeDtypeStruct((bsz, t, LANES), F32)],
        compiler_params=_cparams(("arbitrary", "arbitrary")),
        name="sgu",
    )(x1p, moe, nw3p, g2p, nw0, sh1, sc1, w_in_bf, ln_w, ln_b, w_s_bf, b_s_t, w_out_bf,
      nw1, g1, nw2, sh2, sc2, r_hi, r_lo)


def _token_prefix(mask, triu, slow):
    local = _dot(mask.astype(BF16), triu)
    rowtot = jnp.broadcast_to(local[:, LANES - 1:LANES], local.shape)
    prev = _dot(slow, rowtot.astype(BF16))
    return local, prev, rowtot


def _route_kernel(lg_ref, triu_ref, slow_ref, idx_ref, gate_ref, a_scr, thr_scr, *, cap):
    ne, nr = a_scr.shape[0], a_scr.shape[1]
    for r in range(nr):
        a_scr[:, r, :] = lg_ref[0, r * LANES:(r + 1) * LANES, :].T[0:ne, :]
    lg = a_scr[...]
    ex = jnp.exp(lg - jnp.max(lg, axis=0, keepdims=True))
    a = ex / jnp.sum(ex, axis=0, keepdims=True)
    a_scr[...] = a
    capf = jnp.float32(cap)

    def count(m):
        return jnp.sum(jnp.sum(m.astype(F32), axis=1, keepdims=True), axis=2, keepdims=True)

    def unresolved(state):
        it, lo, hi = state
        mid = 0.5 * (lo + hi)
        return jnp.logical_and(it < BISECT_ITERS, jnp.max(((mid != lo) & (mid != hi)).astype(F32)) > 0.0)

    def bis(state):
        it, lo, hi = state
        mid = 0.5 * (lo + hi)
        ge = count(a >= mid) >= capf
        return it + 1, jnp.where(ge, mid, lo), jnp.where(ge, hi, mid)

    _, lo, _ = lax.while_loop(unresolved, bis, (jnp.int32(0), jnp.zeros((ne, 1, 1), F32), jnp.full((ne, 1, 1), 2.0, F32)))
    thr_scr[...] = jnp.broadcast_to(lo, thr_scr.shape)

    idx_ref[0] = jnp.zeros(idx_ref.shape[1:], I32)
    gate_ref[0] = jnp.zeros(gate_ref.shape[1:], F32)
    triu = triu_ref[...]
    slow = slow_ref[...]
    lane = lax.broadcasted_iota(I32, (LANES, LANES), 1)
    lane_f = lane.astype(F32)
    sub_f = lax.broadcasted_iota(I32, (LANES, LANES), 0).astype(F32)
    rowid = lax.broadcasted_iota(I32, (nr, LANES), 0).astype(F32)

    def per_expert(e, carry):
        ae = a_scr[e]
        v = thr_scr[e][0:1, :]
        gt = ae > v
        eq = ae == v
        need = capf - jnp.sum(jnp.sum(gt.astype(F32), axis=0, keepdims=True), axis=1, keepdims=True)
        eql, eqp, _ = _token_prefix(eq, triu, slow)
        sel = gt | (eq & ((eql + eqp - eq.astype(F32)) < need))
        local, prev, rowtot = _token_prefix(sel, triu, slow)
        rowcum = prev + rowtot
        prev_hi = jnp.floor(prev * (1.0 / LANES))
        a0, a1, a2 = _split3(ae)
        rhs = jnp.concatenate([piece.astype(BF16) for piece in
                               (local, prev_hi, prev - LANES * prev_hi, rowid, a0, a1, a2)], axis=1)

        for p in range(cap // LANES):
            base = float(p * LANES + 1)
            slot_row = base + lane_f[0:1, :]
            onehot_t = ((prev < slot_row) & (rowcum >= slot_row)).astype(BF16)
            g = _dot_tn(onehot_t, rhs)
            g_local = g[:, 0:LANES]
            g_prev = LANES * g[:, LANES:2 * LANES] + g[:, 2 * LANES:3 * LANES]
            g_row = g[:, 3 * LANES:4 * LANES]
            g_a = g[:, 4 * LANES:5 * LANES] + g[:, 5 * LANES:6 * LANES] + g[:, 6 * LANES:7 * LANES]
            slot_col = base + sub_f
            lstar = jnp.sum(((g_local + g_prev) < slot_col).astype(F32), axis=1, keepdims=True)
            tok = LANES * g_row[:, 0:1] + lstar
            gat = jnp.sum(jnp.where(lane_f == lstar, g_a, 0.0), axis=1, keepdims=True)
            rs = slice(p * LANES, (p + 1) * LANES)
            idx_ref[0, rs, :] = jnp.where(lane == e, tok.astype(I32), idx_ref[0, rs, :])
            gate_ref[0, rs, :] = jnp.where(lane == e, gat, gate_ref[0, rs, :])
        return carry

    lax.fori_loop(0, ne, per_expert, 0)


def _route(logits, ne, cap):
    bsz, t, _ = logits.shape
    nr = t // LANES
    assert cap % LANES == 0 and nr % SUBLANES == 0
    triu = (jnp.arange(LANES)[:, None] <= jnp.arange(LANES)[None, :]).astype(BF16)
    slow = (jnp.arange(nr)[:, None] > jnp.arange(nr)[None, :]).astype(BF16)
    idx, gate = pl.pallas_call(
        functools.partial(_route_kernel, cap=cap),
        grid=(bsz,),
        in_specs=[pl.BlockSpec((1, t, LANES), lambda b: (b, 0, 0)),
                  pl.BlockSpec((LANES, LANES), lambda b: (0, 0)), pl.BlockSpec((nr, nr), lambda b: (0, 0))],
        out_specs=[pl.BlockSpec((1, cap, LANES), lambda b: (b, 0, 0))] * 2,
        out_shape=[jax.ShapeDtypeStruct((bsz, cap, LANES), I32), jax.ShapeDtypeStruct((bsz, cap, LANES), F32)],
        scratch_shapes=[pltpu.VMEM((ne, nr, LANES), F32), pltpu.VMEM((ne, SUBLANES, LANES), F32)],
        compiler_params=_cparams(("arbitrary",)),
        name="route",
    )(logits, triu, slow)
    return idx, gate


def _gather_kernel(src_ref, h_hbm, o_ref, h_scr, row_scr, sem, *, cap, ne, nt):
    b, e = pl.program_id(0), pl.program_id(1)

    @pl.when(e == 0)
    def _():
        cp = pltpu.make_async_copy(h_hbm.at[b], h_scr, sem.at[0])
        cp.start()
        cp.wait()

    base = (b * ne + e) * cap
    group = nt * SUBLANES

    def body(jj, carry):
        for u in range(SUBLANES):
            row_scr[pl.ds(jj * group + u, nt, stride=SUBLANES), :] = \
                h_scr[pl.ds(src_ref[base + jj * SUBLANES + u], nt, stride=SUBLANES), :]
        return carry

    lax.fori_loop(0, cap // SUBLANES, body, 0, unroll=4)

    def emit(g, carry):
        rows = [jnp.concatenate([row_scr[pl.ds((g * 2 + i) * group + c * SUBLANES, SUBLANES), :] for c in range(nt)],
                                axis=1) for i in range(2)]
        o_ref[0, 0, pl.ds(g * 2 * SUBLANES, 2 * SUBLANES), :] = jnp.concatenate(rows, axis=0).astype(BF16)
        return carry

    lax.fori_loop(0, cap // (2 * SUBLANES), emit, 0, unroll=4)


def _gather(src_flat, h2_rows, ne, cap, t):
    bsz, rows, _ = h2_rows.shape
    nt = rows // t
    return pl.pallas_call(
        functools.partial(_gather_kernel, cap=cap, ne=ne, nt=nt),
        grid_spec=pltpu.PrefetchScalarGridSpec(
            num_scalar_prefetch=1,
            grid=(bsz, ne),
            in_specs=[pl.BlockSpec(memory_space=pl.ANY)],
            out_specs=pl.BlockSpec((1, 1, cap, nt * LANES), lambda b, e, src: (b, e, 0, 0)),
            scratch_shapes=[pltpu.VMEM((rows, LANES), F32), pltpu.VMEM((cap * nt, LANES), F32),
                            pltpu.SemaphoreType.DMA((1,))],
        ),
        out_shape=jax.ShapeDtypeStruct((bsz, ne, cap, nt * LANES), BF16),
        compiler_params=_cparams(("arbitrary", "arbitrary")),
        name="gather",
    )(src_flat, h2_rows)


def _ffn_kernel(xs_ref, gate_ref, wg_ref, wu_ref, wd_ref, y_ref):
    e, f = pl.program_id(0), pl.program_id(1)
    cap, d = xs_ref.shape[2], xs_ref.shape[3]
    nt = d // LANES

    @pl.when(f == 0)
    def _():
        y_ref[...] = jnp.zeros(y_ref.shape, F32)

    wg = wg_ref[0, 0].astype(BF16)
    wu = wu_ref[0, 0].astype(BF16)
    wd = wd_ref[0, 0].astype(BF16)
    lane = lax.broadcasted_iota(I32, (cap, LANES), 1)
    for b in range(xs_ref.shape[0]):
        xb = xs_ref[b, 0]
        g = _dot(xb, wg)
        u = _dot(xb, wu)
        gate = jnp.sum(jnp.where(lane == e, gate_ref[b], 0.0), axis=1, keepdims=True)
        hid = (g * jax.nn.sigmoid(g) * u * gate).astype(BF16)
        part = _dot(hid, wd)
        for i in range(cap // SUBLANES):
            for c in range(nt):
                r0 = (i * nt + c) * SUBLANES
                y_ref[b, 0, r0:r0 + SUBLANES, :] += part[i * SUBLANES:(i + 1) * SUBLANES, c * LANES:(c + 1) * LANES]


def _ffn(xs, gate_cols, w_gate, w_up, w_down, layer, tf):
    bsz, ne, cap, d = xs.shape
    ff = w_gate.shape[-1]
    tf = min(tf, ff)
    nt = d // LANES
    return pl.pallas_call(
        _ffn_kernel,
        grid=(ne, ff // tf),
        in_specs=[pl.BlockSpec((bsz, 1, cap, d), lambda e, f: (0, e, 0, 0)),
                  pl.BlockSpec((bsz, cap, LANES), lambda e, f: (0, 0, 0)),
                  pl.BlockSpec((1, 1, d, tf), lambda e, f: (layer, e, 0, f)),
                  pl.BlockSpec((1, 1, d, tf), lambda e, f: (layer, e, 0, f)),
                  pl.BlockSpec((1, 1, tf, d), lambda e, f: (layer, e, f, 0))],
        out_specs=pl.BlockSpec((bsz, 1, cap * nt, LANES), lambda e, f: (0, e, 0, 0)),
        out_shape=jax.ShapeDtypeStruct((bsz, ne, cap * nt, LANES), F32),
        compiler_params=_cparams(("arbitrary", "arbitrary")),
        name="ffn",
    )(xs, gate_cols, w_gate, w_up, w_down)


SCATTER_SPLIT = 8


RESIDUAL_TILE = 256
RESIDUAL_LOOKAHEAD = 3


def _scatter_kernel(dst_ref, cut_ref, src_ref, y_ref, zero_hbm, *rest, cap, ne, nt, fused):
    if fused:
        x_hbm, nw_ref, g_ref, o_hbm = rest[:4]
        accs = rest[4:4 + SCATTER_SPLIT]
        sem, x_buf, o_buf, x_sem, o_sem = rest[4 + SCATTER_SPLIT:]
    else:
        o_hbm = rest[0]
        accs, sem = rest[1:1 + SCATTER_SPLIT], rest[1 + SCATTER_SPLIT]
    b, e = pl.program_id(0), pl.program_id(1)

    @pl.when(e == 0)
    def _():
        fills = [pltpu.make_async_copy(zero_hbm, acc, sem.at[k]) for k, acc in enumerate(accs)]
        for cp in fills:
            cp.start()
        for cp in fills:
            cp.wait()

    base = (b * ne + e) * cap
    cbase = (b * ne + e) * (SCATTER_SPLIT + 1)
    starts = [cut_ref[cbase + k] for k in range(SCATTER_SPLIT)]
    counts = [cut_ref[cbase + k + 1] - starts[k] for k in range(SCATTER_SPLIT)]
    shortest = functools.reduce(jnp.minimum, counts)
    longest = functools.reduce(jnp.maximum, counts)

    def update(acc, j, dst, scale=None):
        src = src_ref[j]
        row = y_ref[0, 0, pl.ds(src, nt, stride=SUBLANES), :]
        acc[pl.ds(dst, nt, stride=SUBLANES), :] += row if scale is None else scale * row

    def common(i, carry):
        for k, acc in enumerate(accs):
            j = starts[k] + i
            update(acc, j, dst_ref[base + j])
        return carry

    def tail(i, carry):
        for k, acc in enumerate(accs):
            valid = i < counts[k]
            j = jnp.minimum(starts[k] + i, cap - 1)
            update(acc, j, jnp.where(valid, dst_ref[base + j], 0), jnp.where(valid, 1.0, 0.0))
        return carry

    lax.fori_loop(0, shortest, common, 0)
    lax.fori_loop(shortest, longest, tail, 0)

    part = accs[0].shape[0]

    @pl.when(e == ne - 1)
    def _():
        if not fused:
            copies = [pltpu.make_async_copy(acc, o_hbm.at[b, pl.ds(k * part, part)], sem.at[k])
                      for k, acc in enumerate(accs)]
            for cp in copies:
                cp.start()
            for cp in copies:
                cp.wait()
            return
        tt = RESIDUAL_TILE
        d = nt * LANES
        per_acc = part // (tt * nt)
        n_tiles = SCATTER_SPLIT * per_acc
        nx = RESIDUAL_LOOKAHEAD + 1

        def x_copy(i):
            return pltpu.make_async_copy(x_hbm.at[b, pl.ds(i * tt, tt)], x_buf.at[i % nx], x_sem.at[i % nx])

        def o_copy(i):
            return pltpu.make_async_copy(o_buf.at[i % 2], o_hbm.at[b, pl.ds(i * tt, tt)], o_sem.at[i % 2])

        for i in range(min(RESIDUAL_LOOKAHEAD, n_tiles)):
            x_copy(i).start()
        for i in range(n_tiles):
            x_copy(i).wait()
            if i + RESIDUAL_LOOKAHEAD < n_tiles:
                x_copy(i + RESIDUAL_LOOKAHEAD).start()
            if i >= 2:
                o_copy(i - 2).wait()
            acc = accs[i // per_acc]
            moe = _load_tile_rows(acc.at[pl.ds((i % per_acc) * tt * nt, tt * nt)], tt, d)
            o_buf[i % 2] = x_buf[i % nx] + g_ref[0] * _rms(moe, nw_ref[...])
            o_copy(i).start()
        for i in range(max(n_tiles - 2, 0), n_tiles):
            o_copy(i).wait()


def _scatter(dst_flat, cuts_flat, y_rows, t, nt, residual=None):
    bsz, ne, rows, _ = y_rows.shape
    cap = rows // nt
    d = nt * LANES
    part_rows = (t // SCATTER_SPLIT) * nt
    zeros = jnp.zeros((part_rows, LANES), F32)
    fused = residual is not None
    in_specs = [pl.BlockSpec((1, 1, rows, LANES), lambda b, e, i, c, s: (b, e, 0, 0)),
                pl.BlockSpec(memory_space=pl.ANY)]
    scratch = [pltpu.VMEM((part_rows, LANES), F32)] * SCATTER_SPLIT + [pltpu.SemaphoreType.DMA((SCATTER_SPLIT,))]
    args = (dst_flat, cuts_flat, _tile_row_base(jnp.arange(cap, dtype=I32), nt), y_rows, zeros)
    if fused:
        assert (t // SCATTER_SPLIT) % RESIDUAL_TILE == 0
        in_specs += [pl.BlockSpec(memory_space=pl.ANY), pl.BlockSpec((1, d), lambda b, e, i, c, s: (0, 0)),
                     pl.BlockSpec((1, 1, d), lambda b, e, i, c, s: (b, 0, 0))]
        scratch += [pltpu.VMEM((RESIDUAL_LOOKAHEAD + 1, RESIDUAL_TILE, d), F32), pltpu.VMEM((2, RESIDUAL_TILE, d), F32),
                    pltpu.SemaphoreType.DMA((RESIDUAL_LOOKAHEAD + 1,)), pltpu.SemaphoreType.DMA((2,))]
        args += residual
        out_shape = jax.ShapeDtypeStruct((bsz, t, d), F32)
    else:
        out_shape = jax.ShapeDtypeStruct((bsz, t * nt, LANES), F32)
    return pl.pallas_call(
        functools.partial(_scatter_kernel, cap=cap, ne=ne, nt=nt, fused=fused),
        grid_spec=pltpu.PrefetchScalarGridSpec(
            num_scalar_prefetch=3,
            grid=(bsz, ne),
            in_specs=in_specs,
            out_specs=pl.BlockSpec(memory_space=pl.ANY),
            scratch_shapes=scratch,
        ),
        out_shape=out_shape,
        compiler_params=_cparams(("arbitrary", "arbitrary")),
        name="scatter",
    )(*args)


def _moe(h2_rows, logits, ne, w_gate, w_up, w_down, layer, tf, residual=None):
    bsz, t, _ = logits.shape
    nt = h2_rows.shape[1] // t
    cap = EC_CAPACITY_FACTOR * t // ne
    tpart = t // SCATTER_SPLIT
    idx_cols, gate_cols = _route(logits, ne, cap)
    idx = jnp.swapaxes(idx_cols[:, :, :ne], 1, 2)
    xs = _gather(_tile_row_base(idx, nt).reshape(-1), h2_rows, ne, cap, t)
    y = _ffn(xs, gate_cols, w_gate, w_up, w_down, layer, tf)
    edges = jnp.arange(SCATTER_SPLIT + 1, dtype=I32) * tpart
    cuts = jnp.sum(idx[..., None] < edges, axis=2).astype(I32)
    return _scatter(_tile_row_base(idx % tpart, nt).reshape(-1), cuts.reshape(-1), y, t, nt, residual)


def _sincos_tables(rows, dim):
    quarter = dim // 4
    freqs = jnp.exp(-math.log(10000.0) * jnp.arange(quarter, dtype=F32) / quarter)

    def emb1d(n):
        ang = jnp.arange(n, dtype=F32)[:, None] * freqs[None, :]
        return jnp.concatenate([jnp.sin(ang), jnp.cos(ang)], axis=-1)

    return emb1d(rows), emb1d(GRID_W)


TOKEN_TILE = 256
SGU_TILE = 512
FFN_TILE = 512


def kernel(x, c, ctx, c_ctx, w_ada, b_ada, norm_w, hg_w_in, hg_lb, hg_gnorm, hg_w_out, sg_w_in, sg_ln_w, sg_ln_b,
           sg_w_s, sg_b_s, sg_w_out, moe_router, moe_w_gate, moe_w_up, moe_w_down):
    bsz, t, d = x.shape
    depth = w_ada.shape[0]
    assert depth == 2 and d == N_HEADS * HEAD_DIM and t % GLA_CHUNK == 0 and ctx.shape[1] % GLA_CHUNK == 0
    tm = min(TOKEN_TILE, t)

    cvec = jnp.zeros((SUBLANES, d), F32).at[:bsz].set(c).at[bsz].set(c_ctx)
    mod = _ada(cvec, w_ada, b_ada)

    def mods(layer, rows):
        m = mod[layer, rows].reshape(-1, N_ADA, 1, d)
        return [m[:, k] for k in range(N_ADA)]

    nw = norm_w.reshape(depth, 4, 1, d)
    ne = moe_router.shape[-1]
    router = jnp.zeros((depth, d, LANES), F32).at[:, :, :ne].set(moe_router)
    r_hi = router.astype(BF16)
    r_lo = (router - r_hi.astype(F32)).astype(BF16)

    sh1, sc1, g1, sh2, sc2, g2 = mods(0, slice(0, bsz))
    sh1c, sc1c = [jnp.broadcast_to(m, (bsz, 1, d)) for m in mods(0, slice(bsz, bsz + 1))[:2]]
    w_in_bf = hg_w_in[0].astype(BF16)
    pos = _sincos_tables(t // GRID_W, d)
    tri, masks = _gla_consts()

    cv, ckf, cgf, ckb, cgb, cq, _ = _hg_in(ctx, None, nw[0, 0], sh1c, sc1c, w_in_bf, hg_lb, 1, tm)
    s_zero = jnp.zeros((bsz, 2, N_HEADS, HEAD_DIM, HEAD_DIM), F32)
    _, _, s_ctx = _gla(ckf, cgf, ckb, cgb, cv, cq, s_zero, tri, masks)

    v, kf, gf, kb, gb, q, sg = _hg_in(x, pos, nw[0, 0], sh1, sc1, w_in_bf, hg_lb, 1, tm)
    o_f, o_b, _ = _gla(kf, gf, kb, gb, v, q, s_ctx, tri, masks)
    x1, h2, lg = _hg_out(o_f, o_b, sg, hg_gnorm[0:1], hg_w_out[0].astype(BF16), x, pos, nw[0, 1], g1, nw[0, 2],
                         sh2, sc2, r_hi[0], r_lo[0], tm)
    moe = _moe(h2, lg, ne, moe_w_gate, moe_w_up, moe_w_down, 0, FFN_TILE)

    sh1b, sc1b, g1b, sh2b, sc2b, g2b = mods(1, slice(0, bsz))
    x1b, h2b, lgb = _sgu(x1, moe, nw[0, 3], g2, nw[1, 0], sh1b, sc1b, sg_w_in[0].astype(BF16),
                         sg_ln_w[0:1], sg_ln_b[0:1], sg_w_s[0].astype(BF16), sg_b_s[0].T, sg_w_out[0].astype(BF16),
                         nw[1, 1], g1b, nw[1, 2], sh2b, sc2b, r_hi[1], r_lo[1], SGU_TILE)
    return _moe(h2b, lgb, ne, moe_w_gate, moe_w_up, moe_w_down, 1, FFN_TILE, residual=(x1b, nw[1, 3], g2b))
```
